```python
import math, functools
import jax, jax.numpy as jnp
from jax import lax
import numpy as np

D_MODEL = 1024
BATCH = 2
SEQ = 16384
DEPTH = 4

GRID_W = 64
CTX_LEN = 256
N_MIXERS = 3
N_A_LAYERS = (DEPTH + 2) // 3
N_B_LAYERS = (DEPTH + 1) // 3
N_C_LAYERS = DEPTH // 3
DEEPNORM_ALPHA = (2.0 * DEPTH) ** 0.25
DEEPNORM_BETA = (8.0 * DEPTH) ** -0.25
LN_EPS = 1e-6
RMS_EPS = 1e-6
NEG_INF = -1e30
ROPE_THETA = 10000.0
ATT_BLOCK = 128
MOD_SCALE = 0.5
S5_GROUP_CH = 16
S5_GROUPS = D_MODEL // S5_GROUP_CH
S5_STATE = 64
S5_DT_MIN = 0.001
S5_DT_MAX = 0.1
SCAN_CHUNK = 128
WINDOW = 128
HALO = ((WINDOW + ATT_BLOCK - 1) // ATT_BLOCK) * ATT_BLOCK
HEAD_DIM_B = 64
HEADS_B = D_MODEL // HEAD_DIM_B
KV_HEADS_B = 2
HEAD_DIM_C = 128
HEADS_C = D_MODEL // HEAD_DIM_C
KV_HEADS_C = 2
N_EXPERTS = 16
N_EXPERT_GROUPS = 4
EXPERTS_PER_GROUP = N_EXPERTS // N_EXPERT_GROUPS
TOP_K = 2
D_EXPERT = D_MODEL // 2

kernel_name = 'hybrid_s5_swa_axialgqa_moe_dit'


def _layer_norm(x, g, b):
    xf = x.astype(jnp.float32)
    xc = xf - xf.mean(-1, keepdims=True)
    var = jnp.mean(xc * xc, -1, keepdims=True)
    return (xc * lax.rsqrt(var + LN_EPS) * g.astype(jnp.float32) + b.astype(jnp.float32)).astype(x.dtype)


def _rms_norm(x, g):
    xf = x.astype(jnp.float32)
    return (xf * lax.rsqrt(jnp.mean(xf * xf, -1, keepdims=True) + RMS_EPS) * g.astype(jnp.float32)).astype(x.dtype)


def _modulation(cond, w, b):
    return jnp.split(jax.nn.silu(cond) @ w + b, 6, axis=-1)


def _axial_rope_tables(n_rows, head_dim):
    quarter = head_dim // 4
    inv_freq = ROPE_THETA ** (-jnp.arange(quarter, dtype=jnp.float32) / quarter)
    rows = jnp.repeat(jnp.arange(n_rows, dtype=jnp.float32), GRID_W)
    cols = jnp.tile(jnp.arange(GRID_W, dtype=jnp.float32), n_rows)
    ang = jnp.stack([rows[:, None] * inv_freq, cols[:, None] * inv_freq], axis=1)
    return jnp.cos(ang), jnp.sin(ang)


def _apply_axial_rope(x, cos, sin):
    Bsz, S, H, Dh = x.shape
    xr = x.reshape(Bsz, S, H, 2, 2, Dh // 4)
    x1, x2 = xr[..., 0, :], xr[..., 1, :]
    c = cos[None, :, None].astype(x.dtype)
    s = sin[None, :, None].astype(x.dtype)
    return jnp.stack([x1 * c - x2 * s, x2 * c + x1 * s], axis=-2).reshape(x.shape)


def _split_qkv(proj, n_heads, n_kv, head_dim):
    Bsz, L, _ = proj.shape
    q, k, v = jnp.split(proj, [n_heads * head_dim, (n_heads + n_kv) * head_dim], axis=-1)
    return (q.reshape(Bsz, L, n_heads, head_dim), k.reshape(Bsz, L, n_kv, head_dim),
            v.reshape(Bsz, L, n_kv, head_dim))


def _group_heads(q, n_kv):
    Bsz, L, H, Dh = q.shape
    return q.reshape(Bsz, L, n_kv, H // n_kv, Dh)


def _to_query_blocks(q, n_kv):
    Bsz, S, H, Dh = q.shape
    qb = q.reshape(Bsz, S // ATT_BLOCK, ATT_BLOCK, n_kv, H // n_kv, Dh)
    return qb.transpose(1, 0, 2, 3, 4, 5)


def _from_query_blocks(o):
    nb, Bsz, T, KV, R, Dh = o.shape
    return o.transpose(1, 0, 2, 3, 4, 5).reshape(Bsz, nb * T, KV * R * Dh)


def _attend(q, kv_groups, sink):
    scale = q.shape[-1] ** -0.5
    scores = []
    for k, v, mask in kv_groups:
        s = jnp.einsum('btkrd,blkd->bkrtl', q, k, preferred_element_type=jnp.float32) * scale
        if mask is not None:
            s = jnp.where(mask, s, NEG_INF)
        scores.append(s)
    m = functools.reduce(jnp.maximum, [s.max(-1, keepdims=True) for s in scores])
    if sink is not None:
        sink_logit = sink.astype(jnp.float32)[None, :, :, None, None]
        m = jnp.maximum(m, sink_logit)
    probs = [jnp.exp(s - m) for s in scores]
    denom = functools.reduce(jnp.add, [p.sum(-1, keepdims=True) for p in probs])
    if sink is not None:
        denom = denom + jnp.exp(sink_logit - m)
    inv = 1.0 / denom
    outs = [jnp.einsum('bkrtl,blkd->btkrd', (p * inv).astype(grp[1].dtype), grp[1])
            for p, grp in zip(probs, kv_groups)]
    return functools.reduce(jnp.add, outs)


def _complex_linear_combine(e1, e2):
    a1r, a1i, b1r, b1i = e1
    a2r, a2i, b2r, b2i = e2
    return (a2r * a1r - a2i * a1i, a2r * a1i + a2i * a1r,
            a2r * b1r - a2i * b1i + b2r, a2r * b1i + a2i * b1r + b2i)


def _s5_discretise(a_re, a_im, log_dt, b_re, b_im):
    f32 = jnp.float32
    a_re, a_im, b_re, b_im = a_re.astype(f32), a_im.astype(f32), b_re.astype(f32), b_im.astype(f32)
    dt = jnp.exp(log_dt.astype(f32))[:, None]
    mag = jnp.exp(a_re * dt)
    lam_re, lam_im = mag * jnp.cos(a_im * dt), mag * jnp.sin(a_im * dt)
    inv_den = 1.0 / (a_re * a_re + a_im * a_im)
    n_re = lam_re - 1.0
    f_re = (n_re * a_re + lam_im * a_im) * inv_den
    f_im = (lam_im * a_re - n_re * a_im) * inv_den
    bb_re = f_re[..., None] * b_re - f_im[..., None] * b_im
    bb_im = f_re[..., None] * b_im + f_im[..., None] * b_re
    return lam_re, lam_im, bb_re, bb_im


def _s5_scan(u, h0_re, h0_im, lam_re, lam_im, bb_re, bb_im, c_re, c_im, with_output):
    Bsz, L, G, GC = u.shape
    n_chunks = L // SCAN_CHUNK
    u_chunks = u.reshape(Bsz, n_chunks, SCAN_CHUNK, G, GC).transpose(1, 2, 0, 3, 4)
    lam_t_re = jnp.broadcast_to(lam_re, (SCAN_CHUNK, 1) + lam_re.shape)
    lam_t_im = jnp.broadcast_to(lam_im, (SCAN_CHUNK, 1) + lam_im.shape)

    def chunk_step(carry, u_blk):
        h_re, h_im = carry
        x_re = jnp.einsum('tbgc,gpc->tbgp', u_blk, bb_re)
        x_im = jnp.einsum('tbgc,gpc->tbgp', u_blk, bb_im)
        p_re, p_im, s_re, s_im = lax.associative_scan(
            _complex_linear_combine, (lam_t_re, lam_t_im, x_re, x_im), axis=0)
        s_re, s_im = (s_re + p_re * h_re - p_im * h_im, s_im + p_re * h_im + p_im * h_re)
        y = None
        if with_output:
            y = jnp.einsum('tbgp,gcp->tbgc', s_re, c_re) - jnp.einsum('tbgp,gcp->tbgc', s_im, c_im)
        return (s_re[-1], s_im[-1]), y

    (h_re, h_im), y = lax.scan(chunk_step, (h0_re, h0_im), u_chunks)
    if with_output:
        y = y.transpose(2, 0, 1, 3, 4).reshape(Bsz, L, G * GC)
    return y, h_re, h_im


def _s5_mixer(h_lat, h_ctx, a_re, a_im, log_dt, b_re, b_im, c_re, c_im, d_skip, w_gate, w_val, ctx_out):
    f32 = jnp.float32
    Bsz, S, D = h_lat.shape
    n_ctx = h_ctx.shape[1]
    u_lat = h_lat.astype(f32).reshape(Bsz, S, S5_GROUPS, S5_GROUP_CH)
    u_ctx = h_ctx.astype(f32).reshape(Bsz, n_ctx, S5_GROUPS, S5_GROUP_CH)
    zero = jnp.zeros((Bsz, S5_GROUPS, S5_STATE), f32)
    y_lat = d_skip.astype(f32) * h_lat.astype(f32)
    y_ctx = d_skip.astype(f32) * h_ctx.astype(f32) if ctx_out else None
    for direction in range(2):
        flip = (lambda t: t[:, ::-1]) if direction == 1 else (lambda t: t)
        lam_re, lam_im, bb_re, bb_im = _s5_discretise(
            a_re[direction], a_im[direction], log_dt[direction], b_re[direction], b_im[direction])
        cr, ci = c_re[direction].astype(f32), c_im[direction].astype(f32)
        yc, hc_re, hc_im = _s5_scan(flip(u_ctx), zero, zero, lam_re, lam_im, bb_re, bb_im, cr, ci, ctx_out)
        yl, _, _ = _s5_scan(flip(u_lat), hc_re, hc_im, lam_re, lam_im, bb_re, bb_im, cr, ci, True)
        y_lat = y_lat + flip(yl)
        if ctx_out:
            y_ctx = y_ctx + flip(yc)

    def glu(y, dtype):
        g = jax.nn.gelu(y.astype(dtype))
        return (g @ w_val) * jax.nn.sigmoid(g @ w_gate)

    out_lat = glu(y_lat, h_lat.dtype)
    out_ctx = glu(y_ctx, h_ctx.dtype) if ctx_out else None
    return out_lat, out_ctx


def _windowed_gqa(h_lat, h_ctx, w_qkv, w_o, sink, cos, sin, ctx_out):
    Bsz, S, _ = h_lat.shape
    n_ctx = h_ctx.shape[1]
    q, k, v = _split_qkv(h_lat @ w_qkv, HEADS_B, KV_HEADS_B, HEAD_DIM_B)
    q, k = _apply_axial_rope(q, cos, sin), _apply_axial_rope(k, cos, sin)
    qc, kc, vc = _split_qkv(h_ctx @ w_qkv, HEADS_B, KV_HEADS_B, HEAD_DIM_B)
    sink_g = sink.reshape(KV_HEADS_B, HEADS_B // KV_HEADS_B)
    pad = ((0, 0), (HALO, HALO), (0, 0), (0, 0))
    kp, vp = jnp.pad(k, pad), jnp.pad(v, pad)
    band = ATT_BLOCK + 2 * HALO
    qi = jnp.arange(ATT_BLOCK)[:, None]
    kj = jnp.arange(band)[None, :]
    in_window = jnp.abs(kj - HALO - qi) <= WINDOW

    def one_block(args):
        b, q_blk = args
        start = b * ATT_BLOCK
        k_blk = lax.dynamic_slice_in_dim(kp, start, band, axis=1)
        v_blk = lax.dynamic_slice_in_dim(vp, start, band, axis=1)
        key_pos = start - HALO + kj
        mask = in_window & (key_pos >= 0) & (key_pos < S)
        return _attend(q_blk, ((k_blk, v_blk, mask), (kc, vc, None)), sink_g)

    o = lax.map(one_block, (jnp.arange(S // ATT_BLOCK), _to_query_blocks(q, KV_HEADS_B)))
    y_lat = _from_query_blocks(o) @ w_o
    y_ctx = None
    if ctx_out:
        o_c = _attend(_group_heads(qc, KV_HEADS_B), ((kc, vc, None),), sink_g)
        y_ctx = o_c.reshape(Bsz, n_ctx, HEADS_B * HEAD_DIM_B) @ w_o
    return y_lat, y_ctx


def _axial_gqa(h_lat, h_ctx, w_qkv, w_o, q_norm, k_norm, cos, sin, ctx_out):
    Bsz, S, _ = h_lat.shape
    n_ctx = h_ctx.shape[1]
    q, k, v = _split_qkv(h_lat @ w_qkv, HEADS_C, KV_HEADS_C, HEAD_DIM_C)
    q = _apply_axial_rope(_rms_norm(q, q_norm), cos, sin)
    k = _apply_axial_rope(_rms_norm(k, k_norm), cos, sin)
    qc, kc, vc = _split_qkv(h_ctx @ w_qkv, HEADS_C, KV_HEADS_C, HEAD_DIM_C)
    kc = _rms_norm(kc, k_norm)

    def one_block(q_blk):
        return _attend(q_blk, ((k, v, None), (kc, vc, None)), None)

    o = lax.map(one_block, _to_query_blocks(q, KV_HEADS_C))
    y_lat = _from_query_blocks(o) @ w_o
    y_ctx = None
    if ctx_out:
        o_c = _attend(_group_heads(_rms_norm(qc, q_norm), KV_HEADS_C), ((kc, vc, None),), None)
        y_ctx = o_c.reshape(Bsz, n_ctx, HEADS_C * HEAD_DIM_C) @ w_o
    return y_lat, y_ctx


def _moe(h, router_w, router_b, w1, w3, w2):
    n_tok = h.shape[0]
    aff = jax.nn.sigmoid((h @ router_w).astype(jnp.float32))
    sel = aff + router_b.astype(jnp.float32)
    group_score = lax.top_k(sel.reshape(n_tok, N_EXPERT_GROUPS, EXPERTS_PER_GROUP), TOP_K)[0].sum(-1)
    group = jnp.argmax(group_score, axis=-1)
    expert_group = jnp.arange(N_EXPERTS) // EXPERTS_PER_GROUP
    masked = jnp.where(expert_group[None, :] == group[:, None], sel, NEG_INF)
    _, idx = lax.top_k(masked, TOP_K)
    gate = jnp.take_along_axis(aff, idx, axis=-1)
    gate = gate / gate.sum(-1, keepdims=True)
    combine = jnp.einsum('nk,nke->ne', gate, jax.nn.one_hot(idx, N_EXPERTS, dtype=jnp.float32)).astype(h.dtype)
    y = jnp.zeros_like(h)
    for e in range(N_EXPERTS):
        out = (jax.nn.silu(h @ w1[e]) * (h @ w3[e])) @ w2[e]
        y = y + combine[:, e:e + 1] * out
    return y


def setup_inputs(seed: int = 0) -> dict:
    key = jax.random.key(seed)
    ks = iter(jax.random.split(key, 40))
    f32 = jnp.float32

    def nrm(shape, scale):
        return scale * jax.random.normal(next(ks), shape, f32)

    D = D_MODEL
    qkv_b = (HEADS_B + 2 * KV_HEADS_B) * HEAD_DIM_B
    qkv_c = (HEADS_C + 2 * KV_HEADS_C) * HEAD_DIM_C
    s5_shape = (N_A_LAYERS, 2, S5_GROUPS, S5_STATE)
    return {
        'x': nrm((BATCH, SEQ, D), 1.0),
        'c': nrm((BATCH, D), 1.0),
        'ctx': nrm((BATCH, CTX_LEN, D), 1.0),
        'c_ctx': nrm((D,), 1.0),
        'mod_w': nrm((DEPTH, D, 6 * D), MOD_SCALE * D ** -0.5),
        'mod_b': nrm((DEPTH, 6 * D), 0.02),
        'ln_g': 1.0 + nrm((DEPTH, 2, D), 0.02),
        'ln_b': nrm((DEPTH, 2, D), 0.02),
        'router_w': nrm((D, N_EXPERTS), D ** -0.5),
        'router_b': nrm((N_EXPERTS,), 0.01),
        'moe_w1': nrm((DEPTH, N_EXPERTS, D, D_EXPERT), D ** -0.5),
        'moe_w3': nrm((DEPTH, N_EXPERTS, D, D_EXPERT), D ** -0.5),
        'moe_w2': nrm((DEPTH, N_EXPERTS, D_EXPERT, D), DEEPNORM_BETA * D_EXPERT ** -0.5),
        's5_a_re': -0.5 + nrm(s5_shape, 0.01),
        's5_a_im': jnp.pi * jnp.arange(S5_STATE, dtype=f32) + nrm(s5_shape, 0.01),
        's5_log_dt': jax.random.uniform(next(ks), (N_A_LAYERS, 2, S5_GROUPS), f32,
                                        math.log(S5_DT_MIN), math.log(S5_DT_MAX)),
        's5_b_re': nrm((N_A_LAYERS, 2, S5_GROUPS, S5_STATE, S5_GROUP_CH), (2 * S5_GROUP_CH) ** -0.5),
        's5_b_im': nrm((N_A_LAYERS, 2, S5_GROUPS, S5_STATE, S5_GROUP_CH), (2 * S5_GROUP_CH) ** -0.5),
        's5_c_re': nrm((N_A_LAYERS, 2, S5_GROUPS, S5_GROUP_CH, S5_STATE), 0.5 ** 0.5),
        's5_c_im': nrm((N_A_LAYERS, 2, S5_GROUPS, S5_GROUP_CH, S5_STATE), 0.5 ** 0.5),
        's5_d': nrm((N_A_LAYERS, D), 1.0),
        's5_w_gate': nrm((N_A_LAYERS, D, D), D ** -0.5),
        's5_w_val': nrm((N_A_LAYERS, D, D), DEEPNORM_BETA * D ** -0.5),
        'swa_w_qkv': nrm((N_B_LAYERS, D, qkv_b), D ** -0.5),
        'swa_w_o': nrm((N_B_LAYERS, HEADS_B * HEAD_DIM_B, D), DEEPNORM_BETA * D ** -0.5),
        'swa_sink': nrm((N_B_LAYERS, HEADS_B), 1.0),
        'gqa_w_qkv': nrm((N_C_LAYERS, D, qkv_c), D ** -0.5),
        'gqa_w_o': nrm((N_C_LAYERS, HEADS_C * HEAD_DIM_C, D), DEEPNORM_BETA * D ** -0.5),
        'gqa_q_norm': 1.0 + nrm((N_C_LAYERS, HEAD_DIM_C), 0.02),
        'gqa_k_norm': 1.0 + nrm((N_C_LAYERS, HEAD_DIM_C), 0.02),
    }


def reference(x, c, ctx, c_ctx, mod_w, mod_b, ln_g, ln_b, router_w, router_b,
              moe_w1, moe_w3, moe_w2,
              s5_a_re, s5_a_im, s5_log_dt, s5_b_re, s5_b_im, s5_c_re, s5_c_im, s5_d, s5_w_gate, s5_w_val,
              swa_w_qkv, swa_w_o, swa_sink,
              gqa_w_qkv, gqa_w_o, gqa_q_norm, gqa_k_norm):
    Bsz, S, D = x.shape
    n_ctx = ctx.shape[1]
    n_rows = S // GRID_W
    cos_b, sin_b = _axial_rope_tables(n_rows, HEAD_DIM_B)
    cos_c, sin_c = _axial_rope_tables(n_rows, HEAD_DIM_C)
    for i in range(DEPTH):
        kind, j = i % N_MIXERS, i // N_MIXERS
        ctx_out = i < DEPTH - 1
        sh1, sc1, g1, sh2, sc2, g2 = [t[:, None, :] for t in _modulation(c, mod_w[i], mod_b[i])]
        csh1, csc1, cg1, csh2, csc2, cg2 = _modulation(c_ctx, mod_w[i], mod_b[i])
        h_lat = x * (1 + sc1) + sh1
        h_ctx = ctx * (1 + csc1) + csh1
        if kind == 0:
            y_lat, y_ctx = _s5_mixer(h_lat, h_ctx, s5_a_re[j], s5_a_im[j], s5_log_dt[j], s5_b_re[j], s5_b_im[j],
                                     s5_c_re[j], s5_c_im[j], s5_d[j], s5_w_gate[j], s5_w_val[j], ctx_out)
        elif kind == 1:
            y_lat, y_ctx = _windowed_gqa(h_lat, h_ctx, swa_w_qkv[j], swa_w_o[j], swa_sink[j], cos_b, sin_b, ctx_out)
        else:
            y_lat, y_ctx = _axial_gqa(h_lat, h_ctx, gqa_w_qkv[j], gqa_w_o[j], gqa_q_norm[j], gqa_k_norm[j],
                                      cos_c, sin_c, ctx_out)
        x = _layer_norm(DEEPNORM_ALPHA * x + g1 * y_lat, ln_g[i, 0], ln_b[i, 0])
        h_lat = x * (1 + sc2) + sh2
        if ctx_out:
            ctx = _layer_norm(DEEPNORM_ALPHA * ctx + cg1 * y_ctx, ln_g[i, 0], ln_b[i, 0])
            h_ctx = ctx * (1 + csc2) + csh2
            tokens = jnp.concatenate([h_ctx, h_lat], axis=1).reshape(-1, D)
            y = _moe(tokens, router_w, router_b, moe_w1[i], moe_w3[i], moe_w2[i]).reshape(Bsz, n_ctx + S, D)
            ctx = _layer_norm(DEEPNORM_ALPHA * ctx + cg2 * y[:, :n_ctx], ln_g[i, 1], ln_b[i, 1])
            y_lat = y[:, n_ctx:]
        else:
            y_lat = _moe(h_lat.reshape(-1, D), router_w, router_b, moe_w1[i], moe_w3[i], moe_w2[i]).reshape(Bsz, S, D)
        x = _layer_norm(DEEPNORM_ALPHA * x + g2 * y_lat, ln_g[i, 1], ln_b[i, 1])
    return x
```

```python
import functools
import math

import jax
import jax.numpy as jnp
from jax import lax
from jax.experimental import pallas as pl
from jax.experimental.pallas import tpu as pltpu

GRID_W = 64
N_MIXERS = 3
LN_EPS = 1e-6
RMS_EPS = 1e-6
NEG_INF = -1e30
ROPE_THETA = 10000.0
S5_GROUP_CH = 16
S5_STATE = 64
SCAN_CHUNK = 128
WINDOW = 128
HEAD_DIM_B = 64
KV_HEADS_B = 2
HEAD_DIM_C = 128
KV_HEADS_C = 2
N_EXPERTS = 16
N_EXPERT_GROUPS = 4
EXPERTS_PER_GROUP = N_EXPERTS // N_EXPERT_GROUPS
TOP_K = 2

LANES = 128
VMEM_LIMIT_BYTES = 56 * 1024 * 1024

F32 = jnp.float32
BF16 = jnp.bfloat16
HIGHEST = lax.Precision.HIGHEST


def _params(*semantics):
    return pltpu.CompilerParams(dimension_semantics=semantics,
                                vmem_limit_bytes=VMEM_LIMIT_BYTES)


def _row_tile(n_rows, target):
    t = min(target, n_rows)
    while n_rows % t:
        t //= 2
    return t


def _layer_norm_rows(v, g, b):
    vc = v - jnp.mean(v, axis=-1, keepdims=True)
    var = jnp.mean(vc * vc, axis=-1, keepdims=True)
    return vc * lax.rsqrt(var + LN_EPS) * g + b


def _sigmoid(v):
    return 1.0 / (1.0 + jnp.exp(-v))


def _gelu_tanh(v):
    return 0.5 * v * (1.0 + jnp.tanh(math.sqrt(2.0 / math.pi) * (v + 0.044715 * (v * v * v))))


def _modulation_kernel(cond_ref, w_ref, b_ref, out_ref):
    cond = cond_ref[...]
    act = cond * _sigmoid(cond)
    out_ref[0] = jnp.dot(act, w_ref[0], preferred_element_type=F32, precision=HIGHEST) + b_ref[0]


def _modulation(cond, mod_w, mod_b):
    depth, d, n_out = mod_w.shape
    rows = cond.shape[0]
    tn = 1536
    return pl.pallas_call(
        _modulation_kernel,
        grid=(depth, n_out // tn),
        in_specs=[pl.BlockSpec((rows, d), lambda i, j: (0, 0)),
                  pl.BlockSpec((1, d, tn), lambda i, j: (i, 0, j)),
                  pl.BlockSpec((1, 1, tn), lambda i, j: (i, 0, j))],
        out_specs=pl.BlockSpec((1, rows, tn), lambda i, j: (i, 0, j)),
        out_shape=jax.ShapeDtypeStruct((depth, rows, n_out), F32),
        compiler_params=_params("parallel", "parallel"),
        name="modulation",
    )(cond, mod_w, mod_b.reshape(depth, 1, n_out))


def _swap_rotary_pairs(v, half):
    width = v.shape[-1]
    lane = lax.broadcasted_iota(jnp.int32, v.shape, v.ndim - 1)
    first = (lane // half) % 2 == 0
    return jnp.where(first, pltpu.roll(v, width - half, v.ndim - 1), pltpu.roll(v, half, v.ndim - 1))


def _qkv_kernel(*refs, n_heads, n_kv, head_dim, qk_norm, use_rope):
    it = iter(refs)
    x_ref, sc_ref, sh_ref, w_ref = next(it), next(it), next(it), next(it)
    if qk_norm:
        qn_ref, kn_ref = next(it), next(it)
    if use_rope:
        cos_ref, sin_ref = next(it), next(it)
    q_ref, k_ref, v_ref = next(it), next(it), next(it)

    h = (x_ref[...] * (1.0 + sc_ref[0]) + sh_ref[0]).astype(BF16)
    proj = jnp.dot(h, w_ref[...], preferred_element_type=F32)
    q_width, kv_width = n_heads * head_dim, n_kv * head_dim
    q = proj[:, :q_width]
    k = proj[:, q_width:q_width + kv_width]
    v = proj[:, q_width + kv_width:]

    def rms_heads(t, gain, count):
        parts = []
        for hd in range(count):
            th = t[:, hd * head_dim:(hd + 1) * head_dim]
            ms = jnp.mean(th * th, axis=-1, keepdims=True)
            parts.append(th * lax.rsqrt(ms + RMS_EPS) * gain)
        return jnp.concatenate(parts, axis=-1)

    if qk_norm:
        q = rms_heads(q, qn_ref[...], n_heads)
        k = rms_heads(k, kn_ref[...], n_kv)

    if use_rope:
        cos, sin = cos_ref[...], sin_ref[...]

        def rope(t):
            reps = t.shape[-1] // LANES
            c = jnp.concatenate([cos] * reps, axis=-1) if reps > 1 else cos
            s = jnp.concatenate([sin] * reps, axis=-1) if reps > 1 else sin
            return t * c + _swap_rotary_pairs(t, head_dim // 4) * s

        q, k = rope(q), rope(k)

    q = q * (head_dim ** -0.5)
    group = q_width // n_kv
    for g in range(n_kv):
        q_ref[g] = q[:, g * group:(g + 1) * group].astype(BF16)
        k_ref[g] = k[:, g * head_dim:(g + 1) * head_dim].astype(BF16)
        v_ref[g] = v[:, g * head_dim:(g + 1) * head_dim].astype(BF16)


def _qkv_project(x, sc, sh, w, n_heads, n_kv, head_dim, norms=None, rope=None, tm_target=512):
    n, d = x.shape
    bsz = sc.shape[0]
    per_batch = n // bsz
    tm = _row_tile(per_batch, tm_target)
    tiles_per_batch = per_batch // tm
    n_out = w.shape[1]
    group = n_heads * head_dim // n_kv
    row = lambda i: (i, 0)
    batch_vec = lambda i: (i // tiles_per_batch, 0, 0)
    const2 = lambda i: (0, 0)
    in_specs = [pl.BlockSpec((tm, d), row),
                pl.BlockSpec((1, 1, d), batch_vec),
                pl.BlockSpec((1, 1, d), batch_vec),
                pl.BlockSpec((d, n_out), const2)]
    args = [x, sc, sh, w]
    if norms is not None:
        in_specs += [pl.BlockSpec((1, head_dim), const2)] * 2
        args += [norms[0].reshape(1, head_dim), norms[1].reshape(1, head_dim)]
    if rope is not None:
        in_specs += [pl.BlockSpec((tm, LANES), lambda i: (i % tiles_per_batch, 0))] * 2
        args += [rope[0], rope[1]]
    kernel = functools.partial(_qkv_kernel, n_heads=n_heads, n_kv=n_kv, head_dim=head_dim,
                               qk_norm=norms is not None, use_rope=rope is not None)
    return pl.pallas_call(
        kernel,
        grid=(n // tm,),
        in_specs=in_specs,
        out_specs=[pl.BlockSpec((n_kv, tm, group), lambda i: (0, i, 0)),
                   pl.BlockSpec((n_kv, tm, head_dim), lambda i: (0, i, 0)),
                   pl.BlockSpec((n_kv, tm, head_dim), lambda i: (0, i, 0))],
        out_shape=[jax.ShapeDtypeStruct((n_kv, n, group), BF16),
                   jax.ShapeDtypeStruct((n_kv, n, head_dim), BF16),
                   jax.ShapeDtypeStruct((n_kv, n, head_dim), BF16)],
        compiler_params=_params("parallel"),
        name="qkv_project",
    )(*args)


def _rope_tables(seq, head_dim):
    quarter = head_dim // 4
    inv_freq = ROPE_THETA ** (-jnp.arange(quarter, dtype=F32) / quarter)
    t = jnp.arange(seq)
    rows = (t // GRID_W).astype(F32)
    cols = (t % GRID_W).astype(F32)
    ang_r = rows[:, None] * inv_freq
    ang_c = cols[:, None] * inv_freq
    cos = jnp.concatenate([jnp.cos(ang_r)] * 2 + [jnp.cos(ang_c)] * 2, axis=-1)
    sin = jnp.concatenate([-jnp.sin(ang_r), jnp.sin(ang_r), -jnp.sin(ang_c), jnp.sin(ang_c)], axis=-1)
    reps = LANES // head_dim
    return jnp.tile(cos, (1, reps)), jnp.tile(sin, (1, reps))


def _stack_heads(q, n_rep, head_dim):
    if n_rep == 1:
        return q
    return jnp.concatenate([q[:, r * head_dim:(r + 1) * head_dim] for r in range(n_rep)], axis=0)


def _unstack_heads(o, n_rep, rows):
    if n_rep == 1:
        return o
    return jnp.concatenate([o[r * rows:(r + 1) * rows] for r in range(n_rep)], axis=-1)


def _sink_column(sink_ref, g, n_rep, rows):
    return jnp.concatenate([jnp.full((rows, 1), sink_ref[g * n_rep + r], F32) for r in range(n_rep)], axis=0)


def _flash_kernel(*refs, n_rep, head_dim, has_sink):
    if has_sink:
        sink_ref, q_ref, k_ref, v_ref, o_ref, m_sc, l_sc, acc_sc = refs
    else:
        q_ref, k_ref, v_ref, o_ref, m_sc, l_sc, acc_sc = refs
    kv_head, kj = pl.program_id(1), pl.program_id(3)
    tq = q_ref.shape[1]

    @pl.when(kj == 0)
    def _():
        m_sc[...] = jnp.full(m_sc.shape, NEG_INF, F32)
        l_sc[...] = jnp.zeros(l_sc.shape, F32)
        acc_sc[...] = jnp.zeros(acc_sc.shape, F32)

    q = _stack_heads(q_ref[0], n_rep, head_dim)
    s = lax.dot_general(q, k_ref[0], (((1,), (1,)), ((), ())), preferred_element_type=F32)
    m_prev = m_sc[...]
    m_new = jnp.maximum(m_prev, jnp.max(s, axis=-1, keepdims=True))
    alpha = jnp.exp(m_prev - m_new)
    p = jnp.exp(s - m_new)
    l_sc[...] = alpha * l_sc[...] + jnp.sum(p, axis=-1, keepdims=True)
    acc_sc[...] = alpha * acc_sc[...] + jnp.dot(p.astype(BF16), v_ref[0], preferred_element_type=F32)
    m_sc[...] = m_new

    @pl.when(kj == pl.num_programs(3) - 1)
    def _():
        m, l, acc = m_sc[...], l_sc[...], acc_sc[...]
        if has_sink:
            sink = _sink_column(sink_ref, kv_head, n_rep, tq)
            m_fin = jnp.maximum(m, sink)
            scale = jnp.exp(m - m_fin)
            l = l * scale + jnp.exp(sink - m_fin)
            acc = acc * scale
        o_ref[...] = _unstack_heads(acc * (1.0 / l), n_rep, tq).astype(o_ref.dtype)


def _flash_attention(q, k, v, bsz, head_dim, sink=None, tq_target=256, tk_target=1280):
    n_kv, nq, group = q.shape
    n_rep = group // head_dim
    lq, lk = nq // bsz, k.shape[1] // bsz
    tq = _row_tile(lq, tq_target)
    tk = tk_target if lk % tk_target == 0 else _row_tile(lk, 256)
    qt, kt = lq // tq, lk // tk
    kernel = functools.partial(_flash_kernel, n_rep=n_rep, head_dim=head_dim, has_sink=sink is not None)
    in_specs = [pl.BlockSpec((1, tq, group), lambda b, g, i, j: (g, b * qt + i, 0)),
                pl.BlockSpec((1, tk, head_dim), lambda b, g, i, j: (g, b * kt + j, 0)),
                pl.BlockSpec((1, tk, head_dim), lambda b, g, i, j: (g, b * kt + j, 0))]
    args = [q, k, v]
    if sink is not None:
        in_specs = [pl.BlockSpec(memory_space=pltpu.SMEM)] + in_specs
        args = [sink] + args
    return pl.pallas_call(
        kernel,
        grid=(bsz, n_kv, qt, kt),
        in_specs=in_specs,
        out_specs=pl.BlockSpec((tq, group), lambda b, g, i, j: (b * qt + i, g)),
        out_shape=jax.ShapeDtypeStruct((nq, n_kv * group), BF16),
        scratch_shapes=[pltpu.VMEM((n_rep * tq, 1), F32),
                        pltpu.VMEM((n_rep * tq, 1), F32),
                        pltpu.VMEM((n_rep * tq, head_dim), F32)],
        compiler_params=_params("parallel", "parallel", "parallel", "arbitrary"),
        name="flash_attention",
    )(*args)


def _window_kernel(sink_ref, q_ref, kp_ref, kc_ref, kn_ref, vp_ref, vc_ref, vn_ref, kx_ref, vx_ref, o_ref,
                   *, n_rep, head_dim, seq):
    i = pl.program_id(2)
    tq = q_ref.shape[1]
    band = tq + 2 * WINDOW
    q = _stack_heads(q_ref[0], n_rep, head_dim)
    keys = jnp.concatenate([kp_ref[0], kc_ref[0], kn_ref[0]], axis=0)
    vals = jnp.concatenate([vp_ref[0], vc_ref[0], vn_ref[0]], axis=0)
    nt = (((1,), (1,)), ((), ()))
    s_band = lax.dot_general(q, keys, nt, preferred_element_type=F32)
    s_ctx = lax.dot_general(q, kx_ref[0], nt, preferred_element_type=F32)

    qi = lax.broadcasted_iota(jnp.int32, (tq, band), 0)
    kj = lax.broadcasted_iota(jnp.int32, (tq, band), 1)
    key_pos = i * tq - WINDOW + kj
    ok = (jnp.abs(kj - WINDOW - qi) <= WINDOW) & (key_pos >= 0) & (key_pos < seq)
    ok = jnp.concatenate([ok] * n_rep, axis=0)
    s_band = jnp.where(ok, s_band, NEG_INF)

    sink = _sink_column(sink_ref, pl.program_id(1), n_rep, tq)
    m = jnp.maximum(jnp.maximum(jnp.max(s_band, axis=-1, keepdims=True),
                                jnp.max(s_ctx, axis=-1, keepdims=True)), sink)
    p_band = jnp.exp(s_band - m)
    p_ctx = jnp.exp(s_ctx - m)
    denom = (jnp.sum(p_band, axis=-1, keepdims=True) + jnp.sum(p_ctx, axis=-1, keepdims=True)
             + jnp.exp(sink - m))
    acc = (jnp.dot(p_band.astype(BF16), vals, preferred_element_type=F32)
           + jnp.dot(p_ctx.astype(BF16), vx_ref[0], preferred_element_type=F32))
    o_ref[...] = _unstack_heads(acc * (1.0 / denom), n_rep, tq).astype(o_ref.dtype)


def _window_attention(q, k, v, k_ctx, v_ctx, sink, bsz, head_dim, tq=128):
    n_kv, nq, group = q.shape
    n_rep = group // head_dim
    seq = nq // bsz
    n_ctx = k_ctx.shape[1] // bsz
    qt = seq // tq
    ratio = tq // WINDOW
    halo_blocks = seq // WINDOW
    q_map = lambda b, g, i: (g, b * qt + i, 0)
    prev_map = lambda b, g, i: (g, b * halo_blocks + jnp.maximum(i * ratio - 1, 0), 0)
    next_map = lambda b, g, i: (g, b * halo_blocks + jnp.minimum((i + 1) * ratio, halo_blocks - 1), 0)
    ctx_map = lambda b, g, i: (g, b, 0)
    halo = pl.BlockSpec((1, WINDOW, head_dim), prev_map)
    halo_next = pl.BlockSpec((1, WINDOW, head_dim), next_map)
    cur = pl.BlockSpec((1, tq, head_dim), q_map)
    ctx = pl.BlockSpec((1, n_ctx, head_dim), ctx_map)
    kernel = functools.partial(_window_kernel, n_rep=n_rep, head_dim=head_dim, seq=seq)
    return pl.pallas_call(
        kernel,
        grid=(bsz, n_kv, qt),
        in_specs=[pl.BlockSpec(memory_space=pltpu.SMEM),
                  pl.BlockSpec((1, tq, group), q_map),
                  halo, cur, halo_next, halo, cur, halo_next, ctx, ctx],
        out_specs=pl.BlockSpec((tq, group), lambda b, g, i: (b * qt + i, g)),
        out_shape=jax.ShapeDtypeStruct((nq, n_kv * group), BF16),
        compiler_params=_params("parallel", "parallel", "parallel"),
        name="window_attention",
    )(sink, q, k, k, k, v, v, v, k_ctx, v_ctx)


def _router_combine(logits_t, bias_ref):
    aff = [_sigmoid(logits_t[e:e + 1, :]) for e in range(N_EXPERTS)]
    sel = [aff[e] + bias_ref[e] for e in range(N_EXPERTS)]
    best_score, best_group = None, None
    for g in range(N_EXPERT_GROUPS):
        a, b, c, d = sel[g * EXPERTS_PER_GROUP:(g + 1) * EXPERTS_PER_GROUP]
        hi1, lo1 = jnp.maximum(a, b), jnp.minimum(a, b)
        hi2, lo2 = jnp.maximum(c, d), jnp.minimum(c, d)
        score = jnp.maximum(hi1, hi2) + jnp.maximum(jnp.minimum(hi1, hi2), jnp.maximum(lo1, lo2))
        if g == 0:
            best_score, best_group = score, jnp.zeros(score.shape, jnp.int32)
        else:
            better = score > best_score
            best_score = jnp.where(better, score, best_score)
            best_group = jnp.where(better, g, best_group)
    gates = []
    for e in range(N_EXPERTS):
        g = e // EXPERTS_PER_GROUP
        beaten = jnp.zeros(best_group.shape, jnp.int32)
        for o in range(g * EXPERTS_PER_GROUP, (g + 1) * EXPERTS_PER_GROUP):
            if o == e:
                continue
            wins = (sel[o] > sel[e]) | ((sel[o] == sel[e]) & (o < e))
            beaten = beaten + wins.astype(jnp.int32)
        chosen = (best_group == g) & (beaten < TOP_K)
        gates.append(jnp.where(chosen, aff[e], 0.0))
    total = functools.reduce(jnp.add, gates)
    inv = 1.0 / total
    return jnp.concatenate([gt * inv for gt in gates], axis=0)


def _post_mixer_kernel(*refs, alpha, glu, route):
    it = iter(refs)
    rb_ref = next(it) if route else None
    a_ref, x_ref = next(it), next(it)
    if glu:
        sc1_ref, sh1_ref, dsk_ref, wv_ref, wg_ref = next(it), next(it), next(it), next(it), next(it)
    else:
        wo_ref = next(it)
    g1_ref, lng_ref, lnb_ref = next(it), next(it), next(it)
    if route:
        sc2_ref, sh2_ref, rw_ref = next(it), next(it), next(it)
    x1_ref = next(it)
    if route:
        h2_ref, comb_ref = next(it), next(it)

    x = x_ref[...]
    if glu:
        h = x * (1.0 + sc1_ref[0]) + sh1_ref[0]
        act = _gelu_tanh(dsk_ref[...] * h + a_ref[...]).astype(BF16)
        y = (jnp.dot(act, wv_ref[...], preferred_element_type=F32)
             * _sigmoid(jnp.dot(act, wg_ref[...], preferred_element_type=F32)))
    else:
        y = jnp.dot(a_ref[...], wo_ref[...], preferred_element_type=F32)
    x1 = _layer_norm_rows(alpha * x + g1_ref[0] * y, lng_ref[...], lnb_ref[...])
    x1_ref[...] = x1
    if route:
        h2 = x1 * (1.0 + sc2_ref[0]) + sh2_ref[0]
        h2_ref[...] = h2.astype(BF16)
        logits_t = lax.dot_general(rw_ref[...], h2, (((1,), (1,)), ((), ())),
                                   preferred_element_type=F32, precision=HIGHEST)
        comb_ref[...] = _router_combine(logits_t, rb_ref)


def _post_mixer(a, x, g1, ln_g, ln_b, alpha, *, w_o=None, glu=None, route=None, tm_target=512):
    n, d = x.shape
    bsz = g1.shape[0]
    per_batch = n // bsz
    tm = _row_tile(per_batch, tm_target)
    tiles_per_batch = per_batch // tm
    row = lambda i: (i, 0)
    batch_vec = lambda i: (i // tiles_per_batch, 0, 0)
    const2 = lambda i: (0, 0)
    vec = pl.BlockSpec((1, 1, d), batch_vec)
    in_specs, args = [], []
    if route is not None:
        in_specs.append(pl.BlockSpec(memory_space=pltpu.SMEM))
        args.append(route[3])
    in_specs += [pl.BlockSpec((tm, d), row), pl.BlockSpec((tm, d), row)]
    args += [a, x]
    if glu is not None:
        sc1, sh1, d_skip, w_val, w_gate = glu
        in_specs += [vec, vec, pl.BlockSpec((1, d), const2),
                     pl.BlockSpec((d, d), const2), pl.BlockSpec((d, d), const2)]
        args += [sc1, sh1, d_skip.reshape(1, d), w_val, w_gate]
    else:
        in_specs.append(pl.BlockSpec((d, d), const2))
        args.append(w_o)
    in_specs += [vec, pl.BlockSpec((1, d), const2), pl.BlockSpec((1, d), const2)]
    args += [g1, ln_g.reshape(1, d), ln_b.reshape(1, d)]
    out_specs = [pl.BlockSpec((tm, d), row)]
    out_shape = [jax.ShapeDtypeStruct((n, d), F32)]
    if route is not None:
        in_specs += [vec, vec, pl.BlockSpec((N_EXPERTS, d), const2)]
        args += [route[0], route[1], route[2]]
        out_specs += [pl.BlockSpec((tm, d), row), pl.BlockSpec((N_EXPERTS, tm), lambda i: (0, i))]
        out_shape += [jax.ShapeDtypeStruct((n, d), BF16), jax.ShapeDtypeStruct((N_EXPERTS, n), F32)]
    kernel = functools.partial(_post_mixer_kernel, alpha=alpha, glu=glu is not None, route=route is not None)
    return pl.pallas_call(
        kernel,
        grid=(n // tm,),
        in_specs=in_specs,
        out_specs=out_specs,
        out_shape=out_shape,
        compiler_params=_params("parallel"),
        name="post_mixer",
    )(*args)


def _moe_kernel(h_ref, comb_ref, w13_ref, w2_ref, x_ref, g2_ref, lng_ref, lnb_ref, out_ref, acc_sc,
                *, alpha, d_expert):
    e = pl.program_id(1)

    @pl.when(e == 0)
    def _():
        acc_sc[...] = jnp.zeros(acc_sc.shape, F32)

    a = jnp.dot(h_ref[...], w13_ref[0], preferred_element_type=F32)
    up = a[:, :d_expert]
    act = (up * _sigmoid(up) * a[:, d_expert:]).astype(BF16)
    out = jnp.dot(act, w2_ref[0], preferred_element_type=F32)
    comb = comb_ref[...]
    lane = lax.broadcasted_iota(jnp.int32, comb.shape, 1)
    weight = jnp.sum(jnp.where(lane == e, comb, 0.0), axis=-1, keepdims=True)
    acc_sc[...] += weight * out

    @pl.when(e == pl.num_programs(1) - 1)
    def _():
        out_ref[...] = _layer_norm_rows(alpha * x_ref[...] + g2_ref[0] * acc_sc[...], lng_ref[...], lnb_ref[...])


def _moe(h2, comb, w13, w2, x1, g2, ln_g, ln_b, alpha, tm_target=1024):
    n, d = x1.shape
    bsz = g2.shape[0]
    per_batch = n // bsz
    tm = _row_tile(per_batch, tm_target)
    tiles_per_batch = per_batch // tm
    n_exp, _, two_de = w13.shape
    row = lambda i, e: (i, 0)
    const2 = lambda i, e: (0, 0)
    kernel = functools.partial(_moe_kernel, alpha=alpha, d_expert=two_de // 2)
    return pl.pallas_call(
        kernel,
        grid=(n // tm, n_exp),
        in_specs=[pl.BlockSpec((tm, d), row),
                  pl.BlockSpec((tm, n_exp), row),
                  pl.BlockSpec((1, d, two_de), lambda i, e: (e, 0, 0)),
                  pl.BlockSpec((1, two_de // 2, d), lambda i, e: (e, 0, 0)),
                  pl.BlockSpec((tm, d), row),
                  pl.BlockSpec((1, 1, d), lambda i, e: (i // tiles_per_batch, 0, 0)),
                  pl.BlockSpec((1, d), const2),
                  pl.BlockSpec((1, d), const2)],
        out_specs=pl.BlockSpec((tm, d), row),
        out_shape=jax.ShapeDtypeStruct((n, d), F32),
        scratch_shapes=[pltpu.VMEM((tm, d), F32)],
        compiler_params=_params("parallel", "arbitrary"),
        name="moe",
    )(h2, comb, w13, w2, x1, g2, ln_g.reshape(1, d), ln_b.reshape(1, d))


def _modulated_slab(u, scale_ref, shift_ref, channel, row, class_starts, d_model):
    scale = scale_ref[channel]
    shift = shift_ref[channel]
    for k, start in enumerate(class_starts[1:], start=1):
        later = row >= start
        scale = jnp.where(later, scale_ref[k * d_model + channel], scale)
        shift = jnp.where(later, shift_ref[k * d_model + channel], shift)
    return (u * scale + shift).astype(BF16)


def _s5_inject_kernel(scale_ref, shift_ref, xt_ref, pin_ref, out_ref, *, class_starts, d_model):
    g = pl.program_id(0)
    rows = xt_ref.shape[1]
    row = lax.broadcasted_iota(jnp.int32, (rows, SCAN_CHUNK), 0)
    acc = jnp.zeros(out_ref.shape, F32)
    for c in range(S5_GROUP_CH):
        u = _modulated_slab(xt_ref[c], scale_ref, shift_ref, g * S5_GROUP_CH + c, row, class_starts, d_model)
        acc = acc + jnp.dot(u, pin_ref[0, c], preferred_element_type=F32)
    out_ref[...] = acc


def _s5_carry_kernel(sin_ref, a_ref, b_ref, h_ref, *, bsz, ctx_chunks, lat_chunks):
    half = 2 * S5_STATE
    lat0 = bsz * ctx_chunks

    def advance(h, row, lo):
        h_ref[row, :, lo:lo + half] = h
        a = a_ref[:, lo:lo + half]
        b = b_ref[:, lo:lo + half]
        return a * h + b * pltpu.roll(h, S5_STATE, 1) + sin_ref[row, :, lo:lo + half]

    zero = jnp.zeros((sin_ref.shape[1], half), F32)
    states = []
    for bi in range(bsz):
        hf, hb = zero, zero
        for n in range(ctx_chunks):
            hf = advance(hf, bi * ctx_chunks + n, 0)
            hb = advance(hb, bi * ctx_chunks + ctx_chunks - 1 - n, half)
        states += [hf, hb]

    def body(n, carry):
        out = []
        for bi in range(bsz):
            base = lat0 + bi * lat_chunks
            out.append(advance(carry[2 * bi], base + n, 0))
            out.append(advance(carry[2 * bi + 1], base + lat_chunks - 1 - n, half))
        return tuple(out)

    lax.fori_loop(0, lat_chunks, body, tuple(states))


def _s5_output_kernel(scale_ref, shift_ref, xt_ref, kc_ref, h_ref, pout_ref, out_ref, m_sc,
                      *, class_starts, d_model):
    g = pl.program_id(0)
    rows = xt_ref.shape[1]
    t = SCAN_CHUNK

    def build(cp, carry):
        for c in range(S5_GROUP_CH):
            lags = kc_ref[0, cp, c:c + 1, :]
            shifted = pltpu.roll(jnp.broadcast_to(lags, (t, 2 * t)), 0, 1, stride=1, stride_axis=0)
            m_sc[cp, :, c * t:(c + 1) * t] = shifted[:, t:].astype(BF16)
        return carry

    lax.fori_loop(0, S5_GROUP_CH, build, 0)

    row = lax.broadcasted_iota(jnp.int32, (rows, t), 0)
    acc = jnp.dot(h_ref[...].astype(BF16), pout_ref[0], preferred_element_type=F32)
    for cp in range(S5_GROUP_CH):
        u = _modulated_slab(xt_ref[cp], scale_ref, shift_ref, g * S5_GROUP_CH + cp, row, class_starts, d_model)
        acc = acc + jnp.dot(u, m_sc[cp], preferred_element_type=F32)
    for c in range(S5_GROUP_CH):
        out_ref[c] = acc[:, c * t:(c + 1) * t]


def _s5_tables(a_re, a_im, log_dt, b_re, b_im, c_re, c_im):
    t = SCAN_CHUNK
    dt = jnp.exp(log_dt.astype(F32))[..., None]
    a_re, a_im = a_re.astype(F32), a_im.astype(F32)
    steps = jnp.arange(t + 1, dtype=F32)[None, :, None, None]
    mag = jnp.exp(steps * (a_re * dt)[:, None])
    ang = steps * (a_im * dt)[:, None]
    pw_re, pw_im = mag * jnp.cos(ang), mag * jnp.sin(ang)
    lam_re, lam_im = pw_re[:, 1], pw_im[:, 1]
    inv_den = 1.0 / (a_re * a_re + a_im * a_im)
    n_re = lam_re - 1.0
    f_re = (n_re * a_re + lam_im * a_im) * inv_den
    f_im = (lam_im * a_re - n_re * a_im) * inv_den
    bb_re = f_re[..., None] * b_re - f_im[..., None] * b_im
    bb_im = f_re[..., None] * b_im + f_im[..., None] * b_re
    lb_re = pw_re[..., None] * bb_re[:, None] - pw_im[..., None] * bb_im[:, None]
    lb_im = pw_re[..., None] * bb_im[:, None] + pw_im[..., None] * bb_re[:, None]
    resp = (jnp.einsum('dgcp,dtgpk->dgkct', c_re, lb_re[:, :t], precision=HIGHEST)
            - jnp.einsum('dgcp,dtgpk->dgkct', c_im, lb_im[:, :t], precision=HIGHEST))
    fwd, bwd = resp[0], resp[1]
    lags = jnp.concatenate([jnp.zeros_like(fwd[..., :1]), bwd[..., :0:-1],
                            fwd[..., :1] + bwd[..., :1], fwd[..., 1:]], axis=-1)
    def inject(re, im, order):
        re, im = re[order], im[order]
        return jnp.concatenate([re, im], axis=2).transpose(1, 3, 0, 2)
    s_fwd = jnp.arange(t - 1, -1, -1)
    s_bwd = jnp.arange(t)
    p_in = jnp.concatenate([inject(lb_re[0], lb_im[0], s_fwd), inject(lb_re[1], lb_im[1], s_bwd)], axis=-1)
    cl_re = c_re[:, None] * pw_re[:, :, :, None, :] - c_im[:, None] * pw_im[:, :, :, None, :]
    cl_im = c_re[:, None] * pw_im[:, :, :, None, :] + c_im[:, None] * pw_re[:, :, :, None, :]
    def readout(re, im, order):
        re, im = re[order], im[order]
        both = jnp.concatenate([re, -im], axis=-1)
        return both.transpose(1, 3, 2, 0).reshape(both.shape[1], 2 * S5_STATE, -1)
    t_fwd = jnp.arange(1, t + 1)
    t_bwd = jnp.arange(t, 0, -1)
    p_out = jnp.concatenate([readout(cl_re[0], cl_im[0], t_fwd), readout(cl_re[1], cl_im[1], t_bwd)], axis=1)
    carry_a = jnp.concatenate([pw_re[0, t], pw_re[0, t], pw_re[1, t], pw_re[1, t]], axis=-1)
    carry_b = jnp.concatenate([-pw_im[0, t], pw_im[0, t], -pw_im[1, t], pw_im[1, t]], axis=-1)
    return lags, p_in.astype(BF16), p_out.astype(BF16), carry_a, carry_b


def _to_chunk_major(rows2d, bsz):
    n, d = rows2d.shape
    return rows2d.reshape(n // SCAN_CHUNK, SCAN_CHUNK, d).transpose(2, 0, 1)


def _from_chunk_major(xt):
    d, chunks, t = xt.shape
    return xt.transpose(1, 2, 0).reshape(chunks * t, d)


def _s5_scan(x_lat, x_ctx, scale, shift, tables, bsz):
    lags, p_in, p_out, carry_a, carry_b = tables
    d = x_lat.shape[1]
    groups = d // S5_GROUP_CH
    ctx_chunks = x_ctx.shape[0] // bsz // SCAN_CHUNK
    lat_chunks = x_lat.shape[0] // bsz // SCAN_CHUNK
    lat0 = bsz * ctx_chunks
    rows = lat0 + bsz * lat_chunks
    xt = jnp.concatenate([_to_chunk_major(x_ctx, bsz), _to_chunk_major(x_lat, bsz)], axis=1)
    class_starts = (0,) + tuple(lat0 + b * lat_chunks for b in range(bsz))
    scale_flat, shift_flat = scale.reshape(-1), shift.reshape(-1)
    state_w = 4 * S5_STATE
    smem = pl.BlockSpec(memory_space=pltpu.SMEM)
    slab = pl.BlockSpec((S5_GROUP_CH, rows, SCAN_CHUNK), lambda g: (g, 0, 0))

    inject = pl.pallas_call(
        functools.partial(_s5_inject_kernel, class_starts=class_starts, d_model=d),
        grid=(groups,),
        in_specs=[smem, smem, slab,
                  pl.BlockSpec((1, S5_GROUP_CH, SCAN_CHUNK, state_w), lambda g: (g, 0, 0, 0))],
        out_specs=pl.BlockSpec((rows, state_w), lambda g: (0, g)),
        out_shape=jax.ShapeDtypeStruct((rows, groups * state_w), F32),
        compiler_params=_params("parallel"),
        name="s5_inject",
    )(scale_flat, shift_flat, xt, p_in)

    gb = 8
    entering = pl.pallas_call(
        functools.partial(_s5_carry_kernel, bsz=bsz, ctx_chunks=ctx_chunks, lat_chunks=lat_chunks),
        grid=(groups // gb,),
        in_specs=[pl.BlockSpec((rows, gb, state_w), lambda g: (0, g, 0)),
                  pl.BlockSpec((gb, state_w), lambda g: (g, 0)),
                  pl.BlockSpec((gb, state_w), lambda g: (g, 0))],
        out_specs=pl.BlockSpec((rows, gb, state_w), lambda g: (0, g, 0)),
        out_shape=jax.ShapeDtypeStruct((rows, groups, state_w), F32),
        compiler_params=_params("parallel"),
        name="s5_carry",
    )(inject.reshape(rows, groups, state_w), carry_a, carry_b)

    yt = pl.pallas_call(
        functools.partial(_s5_output_kernel, class_starts=class_starts, d_model=d),
        grid=(groups,),
        in_specs=[smem, smem, slab,
                  pl.BlockSpec((1, S5_GROUP_CH, S5_GROUP_CH, 2 * SCAN_CHUNK), lambda g: (g, 0, 0, 0)),
                  pl.BlockSpec((rows, state_w), lambda g: (0, g)),
                  pl.BlockSpec((1, state_w, S5_GROUP_CH * SCAN_CHUNK), lambda g: (g, 0, 0))],
        out_specs=slab,
        out_shape=jax.ShapeDtypeStruct((d, rows, SCAN_CHUNK), F32),
        scratch_shapes=[pltpu.VMEM((S5_GROUP_CH, SCAN_CHUNK, S5_GROUP_CH * SCAN_CHUNK), BF16)],
        compiler_params=_params("parallel"),
        name="s5_output",
    )(scale_flat, shift_flat, xt, lags, entering.reshape(rows, groups * state_w), p_out)

    return _from_chunk_major(yt[:, lat0:]), _from_chunk_major(yt[:, :lat0])


def kernel(x, c, ctx, c_ctx, mod_w, mod_b, ln_g, ln_b, router_w, router_b, moe_w1, moe_w3, moe_w2, s5_a_re, s5_a_im, s5_log_dt, s5_b_re, s5_b_im, s5_c_re, s5_c_im, s5_d, s5_w_gate, s5_w_val, swa_w_qkv, swa_w_o, swa_sink, gqa_w_qkv, gqa_w_o, gqa_q_norm, gqa_k_norm):
    bsz, seq, d = x.shape
    n_ctx = ctx.shape[1]
    depth = mod_w.shape[0]
    alpha = (2.0 * depth) ** 0.25
    heads_b, heads_c = d // HEAD_DIM_B, d // HEAD_DIM_C

    cond = jnp.concatenate([c_ctx[None, :], c], axis=0)
    mods = _modulation(cond, mod_w, mod_b).reshape(depth, bsz + 1, 6, d)
    xl = x.reshape(bsz * seq, d)
    xc = ctx.reshape(bsz * n_ctx, d)
    router_w_t = router_w.T
    rope_b = _rope_tables(seq, HEAD_DIM_B)
    rope_c = _rope_tables(seq, HEAD_DIM_C)

    for i in range(depth):
        kind, j = i % N_MIXERS, i // N_MIXERS
        ctx_out = i < depth - 1
        lat_mod = [mods[i, 1:, k][:, None, :] for k in range(6)]
        ctx_mod = [jnp.broadcast_to(mods[i, 0, k][None, None, :], (bsz, 1, d)) for k in range(6)]
        ln1 = (ln_g[i, 0], ln_b[i, 0])
        ln2 = (ln_g[i, 1], ln_b[i, 1])
        buffers = [(xl, lat_mod, True)] + ([(xc, ctx_mod, False)] if ctx_out else [])

        if kind == 0:
            tables = _s5_tables(s5_a_re[j], s5_a_im[j], s5_log_dt[j], s5_b_re[j], s5_b_im[j],
                                s5_c_re[j], s5_c_im[j])
            y_lat, y_ctx = _s5_scan(xl, xc, 1.0 + mods[i, :, 1], mods[i, :, 0], tables, bsz)
            glu_w = (s5_d[j], s5_w_val[j].astype(BF16), s5_w_gate[j].astype(BF16))
            mixed = {True: y_lat, False: y_ctx}
        else:
            if kind == 1:
                w_qkv, w_o = swa_w_qkv[j].astype(BF16), swa_w_o[j].astype(BF16)
                heads, kv, dh, norms, rope = heads_b, KV_HEADS_B, HEAD_DIM_B, None, rope_b
            else:
                w_qkv, w_o = gqa_w_qkv[j].astype(BF16), gqa_w_o[j].astype(BF16)
                heads, kv, dh, norms, rope = heads_c, KV_HEADS_C, HEAD_DIM_C, (gqa_q_norm[j], gqa_k_norm[j]), rope_c
            ql, kl, vl = _qkv_project(xl, lat_mod[1], lat_mod[0], w_qkv, heads, kv, dh, norms=norms, rope=rope)
            qc, kc, vc = _qkv_project(xc, ctx_mod[1], ctx_mod[0], w_qkv, heads, kv, dh, norms=norms, rope=None)
            if kind == 1:
                sink = swa_sink[j].astype(F32)
                o_lat = _window_attention(ql, kl, vl, kc, vc, sink, bsz, dh)
                o_ctx = _flash_attention(qc, kc, vc, bsz, dh, sink=sink) if ctx_out else None
            else:
                def with_ctx(lat, cx):
                    both = jnp.concatenate([lat.reshape(kv, bsz, seq, dh), cx.reshape(kv, bsz, n_ctx, dh)], axis=2)
                    return both.reshape(kv, bsz * (seq + n_ctx), dh)
                o_lat = _flash_attention(ql, with_ctx(kl, kc), with_ctx(vl, vc), bsz, dh)
                o_ctx = _flash_attention(qc, kc, vc, bsz, dh) if ctx_out else None
            mixed = {True: o_lat, False: o_ctx}

        new = []
        for rows, mod, is_lat in buffers:
            sh1, sc1, g1, sh2, sc2, g2 = mod
            route = (sc2, sh2, router_w_t, router_b.astype(F32))
            if kind == 0:
                x1, h2, comb_t = _post_mixer(mixed[is_lat], rows, g1, *ln1, alpha,
                                             glu=(sc1, sh1) + glu_w, route=route)
            else:
                x1, h2, comb_t = _post_mixer(mixed[is_lat], rows, g1, *ln1, alpha, w_o=w_o, route=route)
            w13 = jnp.concatenate([moe_w1[i], moe_w3[i]], axis=-1).astype(BF16)
            new.append(_moe(h2, comb_t.T, w13, moe_w2[i].astype(BF16), x1, g2, *ln2, alpha))
        xl = new[0]
        if ctx_out:
            xc = new[1]
    return xl.reshape(bsz, seq, d)
```

```python
import functools
import math

import jax
import jax.numpy as jnp
from jax import lax
from jax.experimental import pallas as pl
from jax.experimental.pallas import tpu as pltpu

GRID_W = 64
N_MIXERS = 3
LN_EPS = 1e-6
RMS_EPS = 1e-6
NEG_INF = -1e30
ROPE_THETA = 10000.0
S5_GROUP_CH = 16
S5_STATE = 64
SCAN_CHUNK = 128
WINDOW = 128
HEAD_DIM_B = 64
KV_HEADS_B = 2
HEAD_DIM_C = 128
KV_HEADS_C = 2
N_EXPERTS = 16
N_EXPERT_GROUPS = 4
EXPERTS_PER_GROUP = N_EXPERTS // N_EXPERT_GROUPS
TOP_K = 2

LANES = 128
LOG2E = math.log2(math.e)
VMEM_LIMIT_BYTES = 56 * 1024 * 1024

F32 = jnp.float32
BF16 = jnp.bfloat16
HIGHEST = lax.Precision.HIGHEST


def _params(*semantics):
    return pltpu.CompilerParams(dimension_semantics=semantics,
                                vmem_limit_bytes=VMEM_LIMIT_BYTES)


def _row_tile(n_rows, target):
    t = min(target, n_rows)
    while n_rows % t:
        t //= 2
    return t


def _layer_norm_rows(v, g, b):
    vc = v - jnp.mean(v, axis=-1, keepdims=True)
    var = jnp.mean(vc * vc, axis=-1, keepdims=True)
    return vc * lax.rsqrt(var + LN_EPS) * g + b


def _sigmoid(v):
    return 1.0 / (1.0 + jnp.exp(-v))


def _gelu_tanh(v):
    return 0.5 * v * (1.0 + jnp.tanh(math.sqrt(2.0 / math.pi) * (v + 0.044715 * (v * v * v))))


def _modulation_kernel(cond_ref, w_ref, b_ref, out_ref):
    cond = cond_ref[...]
    act = cond * _sigmoid(cond)
    out_ref[0] = jnp.dot(act, w_ref[0], preferred_element_type=F32, precision=HIGHEST) + b_ref[0]


def _modulation(cond, mod_w, mod_b):
    depth, d, n_out = mod_w.shape
    rows = cond.shape[0]
    tn = 1536
    return pl.pallas_call(
        _modulation_kernel,
        grid=(depth, n_out // tn),
        in_specs=[pl.BlockSpec((rows, d), lambda i, j: (0, 0)),
                  pl.BlockSpec((1, d, tn), lambda i, j: (i, 0, j)),
                  pl.BlockSpec((1, 1, tn), lambda i, j: (i, 0, j))],
        out_specs=pl.BlockSpec((1, rows, tn), lambda i, j: (i, 0, j)),
        out_shape=jax.ShapeDtypeStruct((depth, rows, n_out), F32),
        compiler_params=_params("parallel", "parallel"),
        name="modulation",
    )(cond, mod_w, mod_b.reshape(depth, 1, n_out))


def _swap_rotary_pairs(v, half):
    width = v.shape[-1]
    lane = lax.broadcasted_iota(jnp.int32, v.shape, v.ndim - 1)
    first = (lane // half) % 2 == 0
    return jnp.where(first, pltpu.roll(v, width - half, v.ndim - 1), pltpu.roll(v, half, v.ndim - 1))


def _qkv_kernel(*refs, n_heads, n_kv, head_dim, qk_norm, use_rope):
    it = iter(refs)
    x_ref, sc_ref, sh_ref, w_ref = next(it), next(it), next(it), next(it)
    if qk_norm:
        qn_ref, kn_ref = next(it), next(it)
    if use_rope:
        cos_ref, sin_ref = next(it), next(it)
    q_ref, kt_ref, v_ref = next(it), next(it), next(it)

    h = (x_ref[...] * (1.0 + sc_ref[0]) + sh_ref[0]).astype(BF16)
    proj = jnp.dot(h, w_ref[...], preferred_element_type=F32)
    q_width, kv_width = n_heads * head_dim, n_kv * head_dim
    q = proj[:, :q_width]
    k = proj[:, q_width:q_width + kv_width]
    v = proj[:, q_width + kv_width:]

    def rms_heads(t, gain, count):
        parts = []
        for hd in range(count):
            th = t[:, hd * head_dim:(hd + 1) * head_dim]
            ms = jnp.mean(th * th, axis=-1, keepdims=True)
            parts.append(th * lax.rsqrt(ms + RMS_EPS) * gain)
        return jnp.concatenate(parts, axis=-1)

    if qk_norm:
        q = rms_heads(q, qn_ref[...], n_heads)
        k = rms_heads(k, kn_ref[...], n_kv)

    if use_rope:
        cos, sin = cos_ref[...], sin_ref[...]

        def rope(t):
            reps = t.shape[-1] // LANES
            c = jnp.concatenate([cos] * reps, axis=-1) if reps > 1 else cos
            s = jnp.concatenate([sin] * reps, axis=-1) if reps > 1 else sin
            return t * c + _swap_rotary_pairs(t, head_dim // 4) * s

        q, k = rope(q), rope(k)

    q = q * (head_dim ** -0.5 * LOG2E)
    k_t = k.T
    group = q_width // n_kv
    for g in range(n_kv):
        q_ref[g] = q[:, g * group:(g + 1) * group].astype(BF16)
        kt_ref[g] = k_t[g * head_dim:(g + 1) * head_dim, :].astype(BF16)
        v_ref[g] = v[:, g * head_dim:(g + 1) * head_dim].astype(BF16)


def _qkv_project(x, sc, sh, w, n_heads, n_kv, head_dim, norms=None, rope=None, tm_target=512):
    n, d = x.shape
    bsz = sc.shape[0]
    per_batch = n // bsz
    tm = _row_tile(per_batch, tm_target)
    tiles_per_batch = per_batch // tm
    n_out = w.shape[1]
    group = n_heads * head_dim // n_kv
    row = lambda i: (i, 0)
    batch_vec = lambda i: (i // tiles_per_batch, 0, 0)
    const2 = lambda i: (0, 0)
    in_specs = [pl.BlockSpec((tm, d), row),
                pl.BlockSpec((1, 1, d), batch_vec),
                pl.BlockSpec((1, 1, d), batch_vec),
                pl.BlockSpec((d, n_out), const2)]
    args = [x, sc, sh, w]
    if norms is not None:
        in_specs += [pl.BlockSpec((1, head_dim), const2)] * 2
        args += [norms[0].reshape(1, head_dim), norms[1].reshape(1, head_dim)]
    if rope is not None:
        in_specs += [pl.BlockSpec((tm, LANES), lambda i: (i % tiles_per_batch, 0))] * 2
        args += [rope[0], rope[1]]
    kernel = functools.partial(_qkv_kernel, n_heads=n_heads, n_kv=n_kv, head_dim=head_dim,
                               qk_norm=norms is not None, use_rope=rope is not None)
    return pl.pallas_call(
        kernel,
        grid=(n // tm,),
        in_specs=in_specs,
        out_specs=[pl.BlockSpec((n_kv, tm, group), lambda i: (0, i, 0)),
                   pl.BlockSpec((n_kv, head_dim, tm), lambda i: (0, 0, i)),
                   pl.BlockSpec((n_kv, tm, head_dim), lambda i: (0, i, 0))],
        out_shape=[jax.ShapeDtypeStruct((n_kv, n, group), BF16),
                   jax.ShapeDtypeStruct((n_kv, head_dim, n), BF16),
                   jax.ShapeDtypeStruct((n_kv, n, head_dim), BF16)],
        compiler_params=_params("parallel"),
        name="qkv_project",
    )(*args)


def _rope_tables(seq, head_dim):
    quarter = head_dim // 4
    inv_freq = ROPE_THETA ** (-jnp.arange(quarter, dtype=F32) / quarter)
    t = jnp.arange(seq)
    rows = (t // GRID_W).astype(F32)
    cols = (t % GRID_W).astype(F32)
    ang_r = rows[:, None] * inv_freq
    ang_c = cols[:, None] * inv_freq
    cos = jnp.concatenate([jnp.cos(ang_r)] * 2 + [jnp.cos(ang_c)] * 2, axis=-1)
    sin = jnp.concatenate([-jnp.sin(ang_r), jnp.sin(ang_r), -jnp.sin(ang_c), jnp.sin(ang_c)], axis=-1)
    reps = LANES // head_dim
    return jnp.tile(cos, (1, reps)), jnp.tile(sin, (1, reps))


def _stack_heads(q, n_rep, head_dim):
    if n_rep == 1:
        return q
    return jnp.concatenate([q[:, r * head_dim:(r + 1) * head_dim] for r in range(n_rep)], axis=0)


def _unstack_heads(o, n_rep, rows):
    if n_rep == 1:
        return o
    return jnp.concatenate([o[r * rows:(r + 1) * rows] for r in range(n_rep)], axis=-1)


def _sink_column(sink_ref, g, n_rep, rows):
    return jnp.concatenate([jnp.full((rows, 1), sink_ref[g * n_rep + r] * LOG2E, F32) for r in range(n_rep)],
                           axis=0)


def _ones_width(head_dim):
    return LANES - head_dim % LANES


def _with_ones(v, head_dim):
    return jnp.concatenate([v, jnp.ones((v.shape[0], _ones_width(head_dim)), v.dtype)], axis=1)


def _lane_repeat(col_block, width):
    reps = width // col_block.shape[1]
    return col_block if reps == 1 else jnp.concatenate([col_block] * reps, axis=1)


def _flash_kernel(*refs, n_rep, head_dim, has_sink, n_split):
    if has_sink:
        sink_ref, q_ref, kt_ref, v_ref, o_ref, m_sc, acc_sc = refs
    else:
        q_ref, kt_ref, v_ref, o_ref, m_sc, acc_sc = refs
    kv_head, kj = pl.program_id(1), pl.program_id(3)
    tq = q_ref.shape[1]
    width = acc_sc.shape[1]

    first = kj == 0
    m_all = jnp.where(first, NEG_INF, m_sc[...])
    acc_all = jnp.where(first, 0.0, acc_sc[...])
    q = _stack_heads(q_ref[0], n_rep, head_dim)
    kt = kt_ref[0]
    v1 = _with_ones(v_ref[0], head_dim)
    part = n_rep * tq // n_split
    m_out, acc_out = [], []
    for h in range(n_split):
        rs = slice(h * part, (h + 1) * part)
        s = jnp.dot(q[rs], kt, preferred_element_type=F32)
        m_prev = m_all[rs]
        m_new = jnp.maximum(m_prev, jnp.max(s, axis=-1, keepdims=True))
        p = jnp.exp2(s - _lane_repeat(m_new, s.shape[1]))
        alpha = jnp.exp2(m_prev - m_new)
        acc_out.append(_lane_repeat(alpha, width) * acc_all[rs]
                       + jnp.dot(p.astype(BF16), v1, preferred_element_type=F32))
        m_out.append(m_new)
    m_sc[...] = jnp.concatenate(m_out, axis=0)
    acc_sc[...] = jnp.concatenate(acc_out, axis=0)

    @pl.when(kj == pl.num_programs(3) - 1)
    def _():
        acc = acc_sc[...]
        num, l = acc[:, :head_dim], acc[:, head_dim:head_dim + 1]
        if has_sink:
            m = m_sc[:, :1]
            sink = _sink_column(sink_ref, kv_head, n_rep, tq)
            m_fin = jnp.maximum(m, sink)
            scale = jnp.exp2(m - m_fin)
            l = l * scale + jnp.exp2(sink - m_fin)
            num = num * scale
        o_ref[...] = _unstack_heads(num * (1.0 / l), n_rep, tq).astype(o_ref.dtype)


def _flash_attention(q, kt, v, bsz, head_dim, sink=None, tq_target=512, tk_target=3328, n_split=16):
    n_kv, nq, group = q.shape
    n_rep = group // head_dim
    lq, lk = nq // bsz, v.shape[1] // bsz
    tq = _row_tile(lq, tq_target)
    tk = tk_target if lk % tk_target == 0 else _row_tile(lk, 256)
    qt, kt_tiles = lq // tq, lk // tk
    kernel = functools.partial(_flash_kernel, n_rep=n_rep, head_dim=head_dim, has_sink=sink is not None,
                               n_split=n_split)
    in_specs = [pl.BlockSpec((1, tq, group), lambda b, g, i, j: (g, b * qt + i, 0)),
                pl.BlockSpec((1, head_dim, tk), lambda b, g, i, j: (g, 0, b * kt_tiles + j)),
                pl.BlockSpec((1, tk, head_dim), lambda b, g, i, j: (g, b * kt_tiles + j, 0))]
    args = [q, kt, v]
    if sink is not None:
        in_specs = [pl.BlockSpec(memory_space=pltpu.SMEM)] + in_specs
        args = [sink] + args
    return pl.pallas_call(
        kernel,
        grid=(bsz, n_kv, qt, kt_tiles),
        in_specs=in_specs,
        out_specs=pl.BlockSpec((tq, group), lambda b, g, i, j: (b * qt + i, g)),
        out_shape=jax.ShapeDtypeStruct((nq, n_kv * group), BF16),
        scratch_shapes=[pltpu.VMEM((n_rep * tq, LANES), F32),
                        pltpu.VMEM((n_rep * tq, head_dim + _ones_width(head_dim)), F32)],
        compiler_params=_params("parallel", "parallel", "parallel", "arbitrary"),
        name="flash_attention",
    )(*args)


def _window_kernel(sink_ref, q_ref, kp_ref, kc_ref, kn_ref, kx_ref, vp_ref, vc_ref, vn_ref, vx_ref, o_ref,
                   *, n_rep, head_dim, seq):
    kv_head, i = pl.program_id(1), pl.program_id(2)
    tq = q_ref.shape[1]
    band = tq + 2 * WINDOW
    kt = jnp.concatenate([kp_ref[0], kc_ref[0], kn_ref[0], kx_ref[0]], axis=1)
    v1 = _with_ones(jnp.concatenate([vp_ref[0], vc_ref[0], vn_ref[0], vx_ref[0]], axis=0), head_dim)
    n_keys = kt.shape[1]

    qi = lax.broadcasted_iota(jnp.int32, (tq, n_keys), 0)
    kj = lax.broadcasted_iota(jnp.int32, (tq, n_keys), 1)
    key_pos = i * tq - WINDOW + kj
    ok = (kj >= band) | ((jnp.abs(kj - WINDOW - qi) <= WINDOW) & (key_pos >= 0) & (key_pos < seq))

    q = q_ref[0]
    outs = []
    for r in range(n_rep):
        s = jnp.dot(q[:, r * head_dim:(r + 1) * head_dim], kt, preferred_element_type=F32)
        s = jnp.where(ok, s, NEG_INF)
        sink = sink_ref[kv_head * n_rep + r] * LOG2E
        m = jnp.maximum(jnp.max(s, axis=-1, keepdims=True), sink)
        p = jnp.exp2(s - m)
        pv = jnp.dot(p.astype(BF16), v1, preferred_element_type=F32)
        denom = pv[:, head_dim:head_dim + 1] + jnp.exp2(sink - m)
        outs.append(pv[:, :head_dim] * (1.0 / denom))
    o_ref[...] = jnp.concatenate(outs, axis=1).astype(o_ref.dtype)


def _window_attention(q, kt, v, kt_ctx, v_ctx, sink, bsz, head_dim, tq_target=512):
    n_kv, nq, group = q.shape
    n_rep = group // head_dim
    seq = nq // bsz
    n_ctx = v_ctx.shape[1] // bsz
    tq = _row_tile(seq, tq_target)
    qt = seq // tq
    ratio = tq // WINDOW
    halo_blocks = seq // WINDOW
    prev_blk = lambda b, i: b * halo_blocks + jnp.maximum(i * ratio - 1, 0)
    next_blk = lambda b, i: b * halo_blocks + jnp.minimum((i + 1) * ratio, halo_blocks - 1)
    q_map = lambda b, g, i: (g, b * qt + i, 0)
    k_specs = [pl.BlockSpec((1, head_dim, WINDOW), lambda b, g, i: (g, 0, prev_blk(b, i))),
               pl.BlockSpec((1, head_dim, tq), lambda b, g, i: (g, 0, b * qt + i)),
               pl.BlockSpec((1, head_dim, WINDOW), lambda b, g, i: (g, 0, next_blk(b, i))),
               pl.BlockSpec((1, head_dim, n_ctx), lambda b, g, i: (g, 0, b))]
    v_specs = [pl.BlockSpec((1, WINDOW, head_dim), lambda b, g, i: (g, prev_blk(b, i), 0)),
               pl.BlockSpec((1, tq, head_dim), q_map),
               pl.BlockSpec((1, WINDOW, head_dim), lambda b, g, i: (g, next_blk(b, i), 0)),
               pl.BlockSpec((1, n_ctx, head_dim), lambda b, g, i: (g, b, 0))]
    kernel = functools.partial(_window_kernel, n_rep=n_rep, head_dim=head_dim, seq=seq)
    return pl.pallas_call(
        kernel,
        grid=(bsz, n_kv, qt),
        in_specs=[pl.BlockSpec(memory_space=pltpu.SMEM), pl.BlockSpec((1, tq, group), q_map)] + k_specs + v_specs,
        out_specs=pl.BlockSpec((tq, group), lambda b, g, i: (b * qt + i, g)),
        out_shape=jax.ShapeDtypeStruct((nq, n_kv * group), BF16),
        compiler_params=_params("parallel", "parallel", "parallel"),
        name="window_attention",
    )(sink, q, kt, kt, kt, kt_ctx, v, v, v, v_ctx)


def _router_combine(logits_t, bias_ref):
    aff = [_sigmoid(logits_t[e:e + 1, :]) for e in range(N_EXPERTS)]
    sel = [aff[e] + bias_ref[e] for e in range(N_EXPERTS)]
    best_score, best_group = None, None
    for g in range(N_EXPERT_GROUPS):
        a, b, c, d = sel[g * EXPERTS_PER_GROUP:(g + 1) * EXPERTS_PER_GROUP]
        hi1, lo1 = jnp.maximum(a, b), jnp.minimum(a, b)
        hi2, lo2 = jnp.maximum(c, d), jnp.minimum(c, d)
        score = jnp.maximum(hi1, hi2) + jnp.maximum(jnp.minimum(hi1, hi2), jnp.maximum(lo1, lo2))
        if g == 0:
            best_score, best_group = score, jnp.zeros(score.shape, jnp.int32)
        else:
            better = score > best_score
            best_score = jnp.where(better, score, best_score)
            best_group = jnp.where(better, g, best_group)
    gates = []
    for e in range(N_EXPERTS):
        g = e // EXPERTS_PER_GROUP
        beaten = jnp.zeros(best_group.shape, jnp.int32)
        for o in range(g * EXPERTS_PER_GROUP, (g + 1) * EXPERTS_PER_GROUP):
            if o == e:
                continue
            wins = (sel[o] > sel[e]) | ((sel[o] == sel[e]) & (o < e))
            beaten = beaten + wins.astype(jnp.int32)
        chosen = (best_group == g) & (beaten < TOP_K)
        gates.append(jnp.where(chosen, aff[e], 0.0))
    total = functools.reduce(jnp.add, gates)
    inv = 1.0 / total
    return jnp.concatenate([gt * inv for gt in gates], axis=0)


def _post_mixer_kernel(*refs, alpha, glu, route):
    it = iter(refs)
    rb_ref = next(it) if route else None
    a_ref, x_ref = next(it), next(it)
    if glu:
        sc1_ref, sh1_ref, dsk_ref, wv_ref, wg_ref = next(it), next(it), next(it), next(it), next(it)
    else:
        wo_ref = next(it)
    g1_ref, lng_ref, lnb_ref = next(it), next(it), next(it)
    if route:
        sc2_ref, sh2_ref, rw_ref = next(it), next(it), next(it)
    x1_ref = next(it)
    if route:
        h2_ref, comb_ref = next(it), next(it)

    x = x_ref[...]
    if glu:
        h = x * (1.0 + sc1_ref[0]) + sh1_ref[0]
        act = _gelu_tanh(dsk_ref[...] * h + a_ref[...]).astype(BF16)
        y = (jnp.dot(act, wv_ref[...], preferred_element_type=F32)
             * _sigmoid(jnp.dot(act, wg_ref[...], preferred_element_type=F32)))
    else:
        y = jnp.dot(a_ref[...], wo_ref[...], preferred_element_type=F32)
    x1 = _layer_norm_rows(alpha * x + g1_ref[0] * y, lng_ref[...], lnb_ref[...])
    x1_ref[...] = x1
    if route:
        h2 = x1 * (1.0 + sc2_ref[0]) + sh2_ref[0]
        h2_ref[...] = h2.astype(BF16)
        logits_t = lax.dot_general(rw_ref[...], h2, (((1,), (1,)), ((), ())),
                                   preferred_element_type=F32, precision=HIGHEST)
        comb_ref[...] = _router_combine(logits_t, rb_ref)


def _post_mixer(a, x, g1, ln_g, ln_b, alpha, *, w_o=None, glu=None, route=None, tm_target=512):
    n, d = x.shape
    bsz = g1.shape[0]
    per_batch = n // bsz
    tm = _row_tile(per_batch, tm_target)
    tiles_per_batch = per_batch // tm
    row = lambda i: (i, 0)
    batch_vec = lambda i: (i // tiles_per_batch, 0, 0)
    const2 = lambda i: (0, 0)
    vec = pl.BlockSpec((1, 1, d), batch_vec)
    in_specs, args = [], []
    if route is not None:
        in_specs.append(pl.BlockSpec(memory_space=pltpu.SMEM))
        args.append(route[3])
    in_specs += [pl.BlockSpec((tm, d), row), pl.BlockSpec((tm, d), row)]
    args += [a, x]
    if glu is not None:
        sc1, sh1, d_skip, w_val, w_gate = glu
        in_specs += [vec, vec, pl.BlockSpec((1, d), const2),
                     pl.BlockSpec((d, d), const2), pl.BlockSpec((d, d), const2)]
        args += [sc1, sh1, d_skip.reshape(1, d), w_val, w_gate]
    else:
        in_specs.append(pl.BlockSpec((d, d), const2))
        args.append(w_o)
    in_specs += [vec, pl.BlockSpec((1, d), const2), pl.BlockSpec((1, d), const2)]
    args += [g1, ln_g.reshape(1, d), ln_b.reshape(1, d)]
    out_specs = [pl.BlockSpec((tm, d), row)]
    out_shape = [jax.ShapeDtypeStruct((n, d), F32)]
    if route is not None:
        in_specs += [vec, vec, pl.BlockSpec((N_EXPERTS, d), const2)]
        args += [route[0], route[1], route[2]]
        out_specs += [pl.BlockSpec((tm, d), row), pl.BlockSpec((N_EXPERTS, tm), lambda i: (0, i))]
        out_shape += [jax.ShapeDtypeStruct((n, d), BF16), jax.ShapeDtypeStruct((N_EXPERTS, n), F32)]
    kernel = functools.partial(_post_mixer_kernel, alpha=alpha, glu=glu is not None, route=route is not None)
    return pl.pallas_call(
        kernel,
        grid=(n // tm,),
        in_specs=in_specs,
        out_specs=out_specs,
        out_shape=out_shape,
        compiler_params=_params("parallel"),
        name="post_mixer",
    )(*args)


def _moe_kernel(h_ref, comb_ref, w13_ref, w2_ref, x_ref, g2_ref, lng_ref, lnb_ref, out_ref, acc_sc,
                *, alpha, d_expert):
    e = pl.program_id(1)

    @pl.when(e == 0)
    def _():
        acc_sc[...] = jnp.zeros(acc_sc.shape, F32)

    a = jnp.dot(h_ref[...], w13_ref[0], preferred_element_type=F32)
    up = a[:, :d_expert]
    act = (up * _sigmoid(up) * a[:, d_expert:]).astype(BF16)
    out = jnp.dot(act, w2_ref[0], preferred_element_type=F32)
    comb = comb_ref[...]
    lane = lax.broadcasted_iota(jnp.int32, comb.shape, 1)
    weight = jnp.sum(jnp.where(lane == e, comb, 0.0), axis=-1, keepdims=True)
    acc_sc[...] += weight * out

    @pl.when(e == pl.num_programs(1) - 1)
    def _():
        out_ref[...] = _layer_norm_rows(alpha * x_ref[...] + g2_ref[0] * acc_sc[...], lng_ref[...], lnb_ref[...])


def _moe(h2, comb, w13, w2, x1, g2, ln_g, ln_b, alpha, tm_target=1024):
    n, d = x1.shape
    bsz = g2.shape[0]
    per_batch = n // bsz
    tm = _row_tile(per_batch, tm_target)
    tiles_per_batch = per_batch // tm
    n_exp, _, two_de = w13.shape
    row = lambda i, e: (i, 0)
    const2 = lambda i, e: (0, 0)
    kernel = functools.partial(_moe_kernel, alpha=alpha, d_expert=two_de // 2)
    return pl.pallas_call(
        kernel,
        grid=(n // tm, n_exp),
        in_specs=[pl.BlockSpec((tm, d), row),
                  pl.BlockSpec((tm, n_exp), row),
                  pl.BlockSpec((1, d, two_de), lambda i, e: (e, 0, 0)),
                  pl.BlockSpec((1, two_de // 2, d), lambda i, e: (e, 0, 0)),
                  pl.BlockSpec((tm, d), row),
                  pl.BlockSpec((1, 1, d), lambda i, e: (i // tiles_per_batch, 0, 0)),
                  pl.BlockSpec((1, d), const2),
                  pl.BlockSpec((1, d), const2)],
        out_specs=pl.BlockSpec((tm, d), row),
        out_shape=jax.ShapeDtypeStruct((n, d), F32),
        scratch_shapes=[pltpu.VMEM((tm, d), F32)],
        compiler_params=_params("parallel", "arbitrary"),
        name="moe",
    )(h2, comb, w13, w2, x1, g2, ln_g.reshape(1, d), ln_b.reshape(1, d))


def _modulated_slab(u, scale_ref, shift_ref, channel, row, class_starts, d_model):
    scale = scale_ref[channel]
    shift = shift_ref[channel]
    for k, start in enumerate(class_starts[1:], start=1):
        later = row >= start
        scale = jnp.where(later, scale_ref[k * d_model + channel], scale)
        shift = jnp.where(later, shift_ref[k * d_model + channel], shift)
    return (u * scale + shift).astype(BF16)


def _s5_inject_kernel(scale_ref, shift_ref, xt_ref, pin_ref, out_ref, *, class_starts, d_model):
    g = pl.program_id(0)
    rows = xt_ref.shape[1]
    row = lax.broadcasted_iota(jnp.int32, (rows, SCAN_CHUNK), 0)
    acc = jnp.zeros(out_ref.shape, F32)
    for c in range(S5_GROUP_CH):
        u = _modulated_slab(xt_ref[c], scale_ref, shift_ref, g * S5_GROUP_CH + c, row, class_starts, d_model)
        acc = acc + jnp.dot(u, pin_ref[0, c], preferred_element_type=F32)
    out_ref[...] = acc


def _s5_carry_kernel(sin_ref, a_ref, b_ref, h_ref, *, bsz, ctx_chunks, lat_chunks):
    half = 2 * S5_STATE
    lat0 = bsz * ctx_chunks

    def advance(h, row, lo):
        h_ref[row, :, lo:lo + half] = h
        a = a_ref[:, lo:lo + half]
        b = b_ref[:, lo:lo + half]
        return a * h + b * pltpu.roll(h, S5_STATE, 1) + sin_ref[row, :, lo:lo + half]

    zero = jnp.zeros((sin_ref.shape[1], half), F32)
    states = []
    for bi in range(bsz):
        hf, hb = zero, zero
        for n in range(ctx_chunks):
            hf = advance(hf, bi * ctx_chunks + n, 0)
            hb = advance(hb, bi * ctx_chunks + ctx_chunks - 1 - n, half)
        states += [hf, hb]

    def body(n, carry):
        out = []
        for bi in range(bsz):
            base = lat0 + bi * lat_chunks
            out.append(advance(carry[2 * bi], base + n, 0))
            out.append(advance(carry[2 * bi + 1], base + lat_chunks - 1 - n, half))
        return tuple(out)

    lax.fori_loop(0, lat_chunks, body, tuple(states))


def _s5_output_kernel(scale_ref, shift_ref, xt_ref, kc_ref, h_ref, pout_ref, out_ref, m_sc,
                      *, class_starts, d_model):
    g = pl.program_id(0)
    rows = xt_ref.shape[1]
    t = SCAN_CHUNK

    def build(cp, carry):
        for c in range(S5_GROUP_CH):
            lags = kc_ref[0, cp, c:c + 1, :]
            shifted = pltpu.roll(jnp.broadcast_to(lags, (t, 2 * t)), 0, 1, stride=1, stride_axis=0)
            m_sc[cp, :, c * t:(c + 1) * t] = shifted[:, t:].astype(BF16)
        return carry

    lax.fori_loop(0, S5_GROUP_CH, build, 0)

    row = lax.broadcasted_iota(jnp.int32, (rows, t), 0)
    acc = jnp.dot(h_ref[...].astype(BF16), pout_ref[0], preferred_element_type=F32)
    for cp in range(S5_GROUP_CH):
        u = _modulated_slab(xt_ref[cp], scale_ref, shift_ref, g * S5_GROUP_CH + cp, row, class_starts, d_model)
        acc = acc + jnp.dot(u, m_sc[cp], preferred_element_type=F32)
    for c in range(S5_GROUP_CH):
        out_ref[c] = acc[:, c * t:(c + 1) * t]


def _s5_tables(a_re, a_im, log_dt, b_re, b_im, c_re, c_im):
    t = SCAN_CHUNK
    a_re, a_im = a_re.astype(F32), a_im.astype(F32)
    dt = jnp.exp(log_dt.astype(F32))[..., None]
    steps = jnp.arange(t + 1, dtype=F32)[None, None, :, None]
    mag = jnp.exp(steps * (a_re * dt)[:, :, None])
    ang = steps * (a_im * dt)[:, :, None]
    pw_re, pw_im = mag * jnp.cos(ang), mag * jnp.sin(ang)
    lam_re, lam_im = pw_re[:, :, 1], pw_im[:, :, 1]
    inv_den = 1.0 / (a_re * a_re + a_im * a_im)
    n_re = lam_re - 1.0
    f_re = (n_re * a_re + lam_im * a_im) * inv_den
    f_im = (lam_im * a_re - n_re * a_im) * inv_den
    bb_re = f_re[..., None] * b_re - f_im[..., None] * b_im
    bb_im = f_re[..., None] * b_im + f_im[..., None] * b_re

    pt_re = pw_re[:, :, :t].transpose(0, 1, 3, 2)[..., None]
    pt_im = pw_im[:, :, :t].transpose(0, 1, 3, 2)[..., None]
    lb_re = pt_re * bb_re[:, :, :, None] - pt_im * bb_im[:, :, :, None]
    lb_im = pt_re * bb_im[:, :, :, None] + pt_im * bb_re[:, :, :, None]
    resp = (jnp.einsum('dgcp,dgptk->dgkct', c_re, lb_re, precision=HIGHEST)
            - jnp.einsum('dgcp,dgptk->dgkct', c_im, lb_im, precision=HIGHEST))
    fwd, bwd = resp[0], resp[1]
    lags = jnp.concatenate([jnp.zeros_like(fwd[..., :1]), bwd[..., :0:-1],
                            fwd[..., :1] + bwd[..., :1], fwd[..., 1:]], axis=-1)

    bt_re, bt_im = bb_re.transpose(0, 1, 3, 2)[:, :, :, None], bb_im.transpose(0, 1, 3, 2)[:, :, :, None]

    def inject(d, p_re, p_im):
        p_re, p_im = p_re[:, None], p_im[:, None]
        return [p_re * bt_re[d] - p_im * bt_im[d], p_re * bt_im[d] + p_im * bt_re[d]]

    p_in = jnp.concatenate(inject(0, pw_re[0, :, t - 1::-1], pw_im[0, :, t - 1::-1])
                           + inject(1, pw_re[1, :, :t], pw_im[1, :, :t]), axis=-1).astype(BF16)

    ct_re, ct_im = c_re.transpose(0, 1, 3, 2)[..., None], c_im.transpose(0, 1, 3, 2)[..., None]

    def readout(d, p_re, p_im):
        p_re, p_im = p_re.transpose(0, 2, 1)[:, :, None], p_im.transpose(0, 2, 1)[:, :, None]
        return [ct_re[d] * p_re - ct_im[d] * p_im, -(ct_re[d] * p_im + ct_im[d] * p_re)]

    p_out = jnp.concatenate(readout(0, pw_re[0, :, 1:], pw_im[0, :, 1:])
                            + readout(1, pw_re[1, :, :0:-1], pw_im[1, :, :0:-1]), axis=1).astype(BF16)
    p_out = p_out.reshape(p_out.shape[0], 4 * S5_STATE, -1)

    carry_a = jnp.concatenate([pw_re[0, :, t], pw_re[0, :, t], pw_re[1, :, t], pw_re[1, :, t]], axis=-1)
    carry_b = jnp.concatenate([-pw_im[0, :, t], pw_im[0, :, t], -pw_im[1, :, t], pw_im[1, :, t]], axis=-1)
    return lags, p_in, p_out, carry_a, carry_b


def _to_chunk_major(rows2d, bsz):
    n, d = rows2d.shape
    return rows2d.reshape(n // SCAN_CHUNK, SCAN_CHUNK, d).transpose(2, 0, 1)


def _from_chunk_major(xt):
    d, chunks, t = xt.shape
    return xt.transpose(1, 2, 0).reshape(chunks * t, d)


def _s5_scan(x_lat, x_ctx, scale, shift, tables, bsz):
    lags, p_in, p_out, carry_a, carry_b = tables
    d = x_lat.shape[1]
    groups = d // S5_GROUP_CH
    ctx_chunks = x_ctx.shape[0] // bsz // SCAN_CHUNK
    lat_chunks = x_lat.shape[0] // bsz // SCAN_CHUNK
    lat0 = bsz * ctx_chunks
    rows = lat0 + bsz * lat_chunks
    xt = jnp.concatenate([_to_chunk_major(x_ctx, bsz), _to_chunk_major(x_lat, bsz)], axis=1)
    class_starts = (0,) + tuple(lat0 + b * lat_chunks for b in range(bsz))
    scale_flat, shift_flat = scale.reshape(-1), shift.reshape(-1)
    state_w = 4 * S5_STATE
    smem = pl.BlockSpec(memory_space=pltpu.SMEM)
    slab = pl.BlockSpec((S5_GROUP_CH, rows, SCAN_CHUNK), lambda g: (g, 0, 0))

    inject = pl.pallas_call(
        functools.partial(_s5_inject_kernel, class_starts=class_starts, d_model=d),
        grid=(groups,),
        in_specs=[smem, smem, slab,
                  pl.BlockSpec((1, S5_GROUP_CH, SCAN_CHUNK, state_w), lambda g: (g, 0, 0, 0))],
        out_specs=pl.BlockSpec((rows, state_w), lambda g: (0, g)),
        out_shape=jax.ShapeDtypeStruct((rows, groups * state_w), F32),
        compiler_params=_params("parallel"),
        name="s5_inject",
    )(scale_flat, shift_flat, xt, p_in)

    gb = 8
    entering = pl.pallas_call(
        functools.partial(_s5_carry_kernel, bsz=bsz, ctx_chunks=ctx_chunks, lat_chunks=lat_chunks),
        grid=(groups // gb,),
        in_specs=[pl.BlockSpec((rows, gb, state_w), lambda g: (0, g, 0)),
                  pl.BlockSpec((gb, state_w), lambda g: (g, 0)),
                  pl.BlockSpec((gb, state_w), lambda g: (g, 0))],
        out_specs=pl.BlockSpec((rows, gb, state_w), lambda g: (0, g, 0)),
        out_shape=jax.ShapeDtypeStruct((rows, groups, state_w), F32),
        compiler_params=_params("parallel"),
        name="s5_carry",
    )(inject.reshape(rows, groups, state_w), carry_a, carry_b)

    yt = pl.pallas_call(
        functools.partial(_s5_output_kernel, class_starts=class_starts, d_model=d),
        grid=(groups,),
        in_specs=[smem, smem, slab,
                  pl.BlockSpec((1, S5_GROUP_CH, S5_GROUP_CH, 2 * SCAN_CHUNK), lambda g: (g, 0, 0, 0)),
                  pl.BlockSpec((rows, state_w), lambda g: (0, g)),
                  pl.BlockSpec((1, state_w, S5_GROUP_CH * SCAN_CHUNK), lambda g: (g, 0, 0))],
        out_specs=slab,
        out_shape=jax.ShapeDtypeStruct((d, rows, SCAN_CHUNK), F32),
        scratch_shapes=[pltpu.VMEM((S5_GROUP_CH, SCAN_CHUNK, S5_GROUP_CH * SCAN_CHUNK), BF16)],
        compiler_params=_params("parallel"),
        name="s5_output",
    )(scale_flat, shift_flat, xt, lags, entering.reshape(rows, groups * state_w), p_out)

    return _from_chunk_major(yt[:, lat0:]), _from_chunk_major(yt[:, :lat0])


def kernel(x, c, ctx, c_ctx, mod_w, mod_b, ln_g, ln_b, router_w, router_b, moe_w1, moe_w3, moe_w2, s5_a_re, s5_a_im, s5_log_dt, s5_b_re, s5_b_im, s5_c_re, s5_c_im, s5_d, s5_w_gate, s5_w_val, swa_w_qkv, swa_w_o, swa_sink, gqa_w_qkv, gqa_w_o, gqa_q_norm, gqa_k_norm):
    bsz, seq, d = x.shape
    n_ctx = ctx.shape[1]
    depth = mod_w.shape[0]
    alpha = (2.0 * depth) ** 0.25
    heads_b, heads_c = d // HEAD_DIM_B, d // HEAD_DIM_C

    cond = jnp.concatenate([c_ctx[None, :], c], axis=0)
    mods = _modulation(cond, mod_w, mod_b).reshape(depth, bsz + 1, 6, d)
    xl = x.reshape(bsz * seq, d)
    xc = ctx.reshape(bsz * n_ctx, d)
    router_w_t = router_w.T
    rope_b = _rope_tables(seq, HEAD_DIM_B)
    rope_c = _rope_tables(seq, HEAD_DIM_C)

    for i in range(depth):
        kind, j = i % N_MIXERS, i // N_MIXERS
        ctx_out = i < depth - 1
        lat_mod = [mods[i, 1:, k][:, None, :] for k in range(6)]
        ctx_mod = [jnp.broadcast_to(mods[i, 0, k][None, None, :], (bsz, 1, d)) for k in range(6)]
        ln1 = (ln_g[i, 0], ln_b[i, 0])
        ln2 = (ln_g[i, 1], ln_b[i, 1])
        buffers = [(xl, lat_mod, True)] + ([(xc, ctx_mod, False)] if ctx_out else [])

        if kind == 0:
            tables = _s5_tables(s5_a_re[j], s5_a_im[j], s5_log_dt[j], s5_b_re[j], s5_b_im[j],
                                s5_c_re[j], s5_c_im[j])
            y_lat, y_ctx = _s5_scan(xl, xc, 1.0 + mods[i, :, 1], mods[i, :, 0], tables, bsz)
            glu_w = (s5_d[j], s5_w_val[j].astype(BF16), s5_w_gate[j].astype(BF16))
            mixed = {True: y_lat, False: y_ctx}
        else:
            if kind == 1:
                w_qkv, w_o = swa_w_qkv[j].astype(BF16), swa_w_o[j].astype(BF16)
                heads, kv, dh, norms, rope = heads_b, KV_HEADS_B, HEAD_DIM_B, None, rope_b
            else:
                w_qkv, w_o = gqa_w_qkv[j].astype(BF16), gqa_w_o[j].astype(BF16)
                heads, kv, dh, norms, rope = heads_c, KV_HEADS_C, HEAD_DIM_C, (gqa_q_norm[j], gqa_k_norm[j]), rope_c
            ql, kl, vl = _qkv_project(xl, lat_mod[1], lat_mod[0], w_qkv, heads, kv, dh, norms=norms, rope=rope)
            qc, kc, vc = _qkv_project(xc, ctx_mod[1], ctx_mod[0], w_qkv, heads, kv, dh, norms=norms, rope=None)
            if kind == 1:
                sink = swa_sink[j].astype(F32)
                o_lat = _window_attention(ql, kl, vl, kc, vc, sink, bsz, dh)
                o_ctx = _flash_attention(qc, kc, vc, bsz, dh, sink=sink) if ctx_out else None
            else:
                keys = jnp.concatenate([kl.reshape(kv, dh, bsz, seq), kc.reshape(kv, dh, bsz, n_ctx)], axis=3)
                vals = jnp.concatenate([vl.reshape(kv, bsz, seq, dh), vc.reshape(kv, bsz, n_ctx, dh)], axis=2)
                o_lat = _flash_attention(ql, keys.reshape(kv, dh, -1), vals.reshape(kv, -1, dh), bsz, dh)
                o_ctx = _flash_attention(qc, kc, vc, bsz, dh) if ctx_out else None
            mixed = {True: o_lat, False: o_ctx}

        new = []
        for rows, mod, is_lat in buffers:
            sh1, sc1, g1, sh2, sc2, g2 = mod
            route = (sc2, sh2, router_w_t, router_b.astype(F32))
            if kind == 0:
                x1, h2, comb_t = _post_mixer(mixed[is_lat], rows, g1, *ln1, alpha,
                                             glu=(sc1, sh1) + glu_w, route=route)
            else:
                x1, h2, comb_t = _post_mixer(mixed[is_lat], rows, g1, *ln1, alpha, w_o=w_o, route=route)
            w13 = jnp.concatenate([moe_w1[i], moe_w3[i]], axis=-1).astype(BF16)
            new.append(_moe(h2, comb_t.T, w13, moe_w2[i].astype(BF16), x1, g2, *ln2, alpha))
        xl = new[0]
        if ctx_out:
            xc = new[1]
    return xl.reshape(bsz, seq, d)
```

```python
import functools
import math

import jax
import jax.numpy as jnp
from jax import lax
from jax.experimental import pallas as pl
from jax.experimental.pallas import tpu as pltpu

GRID_W = 64
N_MIXERS = 3
LN_EPS = 1e-6
RMS_EPS = 1e-6
NEG_INF = -1e30
ROPE_THETA = 10000.0
S5_GROUP_CH = 16
S5_STATE = 64
SCAN_CHUNK = 128
WINDOW = 128
HEAD_DIM_B = 64
KV_HEADS_B = 2
HEAD_DIM_C = 128
KV_HEADS_C = 2
N_EXPERTS = 16
N_EXPERT_GROUPS = 4
EXPERTS_PER_GROUP = N_EXPERTS // N_EXPERT_GROUPS
TOP_K = 2

LANES = 128
LOG2E = math.log2(math.e)
VMEM_LIMIT_BYTES = 56 * 1024 * 1024

F32 = jnp.float32
BF16 = jnp.bfloat16
HIGHEST = lax.Precision.HIGHEST


def _params(*semantics):
    return pltpu.CompilerParams(dimension_semantics=semantics,
                                vmem_limit_bytes=VMEM_LIMIT_BYTES)


def _row_tile(n_rows, target):
    t = min(target, n_rows)
    while n_rows % t:
        t //= 2
    return t


def _layer_norm_rows(v, g, b):
    vc = v - jnp.mean(v, axis=-1, keepdims=True)
    var = jnp.mean(vc * vc, axis=-1, keepdims=True)
    return vc * lax.rsqrt(var + LN_EPS) * g + b


def _sigmoid(v):
    return 1.0 / (1.0 + jnp.exp(-v))


def _gelu_tanh(v):
    return 0.5 * v * (1.0 + jnp.tanh(math.sqrt(2.0 / math.pi) * (v + 0.044715 * (v * v * v))))


def _modulation_kernel(cond_ref, w_ref, b_ref, out_ref):
    cond = cond_ref[...]
    act = cond * _sigmoid(cond)
    out_ref[0] = jnp.dot(act, w_ref[0], preferred_element_type=F32, precision=HIGHEST) + b_ref[0]


def _modulation(cond, mod_w, mod_b):
    depth, d, n_out = mod_w.shape
    rows = cond.shape[0]
    tn = 1536
    return pl.pallas_call(
        _modulation_kernel,
        grid=(depth, n_out // tn),
        in_specs=[pl.BlockSpec((rows, d), lambda i, j: (0, 0)),
                  pl.BlockSpec((1, d, tn), lambda i, j: (i, 0, j)),
                  pl.BlockSpec((1, 1, tn), lambda i, j: (i, 0, j))],
        out_specs=pl.BlockSpec((1, rows, tn), lambda i, j: (i, 0, j)),
        out_shape=jax.ShapeDtypeStruct((depth, rows, n_out), F32),
        compiler_params=_params("parallel", "parallel"),
        name="modulation",
    )(cond, mod_w, mod_b.reshape(depth, 1, n_out))


def _swap_rotary_pairs(v, half):
    width = v.shape[-1]
    lane = lax.broadcasted_iota(jnp.int32, v.shape, v.ndim - 1)
    first = (lane // half) % 2 == 0
    return jnp.where(first, pltpu.roll(v, width - half, v.ndim - 1), pltpu.roll(v, half, v.ndim - 1))


def _qkv_kernel(*refs, n_heads, n_kv, head_dim, qk_norm, use_rope):
    it = iter(refs)
    x_ref, sc_ref, sh_ref, w_ref = next(it), next(it), next(it), next(it)
    if qk_norm:
        qn_ref, kn_ref = next(it), next(it)
    if use_rope:
        cos_ref, sin_ref = next(it), next(it)
    q_ref, kt_ref, v_ref = next(it), next(it), next(it)

    h = (x_ref[...] * (1.0 + sc_ref[0]) + sh_ref[0]).astype(BF16)
    proj = jnp.dot(h, w_ref[...], preferred_element_type=F32)
    q_width, kv_width = n_heads * head_dim, n_kv * head_dim
    q = proj[:, :q_width]
    k = proj[:, q_width:q_width + kv_width]
    v = proj[:, q_width + kv_width:]

    def rms_heads(t, gain, count):
        parts = []
        for hd in range(count):
            th = t[:, hd * head_dim:(hd + 1) * head_dim]
            ms = jnp.mean(th * th, axis=-1, keepdims=True)
            parts.append(th * lax.rsqrt(ms + RMS_EPS) * gain)
        return jnp.concatenate(parts, axis=-1)

    if qk_norm:
        q = rms_heads(q, qn_ref[...], n_heads)
        k = rms_heads(k, kn_ref[...], n_kv)

    if use_rope:
        cos, sin = cos_ref[...], sin_ref[...]

        def rope(t):
            reps = t.shape[-1] // LANES
            c = jnp.concatenate([cos] * reps, axis=-1) if reps > 1 else cos
            s = jnp.concatenate([sin] * reps, axis=-1) if reps > 1 else sin
            return t * c + _swap_rotary_pairs(t, head_dim // 4) * s

        q, k = rope(q), rope(k)

    q = q * (head_dim ** -0.5 * LOG2E)
    k_t = k.T
    group = q_width // n_kv
    for g in range(n_kv):
        q_ref[g] = q[:, g * group:(g + 1) * group].astype(BF16)
        kt_ref[g] = k_t[g * head_dim:(g + 1) * head_dim, :].astype(BF16)
        v_ref[g] = v[:, g * head_dim:(g + 1) * head_dim].astype(BF16)


def _qkv_project(x, sc, sh, w, n_heads, n_kv, head_dim, norms=None, rope=None, tm_target=512):
    n, d = x.shape
    bsz = sc.shape[0]
    per_batch = n // bsz
    tm = _row_tile(per_batch, tm_target)
    tiles_per_batch = per_batch // tm
    n_out = w.shape[1]
    group = n_heads * head_dim // n_kv
    row = lambda i: (i, 0)
    batch_vec = lambda i: (i // tiles_per_batch, 0, 0)
    const2 = lambda i: (0, 0)
    in_specs = [pl.BlockSpec((tm, d), row),
                pl.BlockSpec((1, 1, d), batch_vec),
                pl.BlockSpec((1, 1, d), batch_vec),
                pl.BlockSpec((d, n_out), const2)]
    args = [x, sc, sh, w]
    if norms is not None:
        in_specs += [pl.BlockSpec((1, head_dim), const2)] * 2
        args += [norms[0].reshape(1, head_dim), norms[1].reshape(1, head_dim)]
    if rope is not None:
        in_specs += [pl.BlockSpec((tm, LANES), lambda i: (i % tiles_per_batch, 0))] * 2
        args += [rope[0], rope[1]]
    kernel = functools.partial(_qkv_kernel, n_heads=n_heads, n_kv=n_kv, head_dim=head_dim,
                               qk_norm=norms is not None, use_rope=rope is not None)
    return pl.pallas_call(
        kernel,
        grid=(n // tm,),
        in_specs=in_specs,
        out_specs=[pl.BlockSpec((n_kv, tm, group), lambda i: (0, i, 0)),
                   pl.BlockSpec((n_kv, head_dim, tm), lambda i: (0, 0, i)),
                   pl.BlockSpec((n_kv, tm, head_dim), lambda i: (0, i, 0))],
        out_shape=[jax.ShapeDtypeStruct((n_kv, n, group), BF16),
                   jax.ShapeDtypeStruct((n_kv, head_dim, n), BF16),
                   jax.ShapeDtypeStruct((n_kv, n, head_dim), BF16)],
        compiler_params=_params("parallel"),
        name="qkv_project",
    )(*args)


def _rope_tables(seq, head_dim):
    quarter = head_dim // 4
    inv_freq = ROPE_THETA ** (-jnp.arange(quarter, dtype=F32) / quarter)
    t = jnp.arange(seq)
    rows = (t // GRID_W).astype(F32)
    cols = (t % GRID_W).astype(F32)
    ang_r = rows[:, None] * inv_freq
    ang_c = cols[:, None] * inv_freq
    cos = jnp.concatenate([jnp.cos(ang_r)] * 2 + [jnp.cos(ang_c)] * 2, axis=-1)
    sin = jnp.concatenate([-jnp.sin(ang_r), jnp.sin(ang_r), -jnp.sin(ang_c), jnp.sin(ang_c)], axis=-1)
    reps = LANES // head_dim
    return jnp.tile(cos, (1, reps)), jnp.tile(sin, (1, reps))


def _stack_heads(q, n_rep, head_dim):
    if n_rep == 1:
        return q
    return jnp.concatenate([q[:, r * head_dim:(r + 1) * head_dim] for r in range(n_rep)], axis=0)


def _unstack_heads(o, n_rep, rows):
    if n_rep == 1:
        return o
    return jnp.concatenate([o[r * rows:(r + 1) * rows] for r in range(n_rep)], axis=-1)


def _sink_column(sink_ref, g, n_rep, rows):
    return jnp.concatenate([jnp.full((rows, 1), sink_ref[g * n_rep + r] * LOG2E, F32) for r in range(n_rep)],
                           axis=0)


def _ones_width(head_dim):
    return LANES - head_dim % LANES


def _with_ones(v, head_dim):
    return jnp.concatenate([v, jnp.ones((v.shape[0], _ones_width(head_dim)), v.dtype)], axis=1)


def _lane_repeat(col_block, width):
    reps = width // col_block.shape[1]
    return col_block if reps == 1 else jnp.concatenate([col_block] * reps, axis=1)


def _flash_kernel(*refs, n_rep, head_dim, has_sink, n_split):
    if has_sink:
        sink_ref, q_ref, kt_ref, v_ref, o_ref, m_sc, acc_sc = refs
    else:
        q_ref, kt_ref, v_ref, o_ref, m_sc, acc_sc = refs
    kv_head, kj = pl.program_id(1), pl.program_id(3)
    tq = q_ref.shape[1]
    width = acc_sc.shape[1]

    @pl.when(kj == 0)
    def _():
        m_sc[...] = jnp.full(m_sc.shape, NEG_INF, F32)
        acc_sc[...] = jnp.zeros(acc_sc.shape, F32)

    m_all = m_sc[...]
    acc_all = acc_sc[...]
    q = _stack_heads(q_ref[0], n_rep, head_dim)
    kt = kt_ref[0]
    v1 = _with_ones(v_ref[0], head_dim)
    part = n_rep * tq // n_split
    m_out, acc_out = [], []
    for h in range(n_split):
        rs = slice(h * part, (h + 1) * part)
        s = jnp.dot(q[rs], kt, preferred_element_type=F32)
        m_prev = m_all[rs]
        m_new = jnp.maximum(m_prev, jnp.max(s, axis=-1, keepdims=True))
        p = jnp.exp2(s - _lane_repeat(m_new, s.shape[1]))
        alpha = jnp.exp2(m_prev - m_new)
        acc_out.append(_lane_repeat(alpha, width) * acc_all[rs]
                       + jnp.dot(p.astype(BF16), v1, preferred_element_type=F32))
        m_out.append(m_new)
    m_sc[...] = jnp.concatenate(m_out, axis=0)
    acc_sc[...] = jnp.concatenate(acc_out, axis=0)

    @pl.when(kj == pl.num_programs(3) - 1)
    def _():
        acc = acc_sc[...]
        num, l = acc[:, :head_dim], acc[:, head_dim:head_dim + 1]
        if has_sink:
            m = m_sc[:, :1]
            sink = _sink_column(sink_ref, kv_head, n_rep, tq)
            m_fin = jnp.maximum(m, sink)
            scale = jnp.exp2(m - m_fin)
            l = l * scale + jnp.exp2(sink - m_fin)
            num = num * scale
        o_ref[...] = _unstack_heads(num * (1.0 / l), n_rep, tq).astype(o_ref.dtype)


def _flash_attention(q, kt, v, bsz, head_dim, sink=None, tq_target=512, tk_target=3328, n_split=16):
    n_kv, nq, group = q.shape
    n_rep = group // head_dim
    lq, lk = nq // bsz, v.shape[1] // bsz
    tq = _row_tile(lq, tq_target)
    tk = tk_target if lk % tk_target == 0 else _row_tile(lk, 256)
    qt, kt_tiles = lq // tq, lk // tk
    kernel = functools.partial(_flash_kernel, n_rep=n_rep, head_dim=head_dim, has_sink=sink is not None,
                               n_split=n_split)
    in_specs = [pl.BlockSpec((1, tq, group), lambda b, g, i, j: (g, b * qt + i, 0)),
                pl.BlockSpec((1, head_dim, tk), lambda b, g, i, j: (g, 0, b * kt_tiles + j)),
                pl.BlockSpec((1, tk, head_dim), lambda b, g, i, j: (g, b * kt_tiles + j, 0))]
    args = [q, kt, v]
    if sink is not None:
        in_specs = [pl.BlockSpec(memory_space=pltpu.SMEM)] + in_specs
        args = [sink] + args
    return pl.pallas_call(
        kernel,
        grid=(bsz, n_kv, qt, kt_tiles),
        in_specs=in_specs,
        out_specs=pl.BlockSpec((tq, group), lambda b, g, i, j: (b * qt + i, g)),
        out_shape=jax.ShapeDtypeStruct((nq, n_kv * group), BF16),
        scratch_shapes=[pltpu.VMEM((n_rep * tq, LANES), F32),
                        pltpu.VMEM((n_rep * tq, head_dim + _ones_width(head_dim)), F32)],
        compiler_params=_params("parallel", "parallel", "parallel", "arbitrary"),
        name="flash_attention",
    )(*args)


def _window_kernel(sink_ref, q_ref, kp_ref, kc_ref, kn_ref, kx_ref, vp_ref, vc_ref, vn_ref, vx_ref, o_ref,
                   *, n_rep, head_dim, seq):
    kv_head, i = pl.program_id(1), pl.program_id(2)
    tq = q_ref.shape[1]
    band = tq + 2 * WINDOW
    kt = jnp.concatenate([kp_ref[0], kc_ref[0], kn_ref[0], kx_ref[0]], axis=1)
    v1 = _with_ones(jnp.concatenate([vp_ref[0], vc_ref[0], vn_ref[0], vx_ref[0]], axis=0), head_dim)
    n_keys = kt.shape[1]

    qi = lax.broadcasted_iota(jnp.int32, (tq, n_keys), 0)
    kj = lax.broadcasted_iota(jnp.int32, (tq, n_keys), 1)
    key_pos = i * tq - WINDOW + kj
    ok = (kj >= band) | ((jnp.abs(kj - WINDOW - qi) <= WINDOW) & (key_pos >= 0) & (key_pos < seq))

    q = q_ref[0]
    outs = []
    for r in range(n_rep):
        s = jnp.dot(q[:, r * head_dim:(r + 1) * head_dim], kt, preferred_element_type=F32)
        s = jnp.where(ok, s, NEG_INF)
        sink = sink_ref[kv_head * n_rep + r] * LOG2E
        m = jnp.maximum(jnp.max(s, axis=-1, keepdims=True), sink)
        p = jnp.exp2(s - m)
        pv = jnp.dot(p.astype(BF16), v1, preferred_element_type=F32)
        denom = pv[:, head_dim:head_dim + 1] + jnp.exp2(sink - m)
        outs.append(pv[:, :head_dim] * (1.0 / denom))
    o_ref[...] = jnp.concatenate(outs, axis=1).astype(o_ref.dtype)


def _window_attention(q, kt, v, kt_ctx, v_ctx, sink, bsz, head_dim, tq_target=512):
    n_kv, nq, group = q.shape
    n_rep = group // head_dim
    seq = nq // bsz
    n_ctx = v_ctx.shape[1] // bsz
    tq = _row_tile(seq, tq_target)
    qt = seq // tq
    ratio = tq // WINDOW
    halo_blocks = seq // WINDOW
    prev_blk = lambda b, i: b * halo_blocks + jnp.maximum(i * ratio - 1, 0)
    next_blk = lambda b, i: b * halo_blocks + jnp.minimum((i + 1) * ratio, halo_blocks - 1)
    q_map = lambda b, g, i: (g, b * qt + i, 0)
    k_specs = [pl.BlockSpec((1, head_dim, WINDOW), lambda b, g, i: (g, 0, prev_blk(b, i))),
               pl.BlockSpec((1, head_dim, tq), lambda b, g, i: (g, 0, b * qt + i)),
               pl.BlockSpec((1, head_dim, WINDOW), lambda b, g, i: (g, 0, next_blk(b, i))),
               pl.BlockSpec((1, head_dim, n_ctx), lambda b, g, i: (g, 0, b))]
    v_specs = [pl.BlockSpec((1, WINDOW, head_dim), lambda b, g, i: (g, prev_blk(b, i), 0)),
               pl.BlockSpec((1, tq, head_dim), q_map),
               pl.BlockSpec((1, WINDOW, head_dim), lambda b, g, i: (g, next_blk(b, i), 0)),
               pl.BlockSpec((1, n_ctx, head_dim), lambda b, g, i: (g, b, 0))]
    kernel = functools.partial(_window_kernel, n_rep=n_rep, head_dim=head_dim, seq=seq)
    return pl.pallas_call(
        kernel,
        grid=(bsz, n_kv, qt),
        in_specs=[pl.BlockSpec(memory_space=pltpu.SMEM), pl.BlockSpec((1, tq, group), q_map)] + k_specs + v_specs,
        out_specs=pl.BlockSpec((tq, group), lambda b, g, i: (b * qt + i, g)),
        out_shape=jax.ShapeDtypeStruct((nq, n_kv * group), BF16),
        compiler_params=_params("parallel", "parallel", "parallel"),
        name="window_attention",
    )(sink, q, kt, kt, kt, kt_ctx, v, v, v, v_ctx)


def _router_combine(logits_t, bias_ref):
    aff = [_sigmoid(logits_t[e:e + 1, :]) for e in range(N_EXPERTS)]
    sel = [aff[e] + bias_ref[e] for e in range(N_EXPERTS)]
    best_score, best_group = None, None
    for g in range(N_EXPERT_GROUPS):
        a, b, c, d = sel[g * EXPERTS_PER_GROUP:(g + 1) * EXPERTS_PER_GROUP]
        hi1, lo1 = jnp.maximum(a, b), jnp.minimum(a, b)
        hi2, lo2 = jnp.maximum(c, d), jnp.minimum(c, d)
        score = jnp.maximum(hi1, hi2) + jnp.maximum(jnp.minimum(hi1, hi2), jnp.maximum(lo1, lo2))
        if g == 0:
            best_score, best_group = score, jnp.zeros(score.shape, jnp.int32)
        else:
            better = score > best_score
            best_score = jnp.where(better, score, best_score)
            best_group = jnp.where(better, g, best_group)
    gates = []
    for e in range(N_EXPERTS):
        g = e // EXPERTS_PER_GROUP
        beaten = jnp.zeros(best_group.shape, jnp.int32)
        for o in range(g * EXPERTS_PER_GROUP, (g + 1) * EXPERTS_PER_GROUP):
            if o == e:
                continue
            wins = (sel[o] > sel[e]) | ((sel[o] == sel[e]) & (o < e))
            beaten = beaten + wins.astype(jnp.int32)
        chosen = (best_group == g) & (beaten < TOP_K)
        gates.append(jnp.where(chosen, aff[e], 0.0))
    total = functools.reduce(jnp.add, gates)
    inv = 1.0 / total
    return jnp.concatenate([gt * inv for gt in gates], axis=0), best_group


def _post_mixer_kernel(*refs, alpha, glu, route):
    it = iter(refs)
    rb_ref = next(it) if route else None
    a_ref, x_ref = next(it), next(it)
    if glu:
        sc1_ref, sh1_ref, dsk_ref, wv_ref, wg_ref = next(it), next(it), next(it), next(it), next(it)
    else:
        wo_ref = next(it)
    g1_ref, lng_ref, lnb_ref = next(it), next(it), next(it)
    if route:
        sc2_ref, sh2_ref, rw_ref = next(it), next(it), next(it)
    x1_ref = next(it)
    if route:
        h2_ref, comb_ref, grp_ref = next(it), next(it), next(it)

    x = x_ref[...]
    if glu:
        h = x * (1.0 + sc1_ref[0]) + sh1_ref[0]
        act = _gelu_tanh(dsk_ref[...] * h + a_ref[...]).astype(BF16)
        y = (jnp.dot(act, wv_ref[...], preferred_element_type=F32)
             * _sigmoid(jnp.dot(act, wg_ref[...], preferred_element_type=F32)))
    else:
        y = jnp.dot(a_ref[...], wo_ref[...], preferred_element_type=F32)
    x1 = _layer_norm_rows(alpha * x + g1_ref[0] * y, lng_ref[...], lnb_ref[...])
    x1_ref[...] = x1
    if route:
        h2 = x1 * (1.0 + sc2_ref[0]) + sh2_ref[0]
        h2_ref[...] = h2.astype(BF16)
        logits_t = lax.dot_general(rw_ref[...], h2, (((1,), (1,)), ((), ())),
                                   preferred_element_type=F32, precision=HIGHEST)
        comb_ref[...], grp_ref[...] = _router_combine(logits_t, rb_ref)


def _post_mixer(a, x, g1, ln_g, ln_b, alpha, *, w_o=None, glu=None, route=None, tm_target=512):
    n, d = x.shape
    bsz = g1.shape[0]
    per_batch = n // bsz
    tm = _row_tile(per_batch, tm_target)
    tiles_per_batch = per_batch // tm
    row = lambda i: (i, 0)
    batch_vec = lambda i: (i // tiles_per_batch, 0, 0)
    const2 = lambda i: (0, 0)
    vec = pl.BlockSpec((1, 1, d), batch_vec)
    in_specs, args = [], []
    if route is not None:
        in_specs.append(pl.BlockSpec(memory_space=pltpu.SMEM))
        args.append(route[3])
    in_specs += [pl.BlockSpec((tm, d), row), pl.BlockSpec((tm, d), row)]
    args += [a, x]
    if glu is not None:
        sc1, sh1, d_skip, w_val, w_gate = glu
        in_specs += [vec, vec, pl.BlockSpec((1, d), const2),
                     pl.BlockSpec((d, d), const2), pl.BlockSpec((d, d), const2)]
        args += [sc1, sh1, d_skip.reshape(1, d), w_val, w_gate]
    else:
        in_specs.append(pl.BlockSpec((d, d), const2))
        args.append(w_o)
    in_specs += [vec, pl.BlockSpec((1, d), const2), pl.BlockSpec((1, d), const2)]
    args += [g1, ln_g.reshape(1, d), ln_b.reshape(1, d)]
    out_specs = [pl.BlockSpec((tm, d), row)]
    out_shape = [jax.ShapeDtypeStruct((n, d), F32)]
    if route is not None:
        in_specs += [vec, vec, pl.BlockSpec((N_EXPERTS, d), const2)]
        args += [route[0], route[1], route[2]]
        out_specs += [pl.BlockSpec((tm, d), row), pl.BlockSpec((N_EXPERTS, tm), lambda i: (0, i)),
                      pl.BlockSpec((1, tm), lambda i: (0, i))]
        out_shape += [jax.ShapeDtypeStruct((n, d), BF16), jax.ShapeDtypeStruct((N_EXPERTS, n), F32),
                      jax.ShapeDtypeStruct((1, n), jnp.int32)]
    kernel = functools.partial(_post_mixer_kernel, alpha=alpha, glu=glu is not None, route=route is not None)
    return pl.pallas_call(
        kernel,
        grid=(n // tm,),
        in_specs=in_specs,
        out_specs=out_specs,
        out_shape=out_shape,
        compiler_params=_params("parallel"),
        name="post_mixer",
    )(*args)


MOE_BLOCK_ROWS = 256


def _moe_rank_kernel(grp_ref, key_ref, cnt_ref):
    grp = grp_ref[...]
    tm = grp.shape[1]
    member = [grp == g for g in range(N_EXPERT_GROUPS)]
    onehot = jnp.concatenate([m.astype(F32) for m in member]
                             + [jnp.zeros((8 - N_EXPERT_GROUPS, tm), F32)], axis=0)
    earlier = (lax.broadcasted_iota(jnp.int32, (tm, tm), 0)
               < lax.broadcasted_iota(jnp.int32, (tm, tm), 1)).astype(BF16)
    prefix = jnp.dot(onehot.astype(BF16), earlier, preferred_element_type=F32)
    rank = functools.reduce(jnp.add, [jnp.where(member[g], prefix[g:g + 1], 0.0)
                                      for g in range(N_EXPERT_GROUPS)])
    key_ref[...] = (grp * tm).astype(F32) + rank
    cnt_ref[0] = jnp.broadcast_to(jnp.sum(onehot, axis=1, keepdims=True), (8, LANES))


def _moe_kernel(cnt_ref, h_ref, keyr_ref, keyc_ref, cg_ref, w1_ref, w3_ref, w2_ref, x_ref, g2_ref, lng_ref, lnb_ref,
                out_ref, acc_sc, *, alpha):
    i, g = pl.program_id(0), pl.program_id(1)
    tm = h_ref.shape[0]
    rb = MOE_BLOCK_ROWS

    @pl.when(g == 0)
    def _():
        acc_sc[...] = jnp.zeros(acc_sc.shape, F32)

    n_blocks = (cnt_ref[i * N_EXPERT_GROUPS + g] + rb - 1) // rb
    key_row = keyr_ref[...]
    key_col = keyc_ref[...]
    cg = cg_ref[0]
    hi = cg.astype(BF16)
    lo = (cg - hi.astype(F32)).astype(BF16)
    pad = jnp.zeros((tm, LANES - EXPERTS_PER_GROUP), BF16)
    hi, lo = jnp.concatenate([hi, pad], axis=1), jnp.concatenate([lo, pad], axis=1)
    h = h_ref[...]
    block_row = lax.broadcasted_iota(jnp.int32, (rb, tm), 0).astype(F32)
    block_col = lax.broadcasted_iota(jnp.int32, (tm, rb), 1).astype(F32)
    first_key = (g * tm).astype(F32)

    def body(blk, carry):
        base = first_key + (blk * rb).astype(F32)
        gather = (key_row - base == block_row).astype(BF16)
        scatter = (key_col - base == block_col).astype(BF16)
        xs = jnp.dot(gather, h, preferred_element_type=F32).astype(BF16)
        weights = (jnp.dot(gather, hi, preferred_element_type=F32)
                   + jnp.dot(gather, lo, preferred_element_type=F32))
        z = jnp.zeros((rb, h.shape[1]), F32)
        for e in range(EXPERTS_PER_GROUP):
            a1 = jnp.dot(xs, w1_ref[e], preferred_element_type=F32)
            a3 = jnp.dot(xs, w3_ref[e], preferred_element_type=F32)
            act = (a1 * _sigmoid(a1) * a3 * weights[:, e:e + 1]).astype(BF16)
            z = z + jnp.dot(act, w2_ref[e], preferred_element_type=F32)
        acc_sc[...] += jnp.dot(scatter, z.astype(BF16), preferred_element_type=F32)
        return carry

    lax.fori_loop(0, n_blocks, body, 0)

    @pl.when(g == pl.num_programs(1) - 1)
    def _():
        out_ref[...] = _layer_norm_rows(alpha * x_ref[...] + g2_ref[0] * acc_sc[...], lng_ref[...], lnb_ref[...])


def _moe(h2, comb_t, grp, w1, w3, w2, x1, g2, ln_g, ln_b, alpha, tm_target=1024):
    n, d = x1.shape
    bsz = g2.shape[0]
    per_batch = n // bsz
    tm = _row_tile(per_batch, tm_target)
    tiles_per_batch = per_batch // tm
    n_tiles = n // tm
    n_groups = N_EXPERT_GROUPS
    d_expert = w1.shape[2]
    expert_blk = lambda i, g: (g, 0, 0)

    key, cnt = pl.pallas_call(
        _moe_rank_kernel,
        grid=(n_tiles,),
        in_specs=[pl.BlockSpec((1, tm), lambda i: (0, i))],
        out_specs=[pl.BlockSpec((1, tm), lambda i: (0, i)),
                   pl.BlockSpec((1, 8, LANES), lambda i: (i, 0, 0))],
        out_shape=[jax.ShapeDtypeStruct((1, n), F32), jax.ShapeDtypeStruct((n_tiles, 8, LANES), F32)],
        compiler_params=_params("parallel"),
        name="moe_rank",
    )(grp)
    counts = cnt[:, :n_groups, 0].astype(jnp.int32).reshape(-1)
    comb_g = comb_t.reshape(n_groups, EXPERTS_PER_GROUP, n).transpose(0, 2, 1)

    row = lambda i, g: (i, 0)
    const2 = lambda i, g: (0, 0)
    once = pl.Buffered(1)
    kernel = functools.partial(_moe_kernel, alpha=alpha)
    return pl.pallas_call(
        kernel,
        grid=(n_tiles, n_groups),
        in_specs=[pl.BlockSpec(memory_space=pltpu.SMEM),
                  pl.BlockSpec((tm, d), row),
                  pl.BlockSpec((1, tm), lambda i, g: (0, i)),
                  pl.BlockSpec((tm, 1), row),
                  pl.BlockSpec((1, tm, EXPERTS_PER_GROUP), lambda i, g: (g, i, 0)),
                  pl.BlockSpec((EXPERTS_PER_GROUP, d, d_expert), expert_blk),
                  pl.BlockSpec((EXPERTS_PER_GROUP, d, d_expert), expert_blk),
                  pl.BlockSpec((EXPERTS_PER_GROUP, d_expert, d), expert_blk),
                  pl.BlockSpec((tm, d), row, pipeline_mode=once),
                  pl.BlockSpec((1, 1, d), lambda i, g: (i // tiles_per_batch, 0, 0)),
                  pl.BlockSpec((1, d), const2),
                  pl.BlockSpec((1, d), const2)],
        out_specs=pl.BlockSpec((tm, d), row, pipeline_mode=once),
        out_shape=jax.ShapeDtypeStruct((n, d), F32),
        scratch_shapes=[pltpu.VMEM((tm, d), F32)],
        compiler_params=_params("parallel", "arbitrary"),
        name="moe",
    )(counts, h2, key, key.reshape(n, 1), comb_g, w1, w3, w2, x1, g2, ln_g.reshape(1, d), ln_b.reshape(1, d))


def _modulated_slab(u, scale_ref, shift_ref, channel, row, class_starts, d_model):
    scale = scale_ref[channel]
    shift = shift_ref[channel]
    for k, start in enumerate(class_starts[1:], start=1):
        later = row >= start
        scale = jnp.where(later, scale_ref[k * d_model + channel], scale)
        shift = jnp.where(later, shift_ref[k * d_model + channel], shift)
    return (u * scale + shift).astype(BF16)


def _s5_inject_kernel(scale_ref, shift_ref, xt_ref, pin_ref, out_ref, *, class_starts, d_model):
    g = pl.program_id(0)
    rows = xt_ref.shape[1]
    row = lax.broadcasted_iota(jnp.int32, (rows, SCAN_CHUNK), 0)
    u = jnp.concatenate([_modulated_slab(xt_ref[c], scale_ref, shift_ref, g * S5_GROUP_CH + c, row,
                                         class_starts, d_model) for c in range(S5_GROUP_CH)], axis=1)
    out_ref[...] = jnp.dot(u, pin_ref[0], preferred_element_type=F32)


def _s5_carry_kernel(sin_ref, a_ref, b_ref, h_ref, *, bsz, ctx_chunks, lat_chunks):
    half = 2 * S5_STATE
    lat0 = bsz * ctx_chunks

    def advance(h, row, lo):
        h_ref[row, :, lo:lo + half] = h
        a = a_ref[:, lo:lo + half]
        b = b_ref[:, lo:lo + half]
        return a * h + b * pltpu.roll(h, S5_STATE, 1) + sin_ref[row, :, lo:lo + half]

    zero = jnp.zeros((sin_ref.shape[1], half), F32)
    states = []
    for bi in range(bsz):
        hf, hb = zero, zero
        for n in range(ctx_chunks):
            hf = advance(hf, bi * ctx_chunks + n, 0)
            hb = advance(hb, bi * ctx_chunks + ctx_chunks - 1 - n, half)
        states += [hf, hb]

    def body(n, carry):
        out = []
        for bi in range(bsz):
            base = lat0 + bi * lat_chunks
            out.append(advance(carry[2 * bi], base + n, 0))
            out.append(advance(carry[2 * bi + 1], base + lat_chunks - 1 - n, half))
        return tuple(out)

    lax.fori_loop(0, lat_chunks, body, tuple(states))


def _s5_output_kernel(scale_ref, shift_ref, xt_ref, kc_ref, h_ref, pout_ref, out_ref, m_sc,
                      *, class_starts, d_model):
    g = pl.program_id(0)
    rows = xt_ref.shape[1]
    t = SCAN_CHUNK

    def build(cp, carry):
        for c in range(S5_GROUP_CH):
            lags = kc_ref[0, cp, c:c + 1, :]
            shifted = pltpu.roll(jnp.broadcast_to(lags, (t, 2 * t)), 0, 1, stride=1, stride_axis=0)
            m_sc[pl.ds(pl.multiple_of(cp * t, t), t), c * t:(c + 1) * t] = shifted[:, t:].astype(BF16)
        return carry

    lax.fori_loop(0, S5_GROUP_CH, build, 0)

    row = lax.broadcasted_iota(jnp.int32, (rows, t), 0)
    u = jnp.concatenate([_modulated_slab(xt_ref[cp], scale_ref, shift_ref, g * S5_GROUP_CH + cp, row,
                                         class_starts, d_model) for cp in range(S5_GROUP_CH)], axis=1)
    acc = (jnp.dot(u, m_sc[...], preferred_element_type=F32)
           + jnp.dot(h_ref[...].astype(BF16), pout_ref[0], preferred_element_type=F32))
    for c in range(S5_GROUP_CH):
        out_ref[c] = acc[:, c * t:(c + 1) * t]


def _s5_tables(a_re, a_im, log_dt, b_re, b_im, c_re, c_im):
    t = SCAN_CHUNK
    a_re, a_im = a_re.astype(F32), a_im.astype(F32)
    dt = jnp.exp(log_dt.astype(F32))[..., None]
    steps = jnp.arange(t + 1, dtype=F32)[None, None, :, None]
    mag = jnp.exp(steps * (a_re * dt)[:, :, None])
    ang = steps * (a_im * dt)[:, :, None]
    pw_re, pw_im = mag * jnp.cos(ang), mag * jnp.sin(ang)
    lam_re, lam_im = pw_re[:, :, 1], pw_im[:, :, 1]
    inv_den = 1.0 / (a_re * a_re + a_im * a_im)
    n_re = lam_re - 1.0
    f_re = (n_re * a_re + lam_im * a_im) * inv_den
    f_im = (lam_im * a_re - n_re * a_im) * inv_den
    bb_re = f_re[..., None] * b_re - f_im[..., None] * b_im
    bb_im = f_re[..., None] * b_im + f_im[..., None] * b_re

    pt_re = pw_re[:, :, :t].transpose(0, 1, 3, 2)[..., None]
    pt_im = pw_im[:, :, :t].transpose(0, 1, 3, 2)[..., None]
    lb_re = pt_re * bb_re[:, :, :, None] - pt_im * bb_im[:, :, :, None]
    lb_im = pt_re * bb_im[:, :, :, None] + pt_im * bb_re[:, :, :, None]
    resp = (jnp.einsum('dgcp,dgptk->dgkct', c_re, lb_re, precision=HIGHEST)
            - jnp.einsum('dgcp,dgptk->dgkct', c_im, lb_im, precision=HIGHEST))
    fwd, bwd = resp[0], resp[1]
    lags = jnp.concatenate([jnp.zeros_like(fwd[..., :1]), bwd[..., :0:-1],
                            fwd[..., :1] + bwd[..., :1], fwd[..., 1:]], axis=-1)

    bt_re, bt_im = bb_re.transpose(0, 1, 3, 2)[:, :, :, None], bb_im.transpose(0, 1, 3, 2)[:, :, :, None]

    def inject(d, p_re, p_im):
        p_re, p_im = p_re[:, None], p_im[:, None]
        return [p_re * bt_re[d] - p_im * bt_im[d], p_re * bt_im[d] + p_im * bt_re[d]]

    p_in = jnp.concatenate(inject(0, pw_re[0, :, t - 1::-1], pw_im[0, :, t - 1::-1])
                           + inject(1, pw_re[1, :, :t], pw_im[1, :, :t]), axis=-1).astype(BF16)

    ct_re, ct_im = c_re.transpose(0, 1, 3, 2)[..., None], c_im.transpose(0, 1, 3, 2)[..., None]

    def readout(d, p_re, p_im):
        p_re, p_im = p_re.transpose(0, 2, 1)[:, :, None], p_im.transpose(0, 2, 1)[:, :, None]
        return [ct_re[d] * p_re - ct_im[d] * p_im, -(ct_re[d] * p_im + ct_im[d] * p_re)]

    p_out = jnp.concatenate(readout(0, pw_re[0, :, 1:], pw_im[0, :, 1:])
                            + readout(1, pw_re[1, :, :0:-1], pw_im[1, :, :0:-1]), axis=1).astype(BF16)
    p_out = p_out.reshape(p_out.shape[0], 4 * S5_STATE, -1)

    carry_a = jnp.concatenate([pw_re[0, :, t], pw_re[0, :, t], pw_re[1, :, t], pw_re[1, :, t]], axis=-1)
    carry_b = jnp.concatenate([-pw_im[0, :, t], pw_im[0, :, t], -pw_im[1, :, t], pw_im[1, :, t]], axis=-1)
    return lags, p_in, p_out, carry_a, carry_b


def _to_chunk_major(rows2d, bsz):
    n, d = rows2d.shape
    return rows2d.reshape(n // SCAN_CHUNK, SCAN_CHUNK, d).transpose(2, 0, 1)


def _from_chunk_major(xt):
    d, chunks, t = xt.shape
    return xt.transpose(1, 2, 0).reshape(chunks * t, d)


def _s5_scan(x_lat, x_ctx, scale, shift, tables, bsz):
    lags, p_in, p_out, carry_a, carry_b = tables
    d = x_lat.shape[1]
    groups = d // S5_GROUP_CH
    ctx_chunks = x_ctx.shape[0] // bsz // SCAN_CHUNK
    lat_chunks = x_lat.shape[0] // bsz // SCAN_CHUNK
    lat0 = bsz * ctx_chunks
    rows = lat0 + bsz * lat_chunks
    xt = jnp.concatenate([_to_chunk_major(x_ctx, bsz), _to_chunk_major(x_lat, bsz)], axis=1)
    class_starts = (0,) + tuple(lat0 + b * lat_chunks for b in range(bsz))
    scale_flat, shift_flat = scale.reshape(-1), shift.reshape(-1)
    state_w = 4 * S5_STATE
    smem = pl.BlockSpec(memory_space=pltpu.SMEM)
    slab = pl.BlockSpec((S5_GROUP_CH, rows, SCAN_CHUNK), lambda g: (g, 0, 0))

    inject = pl.pallas_call(
        functools.partial(_s5_inject_kernel, class_starts=class_starts, d_model=d),
        grid=(groups,),
        in_specs=[smem, smem, slab,
                  pl.BlockSpec((1, S5_GROUP_CH * SCAN_CHUNK, state_w), lambda g: (g, 0, 0))],
        out_specs=pl.BlockSpec((rows, state_w), lambda g: (0, g)),
        out_shape=jax.ShapeDtypeStruct((rows, groups * state_w), F32),
        compiler_params=_params("parallel"),
        name="s5_inject",
    )(scale_flat, shift_flat, xt, p_in.reshape(groups, S5_GROUP_CH * SCAN_CHUNK, state_w))

    gb = 8
    entering = pl.pallas_call(
        functools.partial(_s5_carry_kernel, bsz=bsz, ctx_chunks=ctx_chunks, lat_chunks=lat_chunks),
        grid=(groups // gb,),
        in_specs=[pl.BlockSpec((rows, gb, state_w), lambda g: (0, g, 0)),
                  pl.BlockSpec((gb, state_w), lambda g: (g, 0)),
                  pl.BlockSpec((gb, state_w), lambda g: (g, 0))],
        out_specs=pl.BlockSpec((rows, gb, state_w), lambda g: (0, g, 0)),
        out_shape=jax.ShapeDtypeStruct((rows, groups, state_w), F32),
        compiler_params=_params("parallel"),
        name="s5_carry",
    )(inject.reshape(rows, groups, state_w), carry_a, carry_b)

    yt = pl.pallas_call(
        functools.partial(_s5_output_kernel, class_starts=class_starts, d_model=d),
        grid=(groups,),
        in_specs=[smem, smem, slab,
                  pl.BlockSpec((1, S5_GROUP_CH, S5_GROUP_CH, 2 * SCAN_CHUNK), lambda g: (g, 0, 0, 0)),
                  pl.BlockSpec((rows, state_w), lambda g: (0, g)),
                  pl.BlockSpec((1, state_w, S5_GROUP_CH * SCAN_CHUNK), lambda g: (g, 0, 0))],
        out_specs=slab,
        out_shape=jax.ShapeDtypeStruct((d, rows, SCAN_CHUNK), F32),
        scratch_shapes=[pltpu.VMEM((S5_GROUP_CH * SCAN_CHUNK, S5_GROUP_CH * SCAN_CHUNK), BF16)],
        compiler_params=_params("parallel"),
        name="s5_output",
    )(scale_flat, shift_flat, xt, lags, entering.reshape(rows, groups * state_w), p_out)

    return _from_chunk_major(yt[:, lat0:]), _from_chunk_major(yt[:, :lat0])


def kernel(x, c, ctx, c_ctx, mod_w, mod_b, ln_g, ln_b, router_w, router_b, moe_w1, moe_w3, moe_w2, s5_a_re, s5_a_im, s5_log_dt, s5_b_re, s5_b_im, s5_c_re, s5_c_im, s5_d, s5_w_gate, s5_w_val, swa_w_qkv, swa_w_o, swa_sink, gqa_w_qkv, gqa_w_o, gqa_q_norm, gqa_k_norm):
    bsz, seq, d = x.shape
    n_ctx = ctx.shape[1]
    depth = mod_w.shape[0]
    alpha = (2.0 * depth) ** 0.25
    heads_b, heads_c = d // HEAD_DIM_B, d // HEAD_DIM_C

    cond = jnp.concatenate([c_ctx[None, :], c], axis=0)
    mods = _modulation(cond, mod_w, mod_b).reshape(depth, bsz + 1, 6, d)
    xl = x.reshape(bsz * seq, d)
    xc = ctx.reshape(bsz * n_ctx, d)
    router_w_t = router_w.T
    rope_b = _rope_tables(seq, HEAD_DIM_B)
    rope_c = _rope_tables(seq, HEAD_DIM_C)

    for i in range(depth):
        kind, j = i % N_MIXERS, i // N_MIXERS
        ctx_out = i < depth - 1
        lat_mod = [mods[i, 1:, k][:, None, :] for k in range(6)]
        ctx_mod = [jnp.broadcast_to(mods[i, 0, k][None, None, :], (bsz, 1, d)) for k in range(6)]
        ln1 = (ln_g[i, 0], ln_b[i, 0])
        ln2 = (ln_g[i, 1], ln_b[i, 1])
        buffers = [(xl, lat_mod, True)] + ([(xc, ctx_mod, False)] if ctx_out else [])

        if kind == 0:
            tables = _s5_tables(s5_a_re[j], s5_a_im[j], s5_log_dt[j], s5_b_re[j], s5_b_im[j],
                                s5_c_re[j], s5_c_im[j])
            y_lat, y_ctx = _s5_scan(xl, xc, 1.0 + mods[i, :, 1], mods[i, :, 0], tables, bsz)
            glu_w = (s5_d[j], s5_w_val[j].astype(BF16), s5_w_gate[j].astype(BF16))
            mixed = {True: y_lat, False: y_ctx}
        else:
            if kind == 1:
                w_qkv, w_o = swa_w_qkv[j].astype(BF16), swa_w_o[j].astype(BF16)
                heads, kv, dh, norms, rope = heads_b, KV_HEADS_B, HEAD_DIM_B, None, rope_b
            else:
                w_qkv, w_o = gqa_w_qkv[j].astype(BF16), gqa_w_o[j].astype(BF16)
                heads, kv, dh, norms, rope = heads_c, KV_HEADS_C, HEAD_DIM_C, (gqa_q_norm[j], gqa_k_norm[j]), rope_c
            ql, kl, vl = _qkv_project(xl, lat_mod[1], lat_mod[0], w_qkv, heads, kv, dh, norms=norms, rope=rope)
            qc, kc, vc = _qkv_project(xc, ctx_mod[1], ctx_mod[0], w_qkv, heads, kv, dh, norms=norms, rope=None)
            if kind == 1:
                sink = swa_sink[j].astype(F32)
                o_lat = _window_attention(ql, kl, vl, kc, vc, sink, bsz, dh)
                o_ctx = _flash_attention(qc, kc, vc, bsz, dh, sink=sink) if ctx_out else None
            else:
                keys = jnp.concatenate([kl.reshape(kv, dh, bsz, seq), kc.reshape(kv, dh, bsz, n_ctx)], axis=3)
                vals = jnp.concatenate([vl.reshape(kv, bsz, seq, dh), vc.reshape(kv, bsz, n_ctx, dh)], axis=2)
                o_lat = _flash_attention(ql, keys.reshape(kv, dh, -1), vals.reshape(kv, -1, dh), bsz, dh)
                o_ctx = _flash_attention(qc, kc, vc, bsz, dh) if ctx_out else None
            mixed = {True: o_lat, False: o_ctx}

        w1, w3, w2 = moe_w1[i].astype(BF16), moe_w3[i].astype(BF16), moe_w2[i].astype(BF16)
        new = []
        for rows, mod, is_lat in buffers:
            sh1, sc1, g1, sh2, sc2, g2 = mod
            route = (sc2, sh2, router_w_t, router_b.astype(F32))
            if kind == 0:
                x1, h2, comb_t, grp = _post_mixer(mixed[is_lat], rows, g1, *ln1, alpha,
                                                  glu=(sc1, sh1) + glu_w, route=route)
            else:
                x1, h2, comb_t, grp = _post_mixer(mixed[is_lat], rows, g1, *ln1, alpha, w_o=w_o, route=route)
            new.append(_moe(h2, comb_t, grp, w1, w3, w2, x1, g2, *ln2, alpha))
        xl = new[0]
        if ctx_out:
            xc = new[1]
    return xl.reshape(bsz, seq, d)
```

```python
import functools
import math

import jax
import jax.numpy as jnp
from jax import lax
from jax.experimental import pallas as pl
from jax.experimental.pallas import tpu as pltpu

GRID_W = 64
N_MIXERS = 3
LN_EPS = 1e-6
RMS_EPS = 1e-6
NEG_INF = -1e30
ROPE_THETA = 10000.0
S5_GROUP_CH = 16
S5_STATE = 64
SCAN_CHUNK = 128
WINDOW = 128
HEAD_DIM_B = 64
KV_HEADS_B = 2
HEAD_DIM_C = 128
KV_HEADS_C = 2
N_EXPERTS = 16
N_EXPERT_GROUPS = 4
EXPERTS_PER_GROUP = N_EXPERTS // N_EXPERT_GROUPS
TOP_K = 2

LANES = 128
LOG2E = math.log2(math.e)
VMEM_LIMIT_BYTES = 56 * 1024 * 1024

F32 = jnp.float32
BF16 = jnp.bfloat16
HIGHEST = lax.Precision.HIGHEST


def _params(*semantics):
    return pltpu.CompilerParams(dimension_semantics=semantics,
                                vmem_limit_bytes=VMEM_LIMIT_BYTES)


def _row_tile(n_rows, target):
    t = min(target, n_rows)
    while n_rows % t:
        t //= 2
    return t


def _layer_norm_rows(v, g, b):
    vc = v - jnp.mean(v, axis=-1, keepdims=True)
    var = jnp.mean(vc * vc, axis=-1, keepdims=True)
    return vc * lax.rsqrt(var + LN_EPS) * g + b


def _sigmoid(v):
    return 1.0 / (1.0 + jnp.exp(-v))


def _gelu_tanh(v):
    return 0.5 * v * (1.0 + jnp.tanh(math.sqrt(2.0 / math.pi) * (v + 0.044715 * (v * v * v))))


def _modulation_kernel(cond_ref, w_ref, b_ref, out_ref):
    cond = cond_ref[...]
    act = cond * _sigmoid(cond)
    out_ref[0] = jnp.dot(act, w_ref[0], preferred_element_type=F32, precision=HIGHEST) + b_ref[0]


def _modulation(cond, mod_w, mod_b):
    depth, d, n_out = mod_w.shape
    rows = cond.shape[0]
    tn = 1536
    return pl.pallas_call(
        _modulation_kernel,
        grid=(depth, n_out // tn),
        in_specs=[pl.BlockSpec((rows, d), lambda i, j: (0, 0)),
                  pl.BlockSpec((1, d, tn), lambda i, j: (i, 0, j)),
                  pl.BlockSpec((1, 1, tn), lambda i, j: (i, 0, j))],
        out_specs=pl.BlockSpec((1, rows, tn), lambda i, j: (i, 0, j)),
        out_shape=jax.ShapeDtypeStruct((depth, rows, n_out), F32),
        compiler_params=_params("parallel", "parallel"),
        name="modulation",
    )(cond, mod_w, mod_b.reshape(depth, 1, n_out))


def _swap_rotary_pairs(v, half):
    width = v.shape[-1]
    lane = lax.broadcasted_iota(jnp.int32, v.shape, v.ndim - 1)
    first = (lane // half) % 2 == 0
    return jnp.where(first, pltpu.roll(v, width - half, v.ndim - 1), pltpu.roll(v, half, v.ndim - 1))


def _qkv_kernel(*refs, n_heads, n_kv, head_dim, qk_norm, use_rope):
    it = iter(refs)
    x_ref, sc_ref, sh_ref, w_ref = next(it), next(it), next(it), next(it)
    if qk_norm:
        qn_ref, kn_ref = next(it), next(it)
    if use_rope:
        cos_ref, sin_ref = next(it), next(it)
    q_ref, kt_ref, v_ref = next(it), next(it), next(it)

    h = (x_ref[...] * (1.0 + sc_ref[0]) + sh_ref[0]).astype(BF16)
    proj = jnp.dot(h, w_ref[...], preferred_element_type=F32)
    q_width, kv_width = n_heads * head_dim, n_kv * head_dim
    q = proj[:, :q_width]
    k = proj[:, q_width:q_width + kv_width]
    v = proj[:, q_width + kv_width:]

    def rms_heads(t, gain, count):
        parts = []
        for hd in range(count):
            th = t[:, hd * head_dim:(hd + 1) * head_dim]
            ms = jnp.mean(th * th, axis=-1, keepdims=True)
            parts.append(th * lax.rsqrt(ms + RMS_EPS) * gain)
        return jnp.concatenate(parts, axis=-1)

    if qk_norm:
        q = rms_heads(q, qn_ref[...], n_heads)
        k = rms_heads(k, kn_ref[...], n_kv)

    if use_rope:
        cos, sin = cos_ref[...], sin_ref[...]

        def rope(t):
            reps = t.shape[-1] // LANES
            c = jnp.concatenate([cos] * reps, axis=-1) if reps > 1 else cos
            s = jnp.concatenate([sin] * reps, axis=-1) if reps > 1 else sin
            return t * c + _swap_rotary_pairs(t, head_dim // 4) * s

        q, k = rope(q), rope(k)

    q = q * (head_dim ** -0.5 * LOG2E)
    k_t = k.T
    group = q_width // n_kv
    for g in range(n_kv):
        q_ref[g] = q[:, g * group:(g + 1) * group].astype(BF16)
        kt_ref[g] = k_t[g * head_dim:(g + 1) * head_dim, :].astype(BF16)
        v_ref[g] = v[:, g * head_dim:(g + 1) * head_dim].astype(BF16)


def _qkv_project(x, sc, sh, w, n_heads, n_kv, head_dim, norms=None, rope=None, tm_target=512):
    n, d = x.shape
    bsz = sc.shape[0]
    per_batch = n // bsz
    tm = _row_tile(per_batch, tm_target)
    tiles_per_batch = per_batch // tm
    n_out = w.shape[1]
    group = n_heads * head_dim // n_kv
    row = lambda i: (i, 0)
    batch_vec = lambda i: (i // tiles_per_batch, 0, 0)
    const2 = lambda i: (0, 0)
    in_specs = [pl.BlockSpec((tm, d), row),
                pl.BlockSpec((1, 1, d), batch_vec),
                pl.BlockSpec((1, 1, d), batch_vec),
                pl.BlockSpec((d, n_out), const2)]
    args = [x, sc, sh, w]
    if norms is not None:
        in_specs += [pl.BlockSpec((1, head_dim), const2)] * 2
        args += [norms[0].reshape(1, head_dim), norms[1].reshape(1, head_dim)]
    if rope is not None:
        in_specs += [pl.BlockSpec((tm, LANES), lambda i: (i % tiles_per_batch, 0))] * 2
        args += [rope[0], rope[1]]
    kernel = functools.partial(_qkv_kernel, n_heads=n_heads, n_kv=n_kv, head_dim=head_dim,
                               qk_norm=norms is not None, use_rope=rope is not None)
    return pl.pallas_call(
        kernel,
        grid=(n // tm,),
        in_specs=in_specs,
        out_specs=[pl.BlockSpec((n_kv, tm, group), lambda i: (0, i, 0)),
                   pl.BlockSpec((n_kv, head_dim, tm), lambda i: (0, 0, i)),
                   pl.BlockSpec((n_kv, tm, head_dim), lambda i: (0, i, 0))],
        out_shape=[jax.ShapeDtypeStruct((n_kv, n, group), BF16),
                   jax.ShapeDtypeStruct((n_kv, head_dim, n), BF16),
                   jax.ShapeDtypeStruct((n_kv, n, head_dim), BF16)],
        compiler_params=_params("parallel"),
        name="qkv_project",
    )(*args)


def _rope_tables(seq, head_dim):
    quarter = head_dim // 4
    inv_freq = ROPE_THETA ** (-jnp.arange(quarter, dtype=F32) / quarter)
    t = jnp.arange(seq)
    rows = (t // GRID_W).astype(F32)
    cols = (t % GRID_W).astype(F32)
    ang_r = rows[:, None] * inv_freq
    ang_c = cols[:, None] * inv_freq
    cos = jnp.concatenate([jnp.cos(ang_r)] * 2 + [jnp.cos(ang_c)] * 2, axis=-1)
    sin = jnp.concatenate([-jnp.sin(ang_r), jnp.sin(ang_r), -jnp.sin(ang_c), jnp.sin(ang_c)], axis=-1)
    reps = LANES // head_dim
    return jnp.tile(cos, (1, reps)), jnp.tile(sin, (1, reps))


def _stack_heads(q, n_rep, head_dim):
    if n_rep == 1:
        return q
    return jnp.concatenate([q[:, r * head_dim:(r + 1) * head_dim] for r in range(n_rep)], axis=0)


def _unstack_heads(o, n_rep, rows):
    if n_rep == 1:
        return o
    return jnp.concatenate([o[r * rows:(r + 1) * rows] for r in range(n_rep)], axis=-1)


def _sink_column(sink_ref, g, n_rep, rows):
    return jnp.concatenate([jnp.full((rows, 1), sink_ref[g * n_rep + r] * LOG2E, F32) for r in range(n_rep)],
                           axis=0)


def _ones_width(head_dim):
    return LANES - head_dim % LANES


def _with_ones(v, head_dim):
    return jnp.concatenate([v, jnp.ones((v.shape[0], _ones_width(head_dim)), v.dtype)], axis=1)


def _lane_repeat(col_block, width):
    reps = width // col_block.shape[1]
    return col_block if reps == 1 else jnp.concatenate([col_block] * reps, axis=1)


def _flash_kernel(*refs, n_rep, head_dim, has_sink, n_split):
    if has_sink:
        sink_ref, q_ref, kt_ref, v_ref, o_ref, m_sc, acc_sc = refs
    else:
        q_ref, kt_ref, v_ref, o_ref, m_sc, acc_sc = refs
    kv_head, kj = pl.program_id(1), pl.program_id(3)
    tq = q_ref.shape[1]
    width = acc_sc.shape[1]

    @pl.when(kj == 0)
    def _():
        m_sc[...] = jnp.full(m_sc.shape, NEG_INF, F32)
        acc_sc[...] = jnp.zeros(acc_sc.shape, F32)

    m_all = m_sc[...]
    acc_all = acc_sc[...]
    q = _stack_heads(q_ref[0], n_rep, head_dim)
    kt = kt_ref[0]
    v1 = _with_ones(v_ref[0], head_dim)
    part = n_rep * tq // n_split
    m_out, acc_out = [], []
    for h in range(n_split):
        rs = slice(h * part, (h + 1) * part)
        s = jnp.dot(q[rs], kt, preferred_element_type=F32)
        m_prev = m_all[rs]
        m_new = jnp.maximum(m_prev, jnp.max(s, axis=-1, keepdims=True))
        p = jnp.exp2(s - _lane_repeat(m_new, s.shape[1]))
        alpha = jnp.exp2(m_prev - m_new)
        acc_out.append(_lane_repeat(alpha, width) * acc_all[rs]
                       + jnp.dot(p.astype(BF16), v1, preferred_element_type=F32))
        m_out.append(m_new)
    m_sc[...] = jnp.concatenate(m_out, axis=0)
    acc_sc[...] = jnp.concatenate(acc_out, axis=0)

    @pl.when(kj == pl.num_programs(3) - 1)
    def _():
        acc = acc_sc[...]
        num, l = acc[:, :head_dim], acc[:, head_dim:head_dim + 1]
        if has_sink:
            m = m_sc[:, :1]
            sink = _sink_column(sink_ref, kv_head, n_rep, tq)
            m_fin = jnp.maximum(m, sink)
            scale = jnp.exp2(m - m_fin)
            l = l * scale + jnp.exp2(sink - m_fin)
            num = num * scale
        o_ref[...] = _unstack_heads(num * (1.0 / l), n_rep, tq).astype(o_ref.dtype)


def _flash_attention(q, kt, v, bsz, head_dim, sink=None, tq_target=512, tk_target=3328, n_split=16):
    n_kv, nq, group = q.shape
    n_rep = group // head_dim
    lq, lk = nq // bsz, v.shape[1] // bsz
    tq = _row_tile(lq, tq_target)
    tk = tk_target if lk % tk_target == 0 else _row_tile(lk, 256)
    qt, kt_tiles = lq // tq, lk // tk
    kernel = functools.partial(_flash_kernel, n_rep=n_rep, head_dim=head_dim, has_sink=sink is not None,
                               n_split=n_split)
    in_specs = [pl.BlockSpec((1, tq, group), lambda b, g, i, j: (g, b * qt + i, 0)),
                pl.BlockSpec((1, head_dim, tk), lambda b, g, i, j: (g, 0, b * kt_tiles + j)),
                pl.BlockSpec((1, tk, head_dim), lambda b, g, i, j: (g, b * kt_tiles + j, 0))]
    args = [q, kt, v]
    if sink is not None:
        in_specs = [pl.BlockSpec(memory_space=pltpu.SMEM)] + in_specs
        args = [sink] + args
    return pl.pallas_call(
        kernel,
        grid=(bsz, n_kv, qt, kt_tiles),
        in_specs=in_specs,
        out_specs=pl.BlockSpec((tq, group), lambda b, g, i, j: (b * qt + i, g)),
        out_shape=jax.ShapeDtypeStruct((nq, n_kv * group), BF16),
        scratch_shapes=[pltpu.VMEM((n_rep * tq, LANES), F32),
                        pltpu.VMEM((n_rep * tq, head_dim + _ones_width(head_dim)), F32)],
        compiler_params=_params("parallel", "parallel", "parallel", "arbitrary"),
        name="flash_attention",
    )(*args)


def _window_kernel(sink_ref, q_ref, kp_ref, kc_ref, kn_ref, kx_ref, vp_ref, vc_ref, vn_ref, vx_ref, o_ref,
                   *, n_rep, head_dim, seq):
    kv_head, i = pl.program_id(1), pl.program_id(2)
    tq = q_ref.shape[1]
    band = tq + 2 * WINDOW
    kt = jnp.concatenate([kp_ref[0], kc_ref[0], kn_ref[0], kx_ref[0]], axis=1)
    v1 = _with_ones(jnp.concatenate([vp_ref[0], vc_ref[0], vn_ref[0], vx_ref[0]], axis=0), head_dim)
    n_keys = kt.shape[1]

    qi = lax.broadcasted_iota(jnp.int32, (tq, n_keys), 0)
    kj = lax.broadcasted_iota(jnp.int32, (tq, n_keys), 1)
    key_pos = i * tq - WINDOW + kj
    ok = (kj >= band) | ((jnp.abs(kj - WINDOW - qi) <= WINDOW) & (key_pos >= 0) & (key_pos < seq))

    q = q_ref[0]
    outs = []
    for r in range(n_rep):
        s = jnp.dot(q[:, r * head_dim:(r + 1) * head_dim], kt, preferred_element_type=F32)
        s = jnp.where(ok, s, NEG_INF)
        sink = sink_ref[kv_head * n_rep + r] * LOG2E
        m = jnp.maximum(jnp.max(s, axis=-1, keepdims=True), sink)
        p = jnp.exp2(s - m)
        pv = jnp.dot(p.astype(BF16), v1, preferred_element_type=F32)
        denom = pv[:, head_dim:head_dim + 1] + jnp.exp2(sink - m)
        outs.append(pv[:, :head_dim] * (1.0 / denom))
    o_ref[...] = jnp.concatenate(outs, axis=1).astype(o_ref.dtype)


def _window_attention(q, kt, v, kt_ctx, v_ctx, sink, bsz, head_dim, tq_target=512):
    n_kv, nq, group = q.shape
    n_rep = group // head_dim
    seq = nq // bsz
    n_ctx = v_ctx.shape[1] // bsz
    tq = _row_tile(seq, tq_target)
    qt = seq // tq
    ratio = tq // WINDOW
    halo_blocks = seq // WINDOW
    prev_blk = lambda b, i: b * halo_blocks + jnp.maximum(i * ratio - 1, 0)
    next_blk = lambda b, i: b * halo_blocks + jnp.minimum((i + 1) * ratio, halo_blocks - 1)
    q_map = lambda b, g, i: (g, b * qt + i, 0)
    k_specs = [pl.BlockSpec((1, head_dim, WINDOW), lambda b, g, i: (g, 0, prev_blk(b, i))),
               pl.BlockSpec((1, head_dim, tq), lambda b, g, i: (g, 0, b * qt + i)),
               pl.BlockSpec((1, head_dim, WINDOW), lambda b, g, i: (g, 0, next_blk(b, i))),
               pl.BlockSpec((1, head_dim, n_ctx), lambda b, g, i: (g, 0, b))]
    v_specs = [pl.BlockSpec((1, WINDOW, head_dim), lambda b, g, i: (g, prev_blk(b, i), 0)),
               pl.BlockSpec((1, tq, head_dim), q_map),
               pl.BlockSpec((1, WINDOW, head_dim), lambda b, g, i: (g, next_blk(b, i), 0)),
               pl.BlockSpec((1, n_ctx, head_dim), lambda b, g, i: (g, b, 0))]
    kernel = functools.partial(_window_kernel, n_rep=n_rep, head_dim=head_dim, seq=seq)
    return pl.pallas_call(
        kernel,
        grid=(bsz, n_kv, qt),
        in_specs=[pl.BlockSpec(memory_space=pltpu.SMEM), pl.BlockSpec((1, tq, group), q_map)] + k_specs + v_specs,
        out_specs=pl.BlockSpec((tq, group), lambda b, g, i: (b * qt + i, g)),
        out_shape=jax.ShapeDtypeStruct((nq, n_kv * group), BF16),
        compiler_params=_params("parallel", "parallel", "parallel"),
        name="window_attention",
    )(sink, q, kt, kt, kt, kt_ctx, v, v, v, v_ctx)


def _router_combine(logits_t, bias_ref):
    aff = [_sigmoid(logits_t[e:e + 1, :]) for e in range(N_EXPERTS)]
    sel = [aff[e] + bias_ref[e] for e in range(N_EXPERTS)]
    best_score, best_group = None, None
    for g in range(N_EXPERT_GROUPS):
        a, b, c, d = sel[g * EXPERTS_PER_GROUP:(g + 1) * EXPERTS_PER_GROUP]
        hi1, lo1 = jnp.maximum(a, b), jnp.minimum(a, b)
        hi2, lo2 = jnp.maximum(c, d), jnp.minimum(c, d)
        score = jnp.maximum(hi1, hi2) + jnp.maximum(jnp.minimum(hi1, hi2), jnp.maximum(lo1, lo2))
        if g == 0:
            best_score, best_group = score, jnp.zeros(score.shape, jnp.int32)
        else:
            better = score > best_score
            best_score = jnp.where(better, score, best_score)
            best_group = jnp.where(better, g, best_group)
    gates = []
    for e in range(N_EXPERTS):
        g = e // EXPERTS_PER_GROUP
        beaten = jnp.zeros(best_group.shape, jnp.int32)
        for o in range(g * EXPERTS_PER_GROUP, (g + 1) * EXPERTS_PER_GROUP):
            if o == e:
                continue
            wins = (sel[o] > sel[e]) | ((sel[o] == sel[e]) & (o < e))
            beaten = beaten + wins.astype(jnp.int32)
        chosen = (best_group == g) & (beaten < TOP_K)
        gates.append(jnp.where(chosen, aff[e], 0.0))
    total = functools.reduce(jnp.add, gates)
    inv = 1.0 / total
    return jnp.concatenate([gt * inv for gt in gates], axis=0), best_group


def _post_mixer_kernel(*refs, alpha, glu, route):
    it = iter(refs)
    rb_ref = next(it) if route else None
    a_ref, x_ref = next(it), next(it)
    if glu:
        sc1_ref, sh1_ref, dsk_ref, wv_ref, wg_ref = next(it), next(it), next(it), next(it), next(it)
    else:
        wo_ref = next(it)
    g1_ref, lng_ref, lnb_ref = next(it), next(it), next(it)
    if route:
        sc2_ref, sh2_ref, rw_ref = next(it), next(it), next(it)
    x1_ref = next(it)
    if route:
        h2_ref, comb_ref, grp_ref = next(it), next(it), next(it)

    x = x_ref[...]
    if glu:
        h = x * (1.0 + sc1_ref[0]) + sh1_ref[0]
        act = _gelu_tanh(dsk_ref[...] * h + a_ref[...]).astype(BF16)
        y = (jnp.dot(act, wv_ref[...], preferred_element_type=F32)
             * _sigmoid(jnp.dot(act, wg_ref[...], preferred_element_type=F32)))
    else:
        y = jnp.dot(a_ref[...], wo_ref[...], preferred_element_type=F32)
    x1 = _layer_norm_rows(alpha * x + g1_ref[0] * y, lng_ref[...], lnb_ref[...])
    x1_ref[...] = x1
    if route:
        h2 = x1 * (1.0 + sc2_ref[0]) + sh2_ref[0]
        h2_ref[...] = h2.astype(BF16)
        logits_t = lax.dot_general(rw_ref[...], h2, (((1,), (1,)), ((), ())),
                                   preferred_element_type=F32, precision=HIGHEST)
        comb_ref[...], grp_ref[...] = _router_combine(logits_t, rb_ref)


def _post_mixer(a, x, g1, ln_g, ln_b, alpha, *, w_o=None, glu=None, route=None, tm_target=512):
    n, d = x.shape
    bsz = g1.shape[0]
    per_batch = n // bsz
    tm = _row_tile(per_batch, tm_target)
    tiles_per_batch = per_batch // tm
    row = lambda i: (i, 0)
    batch_vec = lambda i: (i // tiles_per_batch, 0, 0)
    const2 = lambda i: (0, 0)
    vec = pl.BlockSpec((1, 1, d), batch_vec)
    in_specs, args = [], []
    if route is not None:
        in_specs.append(pl.BlockSpec(memory_space=pltpu.SMEM))
        args.append(route[3])
    in_specs += [pl.BlockSpec((tm, d), row), pl.BlockSpec((tm, d), row)]
    args += [a, x]
    if glu is not None:
        sc1, sh1, d_skip, w_val, w_gate = glu
        in_specs += [vec, vec, pl.BlockSpec((1, d), const2),
                     pl.BlockSpec((d, d), const2), pl.BlockSpec((d, d), const2)]
        args += [sc1, sh1, d_skip.reshape(1, d), w_val, w_gate]
    else:
        in_specs.append(pl.BlockSpec((d, d), const2))
        args.append(w_o)
    in_specs += [vec, pl.BlockSpec((1, d), const2), pl.BlockSpec((1, d), const2)]
    args += [g1, ln_g.reshape(1, d), ln_b.reshape(1, d)]
    out_specs = [pl.BlockSpec((tm, d), row)]
    out_shape = [jax.ShapeDtypeStruct((n, d), F32)]
    if route is not None:
        in_specs += [vec, vec, pl.BlockSpec((N_EXPERTS, d), const2)]
        args += [route[0], route[1], route[2]]
        out_specs += [pl.BlockSpec((tm, d), row), pl.BlockSpec((N_EXPERTS, tm), lambda i: (0, i)),
                      pl.BlockSpec((1, tm), lambda i: (0, i))]
        out_shape += [jax.ShapeDtypeStruct((n, d), BF16), jax.ShapeDtypeStruct((N_EXPERTS, n), F32),
                      jax.ShapeDtypeStruct((1, n), jnp.int32)]
    kernel = functools.partial(_post_mixer_kernel, alpha=alpha, glu=glu is not None, route=route is not None)
    return pl.pallas_call(
        kernel,
        grid=(n // tm,),
        in_specs=in_specs,
        out_specs=out_specs,
        out_shape=out_shape,
        compiler_params=_params("parallel"),
        name="post_mixer",
    )(*args)


MOE_BLOCK_ROWS = 288


def _moe_rank_kernel(grp_ref, key_ref, cnt_ref):
    grp = grp_ref[...]
    tm = grp.shape[1]
    member = [grp == g for g in range(N_EXPERT_GROUPS)]
    onehot = jnp.concatenate([m.astype(F32) for m in member]
                             + [jnp.zeros((8 - N_EXPERT_GROUPS, tm), F32)], axis=0)
    earlier = (lax.broadcasted_iota(jnp.int32, (tm, tm), 0)
               < lax.broadcasted_iota(jnp.int32, (tm, tm), 1)).astype(BF16)
    prefix = jnp.dot(onehot.astype(BF16), earlier, preferred_element_type=F32)
    rank = functools.reduce(jnp.add, [jnp.where(member[g], prefix[g:g + 1], 0.0)
                                      for g in range(N_EXPERT_GROUPS)])
    key_ref[...] = (grp * (2 * tm)).astype(F32) + rank
    cnt_ref[0] = jnp.broadcast_to(jnp.sum(onehot, axis=1, keepdims=True), (8, LANES))


def _moe_kernel(cnt_ref, h_ref, keyr_ref, keyc_ref, cg_ref, w1_ref, w3_ref, w2_ref, x_ref, g2_ref, lng_ref, lnb_ref,
                out_ref, acc_sc, *, alpha):
    i, g = pl.program_id(0), pl.program_id(1)
    tm = h_ref.shape[0]
    rb = MOE_BLOCK_ROWS

    @pl.when(g == 0)
    def _():
        acc_sc[...] = jnp.zeros(acc_sc.shape, F32)

    n_blocks = (cnt_ref[i * N_EXPERT_GROUPS + g] + rb - 1) // rb
    key_row = keyr_ref[...]
    key_col = keyc_ref[...]
    cg = cg_ref[0]
    hi = cg.astype(BF16)
    lo = (cg - hi.astype(F32)).astype(BF16)
    pad = jnp.zeros((tm, LANES - EXPERTS_PER_GROUP), BF16)
    hi, lo = jnp.concatenate([hi, pad], axis=1), jnp.concatenate([lo, pad], axis=1)
    h = h_ref[...]
    block_row = lax.broadcasted_iota(jnp.int32, (rb, tm), 0).astype(F32)
    block_col = lax.broadcasted_iota(jnp.int32, (tm, rb), 1).astype(F32)
    first_key = (g * (2 * tm)).astype(F32)

    def body(blk, carry):
        base = first_key + (blk * rb).astype(F32)
        gather = (key_row - base == block_row).astype(BF16)
        scatter = (key_col - base == block_col).astype(BF16)
        xs = jnp.dot(gather, h, preferred_element_type=F32).astype(BF16)
        weights = (jnp.dot(gather, hi, preferred_element_type=F32)
                   + jnp.dot(gather, lo, preferred_element_type=F32))
        z = jnp.zeros((rb, h.shape[1]), F32)
        for e in range(EXPERTS_PER_GROUP):
            a1 = jnp.dot(xs, w1_ref[e], preferred_element_type=F32)
            a3 = jnp.dot(xs, w3_ref[e], preferred_element_type=F32)
            act = (a1 * _sigmoid(a1) * a3 * weights[:, e:e + 1]).astype(BF16)
            z = z + jnp.dot(act, w2_ref[e], preferred_element_type=F32)
        acc_sc[...] += jnp.dot(scatter, z.astype(BF16), preferred_element_type=F32)
        return carry

    lax.fori_loop(0, n_blocks, body, 0)

    @pl.when(g == pl.num_programs(1) - 1)
    def _():
        out_ref[...] = _layer_norm_rows(alpha * x_ref[...] + g2_ref[0] * acc_sc[...], lng_ref[...], lnb_ref[...])


def _moe(h2, comb_t, grp, w1, w3, w2, x1, g2, ln_g, ln_b, alpha, tm_target=1024):
    n, d = x1.shape
    bsz = g2.shape[0]
    per_batch = n // bsz
    tm = _row_tile(per_batch, tm_target)
    tiles_per_batch = per_batch // tm
    n_tiles = n // tm
    n_groups = N_EXPERT_GROUPS
    d_expert = w1.shape[2]
    expert_blk = lambda i, g: (g, 0, 0)

    key, cnt = pl.pallas_call(
        _moe_rank_kernel,
        grid=(n_tiles,),
        in_specs=[pl.BlockSpec((1, tm), lambda i: (0, i))],
        out_specs=[pl.BlockSpec((1, tm), lambda i: (0, i)),
                   pl.BlockSpec((1, 8, LANES), lambda i: (i, 0, 0))],
        out_shape=[jax.ShapeDtypeStruct((1, n), F32), jax.ShapeDtypeStruct((n_tiles, 8, LANES), F32)],
        compiler_params=_params("parallel"),
        name="moe_rank",
    )(grp)
    counts = cnt[:, :n_groups, 0].astype(jnp.int32).reshape(-1)
    comb_g = comb_t.reshape(n_groups, EXPERTS_PER_GROUP, n).transpose(0, 2, 1)

    row = lambda i, g: (i, 0)
    const2 = lambda i, g: (0, 0)
    once = pl.Buffered(1)
    kernel = functools.partial(_moe_kernel, alpha=alpha)
    return pl.pallas_call(
        kernel,
        grid=(n_tiles, n_groups),
        in_specs=[pl.BlockSpec(memory_space=pltpu.SMEM),
                  pl.BlockSpec((tm, d), row),
                  pl.BlockSpec((1, tm), lambda i, g: (0, i)),
                  pl.BlockSpec((tm, 1), row),
                  pl.BlockSpec((1, tm, EXPERTS_PER_GROUP), lambda i, g: (g, i, 0)),
                  pl.BlockSpec((EXPERTS_PER_GROUP, d, d_expert), expert_blk),
                  pl.BlockSpec((EXPERTS_PER_GROUP, d, d_expert), expert_blk),
                  pl.BlockSpec((EXPERTS_PER_GROUP, d_expert, d), expert_blk),
                  pl.BlockSpec((tm, d), row, pipeline_mode=once),
                  pl.BlockSpec((1, 1, d), lambda i, g: (i // tiles_per_batch, 0, 0)),
                  pl.BlockSpec((1, d), const2),
                  pl.BlockSpec((1, d), const2)],
        out_specs=pl.BlockSpec((tm, d), row, pipeline_mode=once),
        out_shape=jax.ShapeDtypeStruct((n, d), F32),
        scratch_shapes=[pltpu.VMEM((tm, d), F32)],
        compiler_params=_params("parallel", "arbitrary"),
        name="moe",
    )(counts, h2, key, key.reshape(n, 1), comb_g, w1, w3, w2, x1, g2, ln_g.reshape(1, d), ln_b.reshape(1, d))


def _modulated_rows(lat_ref, ctx_ref, scale_ref, shift_ref, g, row, class_starts, d_model):
    slabs = []
    for c in range(S5_GROUP_CH):
        channel = g * S5_GROUP_CH + c
        u = jnp.concatenate([lat_ref[c], ctx_ref[c]], axis=0)
        scale = scale_ref[channel]
        shift = shift_ref[channel]
        for k, start in enumerate(class_starts[1:], start=1):
            later = row >= start
            scale = jnp.where(later, scale_ref[k * d_model + channel], scale)
            shift = jnp.where(later, shift_ref[k * d_model + channel], shift)
        slabs.append((u * scale + shift).astype(BF16))
    return jnp.concatenate(slabs, axis=1)


def _s5_inject_kernel(scale_ref, shift_ref, lat_ref, ctx_ref, pin_ref, out_ref, *, class_starts, d_model):
    rows = out_ref.shape[0]
    row = lax.broadcasted_iota(jnp.int32, (rows, SCAN_CHUNK), 0)
    u = _modulated_rows(lat_ref, ctx_ref, scale_ref, shift_ref, pl.program_id(0), row, class_starts, d_model)
    out_ref[...] = jnp.dot(u, pin_ref[0], preferred_element_type=F32)


def _s5_carry_kernel(sin_ref, a_ref, b_ref, h_ref, *, bsz, ctx_chunks, lat_chunks):
    half = 2 * S5_STATE
    ctx0 = bsz * lat_chunks

    def advance(h, row, lo):
        h_ref[row, :, lo:lo + half] = h
        a = a_ref[:, lo:lo + half]
        b = b_ref[:, lo:lo + half]
        return a * h + b * pltpu.roll(h, S5_STATE, 1) + sin_ref[row, :, lo:lo + half]

    zero = jnp.zeros((sin_ref.shape[1], half), F32)
    states = []
    for bi in range(bsz):
        hf, hb = zero, zero
        for n in range(ctx_chunks):
            hf = advance(hf, ctx0 + bi * ctx_chunks + n, 0)
            hb = advance(hb, ctx0 + bi * ctx_chunks + ctx_chunks - 1 - n, half)
        states += [hf, hb]

    def body(n, carry):
        out = []
        for bi in range(bsz):
            base = bi * lat_chunks
            out.append(advance(carry[2 * bi], base + n, 0))
            out.append(advance(carry[2 * bi + 1], base + lat_chunks - 1 - n, half))
        return tuple(out)

    lax.fori_loop(0, lat_chunks, body, tuple(states))


def _s5_output_kernel(scale_ref, shift_ref, lat_ref, ctx_ref, kc_ref, h_ref, pout_ref, out_lat_ref, out_ctx_ref,
                      m_sc, *, class_starts, d_model):
    rows = h_ref.shape[0]
    lat_rows = lat_ref.shape[1]
    t = SCAN_CHUNK

    def build(cp, carry):
        for c in range(S5_GROUP_CH):
            lags = kc_ref[0, cp, c:c + 1, :]
            shifted = pltpu.roll(jnp.broadcast_to(lags, (t, 2 * t)), 0, 1, stride=1, stride_axis=0)
            m_sc[pl.ds(pl.multiple_of(cp * t, t), t), c * t:(c + 1) * t] = shifted[:, t:].astype(BF16)
        return carry

    lax.fori_loop(0, S5_GROUP_CH, build, 0)

    row = lax.broadcasted_iota(jnp.int32, (rows, t), 0)
    u = _modulated_rows(lat_ref, ctx_ref, scale_ref, shift_ref, pl.program_id(0), row, class_starts, d_model)
    acc = jnp.dot(u, m_sc[...], preferred_element_type=F32)
    h = h_ref[...].astype(BF16)
    for c in range(S5_GROUP_CH):
        y = acc[:, c * t:(c + 1) * t] + jnp.dot(h, pout_ref[0, c], preferred_element_type=F32)
        out_lat_ref[c] = y[:lat_rows]
        out_ctx_ref[c] = y[lat_rows:]


def _s5_tables(a_re, a_im, log_dt, b_re, b_im, c_re, c_im):
    t = SCAN_CHUNK
    a_re, a_im = a_re.astype(F32), a_im.astype(F32)
    dt = jnp.exp(log_dt.astype(F32))[..., None]
    steps = jnp.arange(t + 1, dtype=F32)[None, None, :, None]
    mag = jnp.exp(steps * (a_re * dt)[:, :, None])
    ang = steps * (a_im * dt)[:, :, None]
    pw_re, pw_im = mag * jnp.cos(ang), mag * jnp.sin(ang)
    lam_re, lam_im = pw_re[:, :, 1], pw_im[:, :, 1]
    inv_den = 1.0 / (a_re * a_re + a_im * a_im)
    n_re = lam_re - 1.0
    f_re = (n_re * a_re + lam_im * a_im) * inv_den
    f_im = (lam_im * a_re - n_re * a_im) * inv_den
    bb_re = f_re[..., None] * b_re - f_im[..., None] * b_im
    bb_im = f_re[..., None] * b_im + f_im[..., None] * b_re

    pt_re = pw_re[:, :, :t].transpose(0, 1, 3, 2)[..., None]
    pt_im = pw_im[:, :, :t].transpose(0, 1, 3, 2)[..., None]
    lb_re = pt_re * bb_re[:, :, :, None] - pt_im * bb_im[:, :, :, None]
    lb_im = pt_re * bb_im[:, :, :, None] + pt_im * bb_re[:, :, :, None]
    resp = (jnp.einsum('dgcp,dgptk->dgkct', c_re, lb_re, precision=HIGHEST)
            - jnp.einsum('dgcp,dgptk->dgkct', c_im, lb_im, precision=HIGHEST))
    fwd, bwd = resp[0], resp[1]
    lags = jnp.concatenate([jnp.zeros_like(fwd[..., :1]), bwd[..., :0:-1],
                            fwd[..., :1] + bwd[..., :1], fwd[..., 1:]], axis=-1)

    def lanes(f_a, f_b, b_a, b_b):
        return jnp.concatenate([f_a, f_b, b_a, b_b], axis=-1)

    in_re = lanes(pw_re[0, :, t - 1::-1], pw_re[0, :, t - 1::-1], pw_re[1, :, :t], pw_re[1, :, :t])
    in_im = lanes(pw_im[0, :, t - 1::-1], pw_im[0, :, t - 1::-1], pw_im[1, :, :t], pw_im[1, :, :t])
    bt_re, bt_im = bb_re.transpose(0, 1, 3, 2), bb_im.transpose(0, 1, 3, 2)
    with_re = lanes(bt_re[0], bt_im[0], bt_re[1], bt_im[1])
    with_im = lanes(-bt_im[0], bt_re[0], -bt_im[1], bt_re[1])
    p_in = (in_re[:, None] * with_re[:, :, None] + in_im[:, None] * with_im[:, :, None]).astype(BF16)
    p_in = p_in.reshape(p_in.shape[0], -1, 4 * S5_STATE)

    def rows(f_a, f_b, b_a, b_b):
        return jnp.concatenate([f_a, f_b, b_a, b_b], axis=-2)

    fr, fi = pw_re[0, :, 1:].transpose(0, 2, 1), pw_im[0, :, 1:].transpose(0, 2, 1)
    br, bi = pw_re[1, :, :0:-1].transpose(0, 2, 1), pw_im[1, :, :0:-1].transpose(0, 2, 1)
    out_x, out_y = rows(fr, fi, br, bi), rows(fi, fr, bi, br)
    cr, ci = c_re[:, :, :, :, None], c_im[:, :, :, :, None]
    with_x = rows(cr[0], -cr[0], cr[1], -cr[1])
    with_y = rows(-ci[0], -ci[0], -ci[1], -ci[1])
    p_out = (with_x * out_x[:, None] + with_y * out_y[:, None]).astype(BF16)

    carry_a = lanes(pw_re[0, :, t], pw_re[0, :, t], pw_re[1, :, t], pw_re[1, :, t])
    carry_b = lanes(-pw_im[0, :, t], pw_im[0, :, t], -pw_im[1, :, t], pw_im[1, :, t])
    return lags, p_in, p_out, carry_a, carry_b


def _to_chunk_major(rows2d):
    n, d = rows2d.shape
    return rows2d.reshape(n // SCAN_CHUNK, SCAN_CHUNK, d).transpose(2, 0, 1)


def _from_chunk_major(xt):
    d, chunks, t = xt.shape
    return xt.transpose(1, 2, 0).reshape(chunks * t, d)


def _s5_scan(x_lat, x_ctx, scale, shift, tables, bsz):
    lags, p_in, p_out, carry_a, carry_b = tables
    d = x_lat.shape[1]
    groups = d // S5_GROUP_CH
    ctx_chunks = x_ctx.shape[0] // bsz // SCAN_CHUNK
    lat_chunks = x_lat.shape[0] // bsz // SCAN_CHUNK
    lat_rows, ctx_rows = bsz * lat_chunks, bsz * ctx_chunks
    rows = lat_rows + ctx_rows
    xt_lat, xt_ctx = _to_chunk_major(x_lat), _to_chunk_major(x_ctx)
    class_starts = tuple(b * lat_chunks for b in range(bsz)) + (lat_rows,)
    scale_flat, shift_flat = scale.reshape(-1), shift.reshape(-1)
    state_w = 4 * S5_STATE
    smem = pl.BlockSpec(memory_space=pltpu.SMEM)
    slab_lat = pl.BlockSpec((S5_GROUP_CH, lat_rows, SCAN_CHUNK), lambda g: (g, 0, 0))
    slab_ctx = pl.BlockSpec((S5_GROUP_CH, ctx_rows, SCAN_CHUNK), lambda g: (g, 0, 0))

    inject = pl.pallas_call(
        functools.partial(_s5_inject_kernel, class_starts=class_starts, d_model=d),
        grid=(groups,),
        in_specs=[smem, smem, slab_lat, slab_ctx,
                  pl.BlockSpec((1, S5_GROUP_CH * SCAN_CHUNK, state_w), lambda g: (g, 0, 0))],
        out_specs=pl.BlockSpec((rows, state_w), lambda g: (0, g)),
        out_shape=jax.ShapeDtypeStruct((rows, groups * state_w), F32),
        compiler_params=_params("parallel"),
        name="s5_inject",
    )(scale_flat, shift_flat, xt_lat, xt_ctx, p_in)

    gb = 8
    entering = pl.pallas_call(
        functools.partial(_s5_carry_kernel, bsz=bsz, ctx_chunks=ctx_chunks, lat_chunks=lat_chunks),
        grid=(groups // gb,),
        in_specs=[pl.BlockSpec((rows, gb, state_w), lambda g: (0, g, 0)),
                  pl.BlockSpec((gb, state_w), lambda g: (g, 0)),
                  pl.BlockSpec((gb, state_w), lambda g: (g, 0))],
        out_specs=pl.BlockSpec((rows, gb, state_w), lambda g: (0, g, 0)),
        out_shape=jax.ShapeDtypeStruct((rows, groups, state_w), F32),
        compiler_params=_params("parallel"),
        name="s5_carry",
    )(inject.reshape(rows, groups, state_w), carry_a, carry_b)

    yt_lat, yt_ctx = pl.pallas_call(
        functools.partial(_s5_output_kernel, class_starts=class_starts, d_model=d),
        grid=(groups,),
        in_specs=[smem, smem, slab_lat, slab_ctx,
                  pl.BlockSpec((1, S5_GROUP_CH, S5_GROUP_CH, 2 * SCAN_CHUNK), lambda g: (g, 0, 0, 0)),
                  pl.BlockSpec((rows, state_w), lambda g: (0, g)),
                  pl.BlockSpec((1, S5_GROUP_CH, state_w, SCAN_CHUNK), lambda g: (g, 0, 0, 0))],
        out_specs=[slab_lat, slab_ctx],
        out_shape=[jax.ShapeDtypeStruct((d, lat_rows, SCAN_CHUNK), F32),
                   jax.ShapeDtypeStruct((d, ctx_rows, SCAN_CHUNK), F32)],
        scratch_shapes=[pltpu.VMEM((S5_GROUP_CH * SCAN_CHUNK, S5_GROUP_CH * SCAN_CHUNK), BF16)],
        compiler_params=_params("parallel"),
        name="s5_output",
    )(scale_flat, shift_flat, xt_lat, xt_ctx, lags, entering.reshape(rows, groups * state_w), p_out)

    return _from_chunk_major(yt_lat), _from_chunk_major(yt_ctx)


def kernel(x, c, ctx, c_ctx, mod_w, mod_b, ln_g, ln_b, router_w, router_b, moe_w1, moe_w3, moe_w2, s5_a_re, s5_a_im, s5_log_dt, s5_b_re, s5_b_im, s5_c_re, s5_c_im, s5_d, s5_w_gate, s5_w_val, swa_w_qkv, swa_w_o, swa_sink, gqa_w_qkv, gqa_w_o, gqa_q_norm, gqa_k_norm):
    bsz, seq, d = x.shape
    n_ctx = ctx.shape[1]
    depth = mod_w.shape[0]
    alpha = (2.0 * depth) ** 0.25
    heads_b, heads_c = d // HEAD_DIM_B, d // HEAD_DIM_C

    cond = jnp.concatenate([c_ctx[None, :], c], axis=0)
    mods = _modulation(cond, mod_w, mod_b).reshape(depth, bsz + 1, 6, d)
    xl = x.reshape(bsz * seq, d)
    xc = ctx.reshape(bsz * n_ctx, d)
    router_w_t = router_w.T
    rope_b = _rope_tables(seq, HEAD_DIM_B)
    rope_c = _rope_tables(seq, HEAD_DIM_C)

    for i in range(depth):
        kind, j = i % N_MIXERS, i // N_MIXERS
        ctx_out = i < depth - 1
        lat_mod = [mods[i, 1:, k][:, None, :] for k in range(6)]
        ctx_mod = [jnp.broadcast_to(mods[i, 0, k][None, None, :], (bsz, 1, d)) for k in range(6)]
        ln1 = (ln_g[i, 0], ln_b[i, 0])
        ln2 = (ln_g[i, 1], ln_b[i, 1])
        buffers = [(xl, lat_mod, True)] + ([(xc, ctx_mod, False)] if ctx_out else [])

        if kind == 0:
            tables = _s5_tables(s5_a_re[j], s5_a_im[j], s5_log_dt[j], s5_b_re[j], s5_b_im[j],
                                s5_c_re[j], s5_c_im[j])
            ctx_last = jnp.roll(mods[i], -1, axis=0)
            y_lat, y_ctx = _s5_scan(xl, xc, 1.0 + ctx_last[:, 1], ctx_last[:, 0], tables, bsz)
            glu_w = (s5_d[j], s5_w_val[j].astype(BF16), s5_w_gate[j].astype(BF16))
            mixed = {True: y_lat, False: y_ctx}
        else:
            if kind == 1:
                w_qkv, w_o = swa_w_qkv[j].astype(BF16), swa_w_o[j].astype(BF16)
                heads, kv, dh, norms, rope = heads_b, KV_HEADS_B, HEAD_DIM_B, None, rope_b
            else:
                w_qkv, w_o = gqa_w_qkv[j].astype(BF16), gqa_w_o[j].astype(BF16)
                heads, kv, dh, norms, rope = heads_c, KV_HEADS_C, HEAD_DIM_C, (gqa_q_norm[j], gqa_k_norm[j]), rope_c
            ql, kl, vl = _qkv_project(xl, lat_mod[1], lat_mod[0], w_qkv, heads, kv, dh, norms=norms, rope=rope)
            qc, kc, vc = _qkv_project(xc, ctx_mod[1], ctx_mod[0], w_qkv, heads, kv, dh, norms=norms, rope=None)
            if kind == 1:
                sink = swa_sink[j].astype(F32)
                o_lat = _window_attention(ql, kl, vl, kc, vc, sink, bsz, dh)
                o_ctx = _flash_attention(qc, kc, vc, bsz, dh, sink=sink) if ctx_out else None
            else:
                keys = jnp.concatenate([kl.reshape(kv, dh, bsz, seq), kc.reshape(kv, dh, bsz, n_ctx)], axis=3)
                vals = jnp.concatenate([vl.reshape(kv, bsz, seq, dh), vc.reshape(kv, bsz, n_ctx, dh)], axis=2)
                o_lat = _flash_attention(ql, keys.reshape(kv, dh, -1), vals.reshape(kv, -1, dh), bsz, dh)
                o_ctx = _flash_attention(qc, kc, vc, bsz, dh) if ctx_out else None
            mixed = {True: o_lat, False: o_ctx}

        w1, w3, w2 = moe_w1[i].astype(BF16), moe_w3[i].astype(BF16), moe_w2[i].astype(BF16)
        new = []
        for rows, mod, is_lat in buffers:
            sh1, sc1, g1, sh2, sc2, g2 = mod
            route = (sc2, sh2, router_w_t, router_b.astype(F32))
            if kind == 0:
                x1, h2, comb_t, grp = _post_mixer(mixed[is_lat], rows, g1, *ln1, alpha,
                                                  glu=(sc1, sh1) + glu_w, route=route)
            else:
                x1, h2, comb_t, grp = _post_mixer(mixed[is_lat], rows, g1, *ln1, alpha, w_o=w_o, route=route)
            new.append(_moe(h2, comb_t, grp, w1, w3, w2, x1, g2, *ln2, alpha))
        xl = new[0]
        if ctx_out:
            xc = new[1]
    return xl.reshape(bsz, seq, d)
```

```python
import functools
import math

import jax
import jax.numpy as jnp
from jax import lax
from jax.experimental import pallas as pl
from jax.experimental.pallas import tpu as pltpu

GRID_W = 64
N_MIXERS = 3
LN_EPS = 1e-6
RMS_EPS = 1e-6
NEG_INF = -1e30
ROPE_THETA = 10000.0
S5_GROUP_CH = 16
S5_STATE = 64
SCAN_CHUNK = 128
WINDOW = 128
HEAD_DIM_B = 64
KV_HEADS_B = 2
HEAD_DIM_C = 128
KV_HEADS_C = 2
N_EXPERTS = 16
N_EXPERT_GROUPS = 4
EXPERTS_PER_GROUP = N_EXPERTS // N_EXPERT_GROUPS
TOP_K = 2

LANES = 128
LOG2E = math.log2(math.e)
VMEM_LIMIT_BYTES = 56 * 1024 * 1024

F32 = jnp.float32
BF16 = jnp.bfloat16
HIGHEST = lax.Precision.HIGHEST


def _params(*semantics):
    return pltpu.CompilerParams(dimension_semantics=semantics,
                                vmem_limit_bytes=VMEM_LIMIT_BYTES)


def _row_tile(n_rows, target):
    t = min(target, n_rows)
    while n_rows % t:
        t //= 2
    return t


def _layer_norm_rows(v, g, b):
    vc = v - jnp.mean(v, axis=-1, keepdims=True)
    var = jnp.mean(vc * vc, axis=-1, keepdims=True)
    return vc * lax.rsqrt(var + LN_EPS) * g + b


def _sigmoid(v):
    return 1.0 / (1.0 + jnp.exp(-v))


def _gelu_tanh(v):
    return 0.5 * v * (1.0 + jnp.tanh(math.sqrt(2.0 / math.pi) * (v + 0.044715 * (v * v * v))))


def _modulation_kernel(cond_ref, w_ref, b_ref, out_ref):
    cond = cond_ref[...]
    act = cond * _sigmoid(cond)
    out_ref[0] = jnp.dot(act, w_ref[0], preferred_element_type=F32, precision=HIGHEST) + b_ref[0]


def _modulation(cond, mod_w, mod_b):
    depth, d, n_out = mod_w.shape
    rows = cond.shape[0]
    tn = 1536
    return pl.pallas_call(
        _modulation_kernel,
        grid=(depth, n_out // tn),
        in_specs=[pl.BlockSpec((rows, d), lambda i, j: (0, 0)),
                  pl.BlockSpec((1, d, tn), lambda i, j: (i, 0, j)),
                  pl.BlockSpec((1, 1, tn), lambda i, j: (i, 0, j))],
        out_specs=pl.BlockSpec((1, rows, tn), lambda i, j: (i, 0, j)),
        out_shape=jax.ShapeDtypeStruct((depth, rows, n_out), F32),
        compiler_params=_params("parallel", "parallel"),
        name="modulation",
    )(cond, mod_w, mod_b.reshape(depth, 1, n_out))


def _swap_rotary_pairs(v, half):
    width = v.shape[-1]
    lane = lax.broadcasted_iota(jnp.int32, v.shape, v.ndim - 1)
    first = (lane // half) % 2 == 0
    return jnp.where(first, pltpu.roll(v, width - half, v.ndim - 1), pltpu.roll(v, half, v.ndim - 1))


def _qkv_kernel(*refs, n_heads, n_kv, head_dim, qk_norm, use_rope):
    it = iter(refs)
    x_ref, sc_ref, sh_ref, w_ref = next(it), next(it), next(it), next(it)
    if qk_norm:
        qn_ref, kn_ref = next(it), next(it)
    if use_rope:
        cos_ref, sin_ref = next(it), next(it)
    q_ref, kt_ref, v_ref = next(it), next(it), next(it)

    h = (x_ref[...] * (1.0 + sc_ref[0]) + sh_ref[0]).astype(BF16)
    proj = jnp.dot(h, w_ref[...], preferred_element_type=F32)
    q_width, kv_width = n_heads * head_dim, n_kv * head_dim
    q = proj[:, :q_width]
    k = proj[:, q_width:q_width + kv_width]
    v = proj[:, q_width + kv_width:]

    def rms_heads(t, gain, count):
        parts = []
        for hd in range(count):
            th = t[:, hd * head_dim:(hd + 1) * head_dim]
            ms = jnp.mean(th * th, axis=-1, keepdims=True)
            parts.append(th * lax.rsqrt(ms + RMS_EPS) * gain)
        return jnp.concatenate(parts, axis=-1)

    if qk_norm:
        q = rms_heads(q, qn_ref[...], n_heads)
        k = rms_heads(k, kn_ref[...], n_kv)

    if use_rope:
        cos, sin = cos_ref[...], sin_ref[...]

        def rope(t):
            reps = t.shape[-1] // LANES
            c = jnp.concatenate([cos] * reps, axis=-1) if reps > 1 else cos
            s = jnp.concatenate([sin] * reps, axis=-1) if reps > 1 else sin
            return t * c + _swap_rotary_pairs(t, head_dim // 4) * s

        q, k = rope(q), rope(k)

    q = q * (head_dim ** -0.5 * LOG2E)
    k_t = k.T
    group = q_width // n_kv
    for g in range(n_kv):
        q_ref[g] = q[:, g * group:(g + 1) * group].astype(BF16)
        kt_ref[g] = k_t[g * head_dim:(g + 1) * head_dim, :].astype(BF16)
        v_ref[g] = v[:, g * head_dim:(g + 1) * head_dim].astype(BF16)


def _qkv_project(x, sc, sh, w, n_heads, n_kv, head_dim, norms=None, rope=None, tm_target=512):
    n, d = x.shape
    bsz = sc.shape[0]
    per_batch = n // bsz
    tm = _row_tile(per_batch, tm_target)
    tiles_per_batch = per_batch // tm
    n_out = w.shape[1]
    group = n_heads * head_dim // n_kv
    row = lambda i: (i, 0)
    batch_vec = lambda i: (i // tiles_per_batch, 0, 0)
    const2 = lambda i: (0, 0)
    in_specs = [pl.BlockSpec((tm, d), row),
                pl.BlockSpec((1, 1, d), batch_vec),
                pl.BlockSpec((1, 1, d), batch_vec),
                pl.BlockSpec((d, n_out), const2)]
    args = [x, sc, sh, w]
    if norms is not None:
        in_specs += [pl.BlockSpec((1, head_dim), const2)] * 2
        args += [norms[0].reshape(1, head_dim), norms[1].reshape(1, head_dim)]
    if rope is not None:
        in_specs += [pl.BlockSpec((tm, LANES), lambda i: (i % tiles_per_batch, 0))] * 2
        args += [rope[0], rope[1]]
    kernel = functools.partial(_qkv_kernel, n_heads=n_heads, n_kv=n_kv, head_dim=head_dim,
                               qk_norm=norms is not None, use_rope=rope is not None)
    return pl.pallas_call(
        kernel,
        grid=(n // tm,),
        in_specs=in_specs,
        out_specs=[pl.BlockSpec((n_kv, tm, group), lambda i: (0, i, 0)),
                   pl.BlockSpec((n_kv, head_dim, tm), lambda i: (0, 0, i)),
                   pl.BlockSpec((n_kv, tm, head_dim), lambda i: (0, i, 0))],
        out_shape=[jax.ShapeDtypeStruct((n_kv, n, group), BF16),
                   jax.ShapeDtypeStruct((n_kv, head_dim, n), BF16),
                   jax.ShapeDtypeStruct((n_kv, n, head_dim), BF16)],
        compiler_params=_params("parallel"),
        name="qkv_project",
    )(*args)


def _rope_tables(seq, head_dim):
    quarter = head_dim // 4
    inv_freq = ROPE_THETA ** (-jnp.arange(quarter, dtype=F32) / quarter)
    t = jnp.arange(seq)
    rows = (t // GRID_W).astype(F32)
    cols = (t % GRID_W).astype(F32)
    ang_r = rows[:, None] * inv_freq
    ang_c = cols[:, None] * inv_freq
    cos = jnp.concatenate([jnp.cos(ang_r)] * 2 + [jnp.cos(ang_c)] * 2, axis=-1)
    sin = jnp.concatenate([-jnp.sin(ang_r), jnp.sin(ang_r), -jnp.sin(ang_c), jnp.sin(ang_c)], axis=-1)
    reps = LANES // head_dim
    return jnp.tile(cos, (1, reps)), jnp.tile(sin, (1, reps))


def _stack_heads(q, n_rep, head_dim):
    if n_rep == 1:
        return q
    return jnp.concatenate([q[:, r * head_dim:(r + 1) * head_dim] for r in range(n_rep)], axis=0)


def _unstack_heads(o, n_rep, rows):
    if n_rep == 1:
        return o
    return jnp.concatenate([o[r * rows:(r + 1) * rows] for r in range(n_rep)], axis=-1)


def _sink_column(sink_ref, g, n_rep, rows):
    return jnp.concatenate([jnp.full((rows, 1), sink_ref[g * n_rep + r] * LOG2E, F32) for r in range(n_rep)],
                           axis=0)


def _ones_width(head_dim):
    return LANES - head_dim % LANES


def _with_ones(v, head_dim):
    return jnp.concatenate([v, jnp.ones((v.shape[0], _ones_width(head_dim)), v.dtype)], axis=1)


def _lane_repeat(col_block, width):
    reps = width // col_block.shape[1]
    return col_block if reps == 1 else jnp.concatenate([col_block] * reps, axis=1)


def _flash_kernel(*refs, n_rep, head_dim, has_sink, n_split):
    if has_sink:
        sink_ref, q_ref, kt_ref, v_ref, o_ref, m_sc, acc_sc = refs
    else:
        q_ref, kt_ref, v_ref, o_ref, m_sc, acc_sc = refs
    kv_head, kj = pl.program_id(1), pl.program_id(3)
    tq = q_ref.shape[1]
    width = acc_sc.shape[1]

    @pl.when(kj == 0)
    def _():
        m_sc[...] = jnp.full(m_sc.shape, NEG_INF, F32)
        acc_sc[...] = jnp.zeros(acc_sc.shape, F32)

    m_all = m_sc[...]
    acc_all = acc_sc[...]
    q = _stack_heads(q_ref[0], n_rep, head_dim)
    kt = kt_ref[0]
    v1 = _with_ones(v_ref[0], head_dim)
    part = n_rep * tq // n_split
    m_out, acc_out = [], []
    for h in range(n_split):
        rs = slice(h * part, (h + 1) * part)
        s = jnp.dot(q[rs], kt, preferred_element_type=F32)
        m_prev = m_all[rs]
        m_new = jnp.maximum(m_prev, jnp.max(s, axis=-1, keepdims=True))
        p = jnp.exp2(s - _lane_repeat(m_new, s.shape[1]))
        alpha = jnp.exp2(m_prev - m_new)
        acc_out.append(_lane_repeat(alpha, width) * acc_all[rs]
                       + jnp.dot(p.astype(BF16), v1, preferred_element_type=F32))
        m_out.append(m_new)
    m_sc[...] = jnp.concatenate(m_out, axis=0)
    acc_sc[...] = jnp.concatenate(acc_out, axis=0)

    @pl.when(kj == pl.num_programs(3) - 1)
    def _():
        acc = acc_sc[...]
        num, l = acc[:, :head_dim], acc[:, head_dim:head_dim + 1]
        if has_sink:
            m = m_sc[:, :1]
            sink = _sink_column(sink_ref, kv_head, n_rep, tq)
            m_fin = jnp.maximum(m, sink)
            scale = jnp.exp2(m - m_fin)
            l = l * scale + jnp.exp2(sink - m_fin)
            num = num * scale
        o_ref[...] = _unstack_heads(num * (1.0 / l), n_rep, tq).astype(o_ref.dtype)


def _flash_attention(q, kt, v, bsz, head_dim, sink=None, tq_target=512, tk_target=3328, n_split=16):
    n_kv, nq, group = q.shape
    n_rep = group // head_dim
    lq, lk = nq // bsz, v.shape[1] // bsz
    tq = _row_tile(lq, tq_target)
    tk = tk_target if lk % tk_target == 0 else _row_tile(lk, 256)
    qt, kt_tiles = lq // tq, lk // tk
    kernel = functools.partial(_flash_kernel, n_rep=n_rep, head_dim=head_dim, has_sink=sink is not None,
                               n_split=n_split)
    in_specs = [pl.BlockSpec((1, tq, group), lambda b, g, i, j: (g, b * qt + i, 0)),
                pl.BlockSpec((1, head_dim, tk), lambda b, g, i, j: (g, 0, b * kt_tiles + j)),
                pl.BlockSpec((1, tk, head_dim), lambda b, g, i, j: (g, b * kt_tiles + j, 0))]
    args = [q, kt, v]
    if sink is not None:
        in_specs = [pl.BlockSpec(memory_space=pltpu.SMEM)] + in_specs
        args = [sink] + args
    return pl.pallas_call(
        kernel,
        grid=(bsz, n_kv, qt, kt_tiles),
        in_specs=in_specs,
        out_specs=pl.BlockSpec((tq, group), lambda b, g, i, j: (b * qt + i, g)),
        out_shape=jax.ShapeDtypeStruct((nq, n_kv * group), BF16),
        scratch_shapes=[pltpu.VMEM((n_rep * tq, LANES), F32),
                        pltpu.VMEM((n_rep * tq, head_dim + _ones_width(head_dim)), F32)],
        compiler_params=_params("parallel", "parallel", "parallel", "arbitrary"),
        name="flash_attention",
    )(*args)


def _window_kernel(sink_ref, q_ref, kp_ref, kc_ref, kn_ref, kx_ref, vp_ref, vc_ref, vn_ref, vx_ref, o_ref,
                   *, n_rep, head_dim, seq):
    kv_head, i = pl.program_id(1), pl.program_id(2)
    tq = q_ref.shape[1]
    band = tq + 2 * WINDOW
    kt = jnp.concatenate([kp_ref[0], kc_ref[0], kn_ref[0], kx_ref[0]], axis=1)
    v1 = _with_ones(jnp.concatenate([vp_ref[0], vc_ref[0], vn_ref[0], vx_ref[0]], axis=0), head_dim)
    n_keys = kt.shape[1]

    qi = lax.broadcasted_iota(jnp.int32, (tq, n_keys), 0)
    kj = lax.broadcasted_iota(jnp.int32, (tq, n_keys), 1)
    key_pos = i * tq - WINDOW + kj
    ok = (kj >= band) | ((jnp.abs(kj - WINDOW - qi) <= WINDOW) & (key_pos >= 0) & (key_pos < seq))

    q = q_ref[0]
    outs = []
    for r in range(n_rep):
        s = jnp.dot(q[:, r * head_dim:(r + 1) * head_dim], kt, preferred_element_type=F32)
        s = jnp.where(ok, s, NEG_INF)
        sink = sink_ref[kv_head * n_rep + r] * LOG2E
        m = jnp.maximum(jnp.max(s, axis=-1, keepdims=True), sink)
        p = jnp.exp2(s - m)
        pv = jnp.dot(p.astype(BF16), v1, preferred_element_type=F32)
        denom = pv[:, head_dim:head_dim + 1] + jnp.exp2(sink - m)
        outs.append(pv[:, :head_dim] * (1.0 / denom))
    o_ref[...] = jnp.concatenate(outs, axis=1).astype(o_ref.dtype)


def _window_attention(q, kt, v, kt_ctx, v_ctx, sink, bsz, head_dim, tq_target=512):
    n_kv, nq, group = q.shape
    n_rep = group // head_dim
    seq = nq // bsz
    n_ctx = v_ctx.shape[1] // bsz
    tq = _row_tile(seq, tq_target)
    qt = seq // tq
    ratio = tq // WINDOW
    halo_blocks = seq // WINDOW
    prev_blk = lambda b, i: b * halo_blocks + jnp.maximum(i * ratio - 1, 0)
    next_blk = lambda b, i: b * halo_blocks + jnp.minimum((i + 1) * ratio, halo_blocks - 1)
    q_map = lambda b, g, i: (g, b * qt + i, 0)
    k_specs = [pl.BlockSpec((1, head_dim, WINDOW), lambda b, g, i: (g, 0, prev_blk(b, i))),
               pl.BlockSpec((1, head_dim, tq), lambda b, g, i: (g, 0, b * qt + i)),
               pl.BlockSpec((1, head_dim, WINDOW), lambda b, g, i: (g, 0, next_blk(b, i))),
               pl.BlockSpec((1, head_dim, n_ctx), lambda b, g, i: (g, 0, b))]
    v_specs = [pl.BlockSpec((1, WINDOW, head_dim), lambda b, g, i: (g, prev_blk(b, i), 0)),
               pl.BlockSpec((1, tq, head_dim), q_map),
               pl.BlockSpec((1, WINDOW, head_dim), lambda b, g, i: (g, next_blk(b, i), 0)),
               pl.BlockSpec((1, n_ctx, head_dim), lambda b, g, i: (g, b, 0))]
    kernel = functools.partial(_window_kernel, n_rep=n_rep, head_dim=head_dim, seq=seq)
    return pl.pallas_call(
        kernel,
        grid=(bsz, n_kv, qt),
        in_specs=[pl.BlockSpec(memory_space=pltpu.SMEM), pl.BlockSpec((1, tq, group), q_map)] + k_specs + v_specs,
        out_specs=pl.BlockSpec((tq, group), lambda b, g, i: (b * qt + i, g)),
        out_shape=jax.ShapeDtypeStruct((nq, n_kv * group), BF16),
        compiler_params=_params("parallel", "parallel", "parallel"),
        name="window_attention",
    )(sink, q, kt, kt, kt, kt_ctx, v, v, v, v_ctx)


def _router_combine(logits_t, bias_ref):
    aff = [_sigmoid(logits_t[e:e + 1, :]) for e in range(N_EXPERTS)]
    sel = [aff[e] + bias_ref[e] for e in range(N_EXPERTS)]
    best_score, best_group = None, None
    for g in range(N_EXPERT_GROUPS):
        a, b, c, d = sel[g * EXPERTS_PER_GROUP:(g + 1) * EXPERTS_PER_GROUP]
        hi1, lo1 = jnp.maximum(a, b), jnp.minimum(a, b)
        hi2, lo2 = jnp.maximum(c, d), jnp.minimum(c, d)
        score = jnp.maximum(hi1, hi2) + jnp.maximum(jnp.minimum(hi1, hi2), jnp.maximum(lo1, lo2))
        if g == 0:
            best_score, best_group = score, jnp.zeros(score.shape, jnp.int32)
        else:
            better = score > best_score
            best_score = jnp.where(better, score, best_score)
            best_group = jnp.where(better, g, best_group)
    gates = []
    for e in range(N_EXPERTS):
        g = e // EXPERTS_PER_GROUP
        beaten = jnp.zeros(best_group.shape, jnp.int32)
        for o in range(g * EXPERTS_PER_GROUP, (g + 1) * EXPERTS_PER_GROUP):
            if o == e:
                continue
            wins = (sel[o] > sel[e]) | ((sel[o] == sel[e]) & (o < e))
            beaten = beaten + wins.astype(jnp.int32)
        chosen = (best_group == g) & (beaten < TOP_K)
        gates.append(jnp.where(chosen, aff[e], 0.0))
    total = functools.reduce(jnp.add, gates)
    inv = 1.0 / total
    return jnp.concatenate([gt * inv for gt in gates], axis=0), best_group


def _post_mixer_kernel(*refs, alpha, glu, route):
    it = iter(refs)
    rb_ref = next(it) if route else None
    a_ref, x_ref = next(it), next(it)
    if glu:
        sc1_ref, sh1_ref, dsk_ref, wv_ref, wg_ref = next(it), next(it), next(it), next(it), next(it)
    else:
        wo_ref = next(it)
    g1_ref, lng_ref, lnb_ref = next(it), next(it), next(it)
    if route:
        sc2_ref, sh2_ref, rw_ref = next(it), next(it), next(it)
    x1_ref = next(it)
    if route:
        h2_ref, comb_ref, grp_ref = next(it), next(it), next(it)

    x = x_ref[...]
    if glu:
        h = x * (1.0 + sc1_ref[0]) + sh1_ref[0]
        act = _gelu_tanh(dsk_ref[...] * h + a_ref[...]).astype(BF16)
        y = (jnp.dot(act, wv_ref[...], preferred_element_type=F32)
             * _sigmoid(jnp.dot(act, wg_ref[...], preferred_element_type=F32)))
    else:
        y = jnp.dot(a_ref[...], wo_ref[...], preferred_element_type=F32)
    x1 = _layer_norm_rows(alpha * x + g1_ref[0] * y, lng_ref[...], lnb_ref[...])
    x1_ref[...] = x1
    if route:
        h2 = x1 * (1.0 + sc2_ref[0]) + sh2_ref[0]
        h2_ref[...] = h2.astype(BF16)
        logits_t = lax.dot_general(rw_ref[...], h2, (((1,), (1,)), ((), ())),
                                   preferred_element_type=F32, precision=HIGHEST)
        comb_ref[...], grp_ref[...] = _router_combine(logits_t, rb_ref)


def _post_mixer(a, x, g1, ln_g, ln_b, alpha, *, w_o=None, glu=None, route=None, tm_target=512):
    n, d = x.shape
    bsz = g1.shape[0]
    per_batch = n // bsz
    tm = _row_tile(per_batch, tm_target)
    tiles_per_batch = per_batch // tm
    row = lambda i: (i, 0)
    batch_vec = lambda i: (i // tiles_per_batch, 0, 0)
    const2 = lambda i: (0, 0)
    vec = pl.BlockSpec((1, 1, d), batch_vec)
    in_specs, args = [], []
    if route is not None:
        in_specs.append(pl.BlockSpec(memory_space=pltpu.SMEM))
        args.append(route[3])
    in_specs += [pl.BlockSpec((tm, d), row), pl.BlockSpec((tm, d), row)]
    args += [a, x]
    if glu is not None:
        sc1, sh1, d_skip, w_val, w_gate = glu
        in_specs += [vec, vec, pl.BlockSpec((1, d), const2),
                     pl.BlockSpec((d, d), const2), pl.BlockSpec((d, d), const2)]
        args += [sc1, sh1, d_skip.reshape(1, d), w_val, w_gate]
    else:
        in_specs.append(pl.BlockSpec((d, d), const2))
        args.append(w_o)
    in_specs += [vec, pl.BlockSpec((1, d), const2), pl.BlockSpec((1, d), const2)]
    args += [g1, ln_g.reshape(1, d), ln_b.reshape(1, d)]
    out_specs = [pl.BlockSpec((tm, d), row)]
    out_shape = [jax.ShapeDtypeStruct((n, d), F32)]
    if route is not None:
        in_specs += [vec, vec, pl.BlockSpec((N_EXPERTS, d), const2)]
        args += [route[0], route[1], route[2]]
        out_specs += [pl.BlockSpec((tm, d), row), pl.BlockSpec((N_EXPERTS, tm), lambda i: (0, i)),
                      pl.BlockSpec((1, tm), lambda i: (0, i))]
        out_shape += [jax.ShapeDtypeStruct((n, d), BF16), jax.ShapeDtypeStruct((N_EXPERTS, n), F32),
                      jax.ShapeDtypeStruct((1, n), jnp.int32)]
    kernel = functools.partial(_post_mixer_kernel, alpha=alpha, glu=glu is not None, route=route is not None)
    return pl.pallas_call(
        kernel,
        grid=(n // tm,),
        in_specs=in_specs,
        out_specs=out_specs,
        out_shape=out_shape,
        compiler_params=_params("parallel"),
        name="post_mixer",
    )(*args)


MOE_BLOCK_ROWS = 256


def _moe_rank_kernel(grp_ref, key_ref, cnt_ref):
    grp = grp_ref[...]
    tm = grp.shape[1]
    member = [grp == g for g in range(N_EXPERT_GROUPS)]
    onehot = jnp.concatenate([m.astype(F32) for m in member]
                             + [jnp.zeros((8 - N_EXPERT_GROUPS, tm), F32)], axis=0)
    earlier = (lax.broadcasted_iota(jnp.int32, (tm, tm), 0)
               < lax.broadcasted_iota(jnp.int32, (tm, tm), 1)).astype(BF16)
    prefix = jnp.dot(onehot.astype(BF16), earlier, preferred_element_type=F32)
    rank = functools.reduce(jnp.add, [jnp.where(member[g], prefix[g:g + 1], 0.0)
                                      for g in range(N_EXPERT_GROUPS)])
    key_ref[...] = (grp * (2 * tm)).astype(F32) + rank
    cnt_ref[0] = jnp.broadcast_to(jnp.sum(onehot, axis=1, keepdims=True), (8, LANES))


def _moe_kernel(cnt_ref, h_ref, keyr_ref, keyc_ref, cg_ref, w1_ref, w3_ref, w2_ref, x_ref, g2_ref, lng_ref, lnb_ref,
                out_ref, acc_sc, *, alpha):
    i, g = pl.program_id(0), pl.program_id(1)
    tm = h_ref.shape[0]
    full, half = MOE_BLOCK_ROWS, MOE_BLOCK_ROWS // 2

    @pl.when(g == 0)
    def _():
        acc_sc[...] = jnp.zeros(acc_sc.shape, F32)

    count = cnt_ref[i * N_EXPERT_GROUPS + g]
    n_full = count // full
    tail = count - n_full * full
    key_row = keyr_ref[...]
    key_col = keyc_ref[...]
    h = h_ref[...]
    first_key = (g * (2 * tm)).astype(F32)

    def run_block(blk, rb):
        base = first_key + (blk * full).astype(F32)
        block_row = lax.broadcasted_iota(jnp.int32, (rb, tm), 0).astype(F32)
        block_col = lax.broadcasted_iota(jnp.int32, (tm, rb), 1).astype(F32)
        onehot = (key_row - base == block_row).astype(F32)
        scatter = (key_col - base == block_col).astype(BF16)
        xs = jnp.dot(onehot.astype(BF16), h, preferred_element_type=F32).astype(BF16)
        z = jnp.zeros((rb, h.shape[1]), F32)
        for e in range(EXPERTS_PER_GROUP):
            weight = jnp.sum(onehot * cg_ref[0, e:e + 1, :], axis=1, keepdims=True)
            a1 = jnp.dot(xs, w1_ref[e], preferred_element_type=F32)
            a3 = jnp.dot(xs, w3_ref[e], preferred_element_type=F32)
            act = (a1 * _sigmoid(a1) * a3 * weight).astype(BF16)
            z = z + jnp.dot(act, w2_ref[e], preferred_element_type=F32)
        acc_sc[...] += jnp.dot(scatter, z.astype(BF16), preferred_element_type=F32)

    def full_block(blk, carry):
        run_block(blk, full)
        return carry

    lax.fori_loop(0, n_full, full_block, 0)

    @pl.when(tail > half)
    def _():
        run_block(n_full, full)

    @pl.when((tail > 0) & (tail <= half))
    def _():
        run_block(n_full, half)

    @pl.when(g == pl.num_programs(1) - 1)
    def _():
        out_ref[...] = _layer_norm_rows(alpha * x_ref[...] + g2_ref[0] * acc_sc[...], lng_ref[...], lnb_ref[...])


def _moe(h2, comb_t, grp, w1, w3, w2, x1, g2, ln_g, ln_b, alpha, tm_target=1024):
    n, d = x1.shape
    bsz = g2.shape[0]
    per_batch = n // bsz
    tm = _row_tile(per_batch, tm_target)
    tiles_per_batch = per_batch // tm
    n_tiles = n // tm
    n_groups = N_EXPERT_GROUPS
    d_expert = w1.shape[2]
    expert_blk = lambda i, g: (g, 0, 0)

    key, cnt = pl.pallas_call(
        _moe_rank_kernel,
        grid=(n_tiles,),
        in_specs=[pl.BlockSpec((1, tm), lambda i: (0, i))],
        out_specs=[pl.BlockSpec((1, tm), lambda i: (0, i)),
                   pl.BlockSpec((1, 8, LANES), lambda i: (i, 0, 0))],
        out_shape=[jax.ShapeDtypeStruct((1, n), F32), jax.ShapeDtypeStruct((n_tiles, 8, LANES), F32)],
        compiler_params=_params("parallel"),
        name="moe_rank",
    )(grp)
    counts = cnt[:, :n_groups, 0].astype(jnp.int32).reshape(-1)
    comb_g = jnp.pad(comb_t.reshape(n_groups, EXPERTS_PER_GROUP, n), ((0, 0), (0, 8 - EXPERTS_PER_GROUP), (0, 0)))

    row = lambda i, g: (i, 0)
    const2 = lambda i, g: (0, 0)
    once = pl.Buffered(1)
    kernel = functools.partial(_moe_kernel, alpha=alpha)
    return pl.pallas_call(
        kernel,
        grid=(n_tiles, n_groups),
        in_specs=[pl.BlockSpec(memory_space=pltpu.SMEM),
                  pl.BlockSpec((tm, d), row),
                  pl.BlockSpec((1, tm), lambda i, g: (0, i)),
                  pl.BlockSpec((tm, 1), row),
                  pl.BlockSpec((1, 8, tm), lambda i, g: (g, 0, i)),
                  pl.BlockSpec((EXPERTS_PER_GROUP, d, d_expert), expert_blk),
                  pl.BlockSpec((EXPERTS_PER_GROUP, d, d_expert), expert_blk),
                  pl.BlockSpec((EXPERTS_PER_GROUP, d_expert, d), expert_blk),
                  pl.BlockSpec((tm, d), row, pipeline_mode=once),
                  pl.BlockSpec((1, 1, d), lambda i, g: (i // tiles_per_batch, 0, 0)),
                  pl.BlockSpec((1, d), const2),
                  pl.BlockSpec((1, d), const2)],
        out_specs=pl.BlockSpec((tm, d), row, pipeline_mode=once),
        out_shape=jax.ShapeDtypeStruct((n, d), F32),
        scratch_shapes=[pltpu.VMEM((tm, d), F32)],
        compiler_params=_params("parallel", "arbitrary"),
        name="moe",
    )(counts, h2, key, key.reshape(n, 1), comb_g, w1, w3, w2, x1, g2, ln_g.reshape(1, d), ln_b.reshape(1, d))


def _modulated_rows(lat_ref, ctx_ref, scale_ref, shift_ref, g, row, class_starts, d_model):
    slabs = []
    for c in range(S5_GROUP_CH):
        channel = g * S5_GROUP_CH + c
        u = jnp.concatenate([lat_ref[c], ctx_ref[c]], axis=0)
        scale = scale_ref[channel]
        shift = shift_ref[channel]
        for k, start in enumerate(class_starts[1:], start=1):
            later = row >= start
            scale = jnp.where(later, scale_ref[k * d_model + channel], scale)
            shift = jnp.where(later, shift_ref[k * d_model + channel], shift)
        slabs.append((u * scale + shift).astype(BF16))
    return jnp.concatenate(slabs, axis=1)


def _s5_inject_kernel(scale_ref, shift_ref, lat_ref, ctx_ref, pin_ref, out_ref, *, class_starts, d_model):
    rows = out_ref.shape[0]
    row = lax.broadcasted_iota(jnp.int32, (rows, SCAN_CHUNK), 0)
    u = _modulated_rows(lat_ref, ctx_ref, scale_ref, shift_ref, pl.program_id(0), row, class_starts, d_model)
    out_ref[...] = jnp.dot(u, pin_ref[0], preferred_element_type=F32)


def _s5_carry_kernel(sin_ref, a_ref, b_ref, h_ref, *, bsz, ctx_chunks, lat_chunks):
    half = 2 * S5_STATE
    ctx0 = bsz * lat_chunks

    def advance(h, row, lo):
        h_ref[row, :, lo:lo + half] = h
        a = a_ref[:, lo:lo + half]
        b = b_ref[:, lo:lo + half]
        return a * h + b * pltpu.roll(h, S5_STATE, 1) + sin_ref[row, :, lo:lo + half]

    zero = jnp.zeros((sin_ref.shape[1], half), F32)
    states = []
    for bi in range(bsz):
        hf, hb = zero, zero
        for n in range(ctx_chunks):
            hf = advance(hf, ctx0 + bi * ctx_chunks + n, 0)
            hb = advance(hb, ctx0 + bi * ctx_chunks + ctx_chunks - 1 - n, half)
        states += [hf, hb]

    def body(n, carry):
        out = []
        for bi in range(bsz):
            base = bi * lat_chunks
            out.append(advance(carry[2 * bi], base + n, 0))
            out.append(advance(carry[2 * bi + 1], base + lat_chunks - 1 - n, half))
        return tuple(out)

    lax.fori_loop(0, lat_chunks, body, tuple(states))


def _s5_output_kernel(scale_ref, shift_ref, lat_ref, ctx_ref, kc_ref, h_ref, pout_ref, out_lat_ref, out_ctx_ref,
                      m_sc, *, class_starts, d_model):
    rows = h_ref.shape[0]
    lat_rows = lat_ref.shape[1]
    t = SCAN_CHUNK

    def build(cp, carry):
        for c in range(S5_GROUP_CH):
            lags = kc_ref[0, cp, c:c + 1, :]
            shifted = pltpu.roll(jnp.broadcast_to(lags, (t, 2 * t)), 0, 1, stride=1, stride_axis=0)
            m_sc[pl.ds(pl.multiple_of(cp * t, t), t), c * t:(c + 1) * t] = shifted[:, t:].astype(BF16)
        return carry

    lax.fori_loop(0, S5_GROUP_CH, build, 0)

    row = lax.broadcasted_iota(jnp.int32, (rows, t), 0)
    u = _modulated_rows(lat_ref, ctx_ref, scale_ref, shift_ref, pl.program_id(0), row, class_starts, d_model)
    acc = jnp.dot(u, m_sc[...], preferred_element_type=F32)
    h = h_ref[...].astype(BF16)
    for c in range(S5_GROUP_CH):
        y = acc[:, c * t:(c + 1) * t] + jnp.dot(h, pout_ref[0, c], preferred_element_type=F32)
        out_lat_ref[c] = y[:lat_rows]
        out_ctx_ref[c] = y[lat_rows:]


def _s5_tables(a_re, a_im, log_dt, b_re, b_im, c_re, c_im):
    t = SCAN_CHUNK
    a_re, a_im = a_re.astype(F32), a_im.astype(F32)
    dt = jnp.exp(log_dt.astype(F32))[..., None]
    steps = jnp.arange(t + 1, dtype=F32)[None, None, :, None]
    mag = jnp.exp(steps * (a_re * dt)[:, :, None])
    ang = steps * (a_im * dt)[:, :, None]
    pw_re, pw_im = mag * jnp.cos(ang), mag * jnp.sin(ang)
    lam_re, lam_im = pw_re[:, :, 1], pw_im[:, :, 1]
    inv_den = 1.0 / (a_re * a_re + a_im * a_im)
    n_re = lam_re - 1.0
    f_re = (n_re * a_re + lam_im * a_im) * inv_den
    f_im = (lam_im * a_re - n_re * a_im) * inv_den
    bb_re = f_re[..., None] * b_re - f_im[..., None] * b_im
    bb_im = f_re[..., None] * b_im + f_im[..., None] * b_re

    pt_re = pw_re[:, :, :t].transpose(0, 1, 3, 2)[..., None]
    pt_im = pw_im[:, :, :t].transpose(0, 1, 3, 2)[..., None]
    lb_re = pt_re * bb_re[:, :, :, None] - pt_im * bb_im[:, :, :, None]
    lb_im = pt_re * bb_im[:, :, :, None] + pt_im * bb_re[:, :, :, None]
    resp = (jnp.einsum('dgcp,dgptk->dgkct', c_re, lb_re, precision=HIGHEST)
            - jnp.einsum('dgcp,dgptk->dgkct', c_im, lb_im, precision=HIGHEST))
    fwd, bwd = resp[0], resp[1]
    lags = jnp.concatenate([jnp.zeros_like(fwd[..., :1]), bwd[..., :0:-1],
                            fwd[..., :1] + bwd[..., :1], fwd[..., 1:]], axis=-1)

    def lanes(f_a, f_b, b_a, b_b):
        return jnp.concatenate([f_a, f_b, b_a, b_b], axis=-1)

    in_re = lanes(pw_re[0, :, t - 1::-1], pw_re[0, :, t - 1::-1], pw_re[1, :, :t], pw_re[1, :, :t])
    in_im = lanes(pw_im[0, :, t - 1::-1], pw_im[0, :, t - 1::-1], pw_im[1, :, :t], pw_im[1, :, :t])
    bt_re, bt_im = bb_re.transpose(0, 1, 3, 2), bb_im.transpose(0, 1, 3, 2)
    with_re = lanes(bt_re[0], bt_im[0], bt_re[1], bt_im[1])
    with_im = lanes(-bt_im[0], bt_re[0], -bt_im[1], bt_re[1])
    p_in = (in_re[:, None] * with_re[:, :, None] + in_im[:, None] * with_im[:, :, None]).astype(BF16)
    p_in = p_in.reshape(p_in.shape[0], -1, 4 * S5_STATE)

    def rows(f_a, f_b, b_a, b_b):
        return jnp.concatenate([f_a, f_b, b_a, b_b], axis=-2)

    fr, fi = pw_re[0, :, 1:].transpose(0, 2, 1), pw_im[0, :, 1:].transpose(0, 2, 1)
    br, bi = pw_re[1, :, :0:-1].transpose(0, 2, 1), pw_im[1, :, :0:-1].transpose(0, 2, 1)
    out_x, out_y = rows(fr, fi, br, bi), rows(fi, fr, bi, br)
    cr, ci = c_re[:, :, :, :, None], c_im[:, :, :, :, None]
    with_x = rows(cr[0], -cr[0], cr[1], -cr[1])
    with_y = rows(-ci[0], -ci[0], -ci[1], -ci[1])
    p_out = (with_x * out_x[:, None] + with_y * out_y[:, None]).astype(BF16)

    carry_a = lanes(pw_re[0, :, t], pw_re[0, :, t], pw_re[1, :, t], pw_re[1, :, t])
    carry_b = lanes(-pw_im[0, :, t], pw_im[0, :, t], -pw_im[1, :, t], pw_im[1, :, t])
    return lags, p_in, p_out, carry_a, carry_b


def _to_chunk_major(rows2d):
    n, d = rows2d.shape
    return rows2d.reshape(n // SCAN_CHUNK, SCAN_CHUNK, d).transpose(2, 0, 1)


def _from_chunk_major(xt):
    d, chunks, t = xt.shape
    return xt.transpose(1, 2, 0).reshape(chunks * t, d)


def _s5_scan(x_lat, x_ctx, scale, shift, tables, bsz):
    lags, p_in, p_out, carry_a, carry_b = tables
    d = x_lat.shape[1]
    groups = d // S5_GROUP_CH
    ctx_chunks = x_ctx.shape[0] // bsz // SCAN_CHUNK
    lat_chunks = x_lat.shape[0] // bsz // SCAN_CHUNK
    lat_rows, ctx_rows = bsz * lat_chunks, bsz * ctx_chunks
    rows = lat_rows + ctx_rows
    xt_lat, xt_ctx = _to_chunk_major(x_lat), _to_chunk_major(x_ctx)
    class_starts = tuple(b * lat_chunks for b in range(bsz)) + (lat_rows,)
    scale_flat, shift_flat = scale.reshape(-1), shift.reshape(-1)
    state_w = 4 * S5_STATE
    smem = pl.BlockSpec(memory_space=pltpu.SMEM)
    slab_lat = pl.BlockSpec((S5_GROUP_CH, lat_rows, SCAN_CHUNK), lambda g: (g, 0, 0))
    slab_ctx = pl.BlockSpec((S5_GROUP_CH, ctx_rows, SCAN_CHUNK), lambda g: (g, 0, 0))

    inject = pl.pallas_call(
        functools.partial(_s5_inject_kernel, class_starts=class_starts, d_model=d),
        grid=(groups,),
        in_specs=[smem, smem, slab_lat, slab_ctx,
                  pl.BlockSpec((1, S5_GROUP_CH * SCAN_CHUNK, state_w), lambda g: (g, 0, 0))],
        out_specs=pl.BlockSpec((rows, state_w), lambda g: (0, g)),
        out_shape=jax.ShapeDtypeStruct((rows, groups * state_w), F32),
        compiler_params=_params("parallel"),
        name="s5_inject",
    )(scale_flat, shift_flat, xt_lat, xt_ctx, p_in)

    gb = 8
    entering = pl.pallas_call(
        functools.partial(_s5_carry_kernel, bsz=bsz, ctx_chunks=ctx_chunks, lat_chunks=lat_chunks),
        grid=(groups // gb,),
        in_specs=[pl.BlockSpec((rows, gb, state_w), lambda g: (0, g, 0)),
                  pl.BlockSpec((gb, state_w), lambda g: (g, 0)),
                  pl.BlockSpec((gb, state_w), lambda g: (g, 0))],
        out_specs=pl.BlockSpec((rows, gb, state_w), lambda g: (0, g, 0)),
        out_shape=jax.ShapeDtypeStruct((rows, groups, state_w), F32),
        compiler_params=_params("parallel"),
        name="s5_carry",
    )(inject.reshape(rows, groups, state_w), carry_a, carry_b)

    yt_lat, yt_ctx = pl.pallas_call(
        functools.partial(_s5_output_kernel, class_starts=class_starts, d_model=d),
        grid=(groups,),
        in_specs=[smem, smem, slab_lat, slab_ctx,
                  pl.BlockSpec((1, S5_GROUP_CH, S5_GROUP_CH, 2 * SCAN_CHUNK), lambda g: (g, 0, 0, 0)),
                  pl.BlockSpec((rows, state_w), lambda g: (0, g)),
                  pl.BlockSpec((1, S5_GROUP_CH, state_w, SCAN_CHUNK), lambda g: (g, 0, 0, 0))],
        out_specs=[slab_lat, slab_ctx],
        out_shape=[jax.ShapeDtypeStruct((d, lat_rows, SCAN_CHUNK), F32),
                   jax.ShapeDtypeStruct((d, ctx_rows, SCAN_CHUNK), F32)],
        scratch_shapes=[pltpu.VMEM((S5_GROUP_CH * SCAN_CHUNK, S5_GROUP_CH * SCAN_CHUNK), BF16)],
        compiler_params=_params("parallel"),
        name="s5_output",
    )(scale_flat, shift_flat, xt_lat, xt_ctx, lags, entering.reshape(rows, groups * state_w), p_out)

    return _from_chunk_major(yt_lat), _from_chunk_major(yt_ctx)


def kernel(x, c, ctx, c_ctx, mod_w, mod_b, ln_g, ln_b, router_w, router_b, moe_w1, moe_w3, moe_w2, s5_a_re, s5_a_im, s5_log_dt, s5_b_re, s5_b_im, s5_c_re, s5_c_im, s5_d, s5_w_gate, s5_w_val, swa_w_qkv, swa_w_o, swa_sink, gqa_w_qkv, gqa_w_o, gqa_q_norm, gqa_k_norm):
    bsz, seq, d = x.shape
    n_ctx = ctx.shape[1]
    depth = mod_w.shape[0]
    alpha = (2.0 * depth) ** 0.25
    heads_b, heads_c = d // HEAD_DIM_B, d // HEAD_DIM_C

    cond = jnp.concatenate([c_ctx[None, :], c], axis=0)
    mods = _modulation(cond, mod_w, mod_b).reshape(depth, bsz + 1, 6, d)
    xl = x.reshape(bsz * seq, d)
    xc = ctx.reshape(bsz * n_ctx, d)
    router_w_t = router_w.T
    rope_b = _rope_tables(seq, HEAD_DIM_B)
    rope_c = _rope_tables(seq, HEAD_DIM_C)

    for i in range(depth):
        kind, j = i % N_MIXERS, i // N_MIXERS
        ctx_out = i < depth - 1
        lat_mod = [mods[i, 1:, k][:, None, :] for k in range(6)]
        ctx_mod = [jnp.broadcast_to(mods[i, 0, k][None, None, :], (bsz, 1, d)) for k in range(6)]
        ln1 = (ln_g[i, 0], ln_b[i, 0])
        ln2 = (ln_g[i, 1], ln_b[i, 1])
        buffers = [(xl, lat_mod, True)] + ([(xc, ctx_mod, False)] if ctx_out else [])

        if kind == 0:
            tables = _s5_tables(s5_a_re[j], s5_a_im[j], s5_log_dt[j], s5_b_re[j], s5_b_im[j],
                                s5_c_re[j], s5_c_im[j])
            ctx_last = jnp.roll(mods[i], -1, axis=0)
            y_lat, y_ctx = _s5_scan(xl, xc, 1.0 + ctx_last[:, 1], ctx_last[:, 0], tables, bsz)
            glu_w = (s5_d[j], s5_w_val[j].astype(BF16), s5_w_gate[j].astype(BF16))
            mixed = {True: y_lat, False: y_ctx}
        else:
            if kind == 1:
                w_qkv, w_o = swa_w_qkv[j].astype(BF16), swa_w_o[j].astype(BF16)
                heads, kv, dh, norms, rope = heads_b, KV_HEADS_B, HEAD_DIM_B, None, rope_b
            else:
                w_qkv, w_o = gqa_w_qkv[j].astype(BF16), gqa_w_o[j].astype(BF16)
                heads, kv, dh, norms, rope = heads_c, KV_HEADS_C, HEAD_DIM_C, (gqa_q_norm[j], gqa_k_norm[j]), rope_c
            ql, kl, vl = _qkv_project(xl, lat_mod[1], lat_mod[0], w_qkv, heads, kv, dh, norms=norms, rope=rope)
            qc, kc, vc = _qkv_project(xc, ctx_mod[1], ctx_mod[0], w_qkv, heads, kv, dh, norms=norms, rope=None)
            if kind == 1:
                sink = swa_sink[j].astype(F32)
                o_lat = _window_attention(ql, kl, vl, kc, vc, sink, bsz, dh)
                o_ctx = _flash_attention(qc, kc, vc, bsz, dh, sink=sink) if ctx_out else None
            else:
                keys = jnp.concatenate([kl.reshape(kv, dh, bsz, seq), kc.reshape(kv, dh, bsz, n_ctx)], axis=3)
                vals = jnp.concatenate([vl.reshape(kv, bsz, seq, dh), vc.reshape(kv, bsz, n_ctx, dh)], axis=2)
                o_lat = _flash_attention(ql, keys.reshape(kv, dh, -1), vals.reshape(kv, -1, dh), bsz, dh)
                o_ctx = _flash_attention(qc, kc, vc, bsz, dh) if ctx_out else None
            mixed = {True: o_lat, False: o_ctx}

        w1, w3, w2 = moe_w1[i].astype(BF16), moe_w3[i].astype(BF16), moe_w2[i].astype(BF16)
        new = []
        for rows, mod, is_lat in buffers:
            sh1, sc1, g1, sh2, sc2, g2 = mod
            route = (sc2, sh2, router_w_t, router_b.astype(F32))
            if kind == 0:
                x1, h2, comb_t, grp = _post_mixer(mixed[is_lat], rows, g1, *ln1, alpha,
                                                  glu=(sc1, sh1) + glu_w, route=route)
            else:
                x1, h2, comb_t, grp = _post_mixer(mixed[is_lat], rows, g1, *ln1, alpha, w_o=w_o, route=route)
            new.append(_moe(h2, comb_t, grp, w1, w3, w2, x1, g2, *ln2, alpha))
        xl = new[0]
        if ctx_out:
            xc = new[1]
    return xl.reshape(bsz, seq, d)
```

```python
import functools
import math

import jax
import jax.numpy as jnp
from jax import lax
from jax.experimental import pallas as pl
from jax.experimental.pallas import tpu as pltpu

GRID_W = 64
N_MIXERS = 3
LN_EPS = 1e-6
RMS_EPS = 1e-6
NEG_INF = -1e30
ROPE_THETA = 10000.0
S5_GROUP_CH = 16
S5_STATE = 64
SCAN_CHUNK = 128
WINDOW = 128
HEAD_DIM_B = 64
KV_HEADS_B = 2
HEAD_DIM_C = 128
KV_HEADS_C = 2
N_EXPERTS = 16
N_EXPERT_GROUPS = 4
EXPERTS_PER_GROUP = N_EXPERTS // N_EXPERT_GROUPS
TOP_K = 2

LANES = 128
LOG2E = math.log2(math.e)
VMEM_LIMIT_BYTES = 56 * 1024 * 1024

F32 = jnp.float32
BF16 = jnp.bfloat16
HIGHEST = lax.Precision.HIGHEST


def _params(*semantics):
    return pltpu.CompilerParams(dimension_semantics=semantics,
                                vmem_limit_bytes=VMEM_LIMIT_BYTES)


def _row_tile(n_rows, target):
    t = min(target, n_rows)
    while n_rows % t:
        t //= 2
    return t


def _layer_norm_rows(v, g, b):
    vc = v - jnp.mean(v, axis=-1, keepdims=True)
    var = jnp.mean(vc * vc, axis=-1, keepdims=True)
    return vc * lax.rsqrt(var + LN_EPS) * g + b


def _sigmoid(v):
    return 1.0 / (1.0 + jnp.exp(-v))


def _gelu_tanh(v):
    return 0.5 * v * (1.0 + jnp.tanh(math.sqrt(2.0 / math.pi) * (v + 0.044715 * (v * v * v))))


def _modulation_kernel(cond_ref, w_ref, b_ref, out_ref):
    cond = cond_ref[...]
    act = cond * _sigmoid(cond)
    out_ref[0] = jnp.dot(act, w_ref[0], preferred_element_type=F32, precision=HIGHEST) + b_ref[0]


def _modulation(cond, mod_w, mod_b):
    depth, d, n_out = mod_w.shape
    rows = cond.shape[0]
    tn = 1536
    return pl.pallas_call(
        _modulation_kernel,
        grid=(depth, n_out // tn),
        in_specs=[pl.BlockSpec((rows, d), lambda i, j: (0, 0)),
                  pl.BlockSpec((1, d, tn), lambda i, j: (i, 0, j)),
                  pl.BlockSpec((1, 1, tn), lambda i, j: (i, 0, j))],
        out_specs=pl.BlockSpec((1, rows, tn), lambda i, j: (i, 0, j)),
        out_shape=jax.ShapeDtypeStruct((depth, rows, n_out), F32),
        compiler_params=_params("parallel", "parallel"),
        name="modulation",
    )(cond, mod_w, mod_b.reshape(depth, 1, n_out))


def _swap_rotary_pairs(v, half):
    width = v.shape[-1]
    lane = lax.broadcasted_iota(jnp.int32, v.shape, v.ndim - 1)
    first = (lane // half) % 2 == 0
    return jnp.where(first, pltpu.roll(v, width - half, v.ndim - 1), pltpu.roll(v, half, v.ndim - 1))


def _qkv_kernel(*refs, n_heads, n_kv, head_dim, qk_norm, use_rope):
    it = iter(refs)
    x_ref, sc_ref, sh_ref, w_ref = next(it), next(it), next(it), next(it)
    if qk_norm:
        qn_ref, kn_ref = next(it), next(it)
    if use_rope:
        cos_ref, sin_ref = next(it), next(it)
    q_ref, kt_ref, v_ref = next(it), next(it), next(it)

    h = (x_ref[...] * (1.0 + sc_ref[0]) + sh_ref[0]).astype(BF16)
    proj = jnp.dot(h, w_ref[...], preferred_element_type=F32)
    q_width, kv_width = n_heads * head_dim, n_kv * head_dim
    q = proj[:, :q_width]
    k = proj[:, q_width:q_width + kv_width]
    v = proj[:, q_width + kv_width:]

    def rms_heads(t, gain, count):
        parts = []
        for hd in range(count):
            th = t[:, hd * head_dim:(hd + 1) * head_dim]
            ms = jnp.mean(th * th, axis=-1, keepdims=True)
            parts.append(th * lax.rsqrt(ms + RMS_EPS) * gain)
        return jnp.concatenate(parts, axis=-1)

    if qk_norm:
        q = rms_heads(q, qn_ref[...], n_heads)
        k = rms_heads(k, kn_ref[...], n_kv)

    if use_rope:
        cos, sin = cos_ref[...], sin_ref[...]

        def rope(t):
            reps = t.shape[-1] // LANES
            c = jnp.concatenate([cos] * reps, axis=-1) if reps > 1 else cos
            s = jnp.concatenate([sin] * reps, axis=-1) if reps > 1 else sin
            return t * c + _swap_rotary_pairs(t, head_dim // 4) * s

        q, k = rope(q), rope(k)

    q = q * (head_dim ** -0.5 * LOG2E)
    k_t = k.T
    group = q_width // n_kv
    for g in range(n_kv):
        q_ref[g] = q[:, g * group:(g + 1) * group].astype(BF16)
        kt_ref[g] = k_t[g * head_dim:(g + 1) * head_dim, :].astype(BF16)
        v_ref[g] = v[:, g * head_dim:(g + 1) * head_dim].astype(BF16)


def _qkv_project(x, sc, sh, w, n_heads, n_kv, head_dim, norms=None, rope=None, tm_target=512):
    n, d = x.shape
    bsz = sc.shape[0]
    per_batch = n // bsz
    tm = _row_tile(per_batch, tm_target)
    tiles_per_batch = per_batch // tm
    n_out = w.shape[1]
    group = n_heads * head_dim // n_kv
    row = lambda i: (i, 0)
    batch_vec = lambda i: (i // tiles_per_batch, 0, 0)
    const2 = lambda i: (0, 0)
    in_specs = [pl.BlockSpec((tm, d), row),
                pl.BlockSpec((1, 1, d), batch_vec),
                pl.BlockSpec((1, 1, d), batch_vec),
                pl.BlockSpec((d, n_out), const2)]
    args = [x, sc, sh, w]
    if norms is not None:
        in_specs += [pl.BlockSpec((1, head_dim), const2)] * 2
        args += [norms[0].reshape(1, head_dim), norms[1].reshape(1, head_dim)]
    if rope is not None:
        in_specs += [pl.BlockSpec((tm, LANES), lambda i: (i % tiles_per_batch, 0))] * 2
        args += [rope[0], rope[1]]
    kernel = functools.partial(_qkv_kernel, n_heads=n_heads, n_kv=n_kv, head_dim=head_dim,
                               qk_norm=norms is not None, use_rope=rope is not None)
    return pl.pallas_call(
        kernel,
        grid=(n // tm,),
        in_specs=in_specs,
        out_specs=[pl.BlockSpec((n_kv, tm, group), lambda i: (0, i, 0)),
                   pl.BlockSpec((n_kv, head_dim, tm), lambda i: (0, 0, i)),
                   pl.BlockSpec((n_kv, tm, head_dim), lambda i: (0, i, 0))],
        out_shape=[jax.ShapeDtypeStruct((n_kv, n, group), BF16),
                   jax.ShapeDtypeStruct((n_kv, head_dim, n), BF16),
                   jax.ShapeDtypeStruct((n_kv, n, head_dim), BF16)],
        compiler_params=_params("parallel"),
        name="qkv_project",
    )(*args)


def _rope_tables(seq, head_dim):
    quarter = head_dim // 4
    inv_freq = ROPE_THETA ** (-jnp.arange(quarter, dtype=F32) / quarter)
    t = jnp.arange(seq)
    rows = (t // GRID_W).astype(F32)
    cols = (t % GRID_W).astype(F32)
    ang_r = rows[:, None] * inv_freq
    ang_c = cols[:, None] * inv_freq
    cos = jnp.concatenate([jnp.cos(ang_r)] * 2 + [jnp.cos(ang_c)] * 2, axis=-1)
    sin = jnp.concatenate([-jnp.sin(ang_r), jnp.sin(ang_r), -jnp.sin(ang_c), jnp.sin(ang_c)], axis=-1)
    reps = LANES // head_dim
    return jnp.tile(cos, (1, reps)), jnp.tile(sin, (1, reps))


def _stack_heads(q, n_rep, head_dim):
    if n_rep == 1:
        return q
    return jnp.concatenate([q[:, r * head_dim:(r + 1) * head_dim] for r in range(n_rep)], axis=0)


def _unstack_heads(o, n_rep, rows):
    if n_rep == 1:
        return o
    return jnp.concatenate([o[r * rows:(r + 1) * rows] for r in range(n_rep)], axis=-1)


def _sink_column(sink_ref, g, n_rep, rows):
    return jnp.concatenate([jnp.full((rows, 1), sink_ref[g * n_rep + r] * LOG2E, F32) for r in range(n_rep)],
                           axis=0)


def _ones_width(head_dim):
    return LANES - head_dim % LANES


def _with_ones(v, head_dim):
    return jnp.concatenate([v, jnp.ones((v.shape[0], _ones_width(head_dim)), v.dtype)], axis=1)


def _lane_repeat(col_block, width):
    reps = width // col_block.shape[1]
    return col_block if reps == 1 else jnp.concatenate([col_block] * reps, axis=1)


def _flash_kernel(*refs, n_rep, head_dim, has_sink, n_split):
    if has_sink:
        sink_ref, q_ref, kt_ref, v_ref, o_ref, m_sc, acc_sc = refs
    else:
        q_ref, kt_ref, v_ref, o_ref, m_sc, acc_sc = refs
    kv_head, kj = pl.program_id(1), pl.program_id(3)
    tq = q_ref.shape[1]
    width = acc_sc.shape[1]

    @pl.when(kj == 0)
    def _():
        m_sc[...] = jnp.full(m_sc.shape, NEG_INF, F32)
        acc_sc[...] = jnp.zeros(acc_sc.shape, F32)

    m_all = m_sc[...]
    acc_all = acc_sc[...]
    q = _stack_heads(q_ref[0], n_rep, head_dim)
    kt = kt_ref[0]
    v1 = _with_ones(v_ref[0], head_dim)
    part = n_rep * tq // n_split
    m_out, acc_out = [], []
    for h in range(n_split):
        rs = slice(h * part, (h + 1) * part)
        s = jnp.dot(q[rs], kt, preferred_element_type=F32)
        m_prev = m_all[rs]
        m_new = jnp.maximum(m_prev, jnp.max(s, axis=-1, keepdims=True))
        p = jnp.exp2(s - _lane_repeat(m_new, s.shape[1]))
        alpha = jnp.exp2(m_prev - m_new)
        acc_out.append(_lane_repeat(alpha, width) * acc_all[rs]
                       + jnp.dot(p.astype(BF16), v1, preferred_element_type=F32))
        m_out.append(m_new)
    m_sc[...] = jnp.concatenate(m_out, axis=0)
    acc_sc[...] = jnp.concatenate(acc_out, axis=0)

    @pl.when(kj == pl.num_programs(3) - 1)
    def _():
        acc = acc_sc[...]
        num, l = acc[:, :head_dim], acc[:, head_dim:head_dim + 1]
        if has_sink:
            m = m_sc[:, :1]
            sink = _sink_column(sink_ref, kv_head, n_rep, tq)
            m_fin = jnp.maximum(m, sink)
            scale = jnp.exp2(m - m_fin)
            l = l * scale + jnp.exp2(sink - m_fin)
            num = num * scale
        o_ref[...] = _unstack_heads(num * (1.0 / l), n_rep, tq).astype(o_ref.dtype)


def _flash_attention(q, kt, v, bsz, head_dim, sink=None, tq_target=1024, tk_target=3328, n_split=32):
    n_kv, nq, group = q.shape
    n_rep = group // head_dim
    lq, lk = nq // bsz, v.shape[1] // bsz
    tq = _row_tile(lq, tq_target)
    tk = tk_target if lk % tk_target == 0 else _row_tile(lk, 256)
    qt, kt_tiles = lq // tq, lk // tk
    kernel = functools.partial(_flash_kernel, n_rep=n_rep, head_dim=head_dim, has_sink=sink is not None,
                               n_split=n_split)
    in_specs = [pl.BlockSpec((1, tq, group), lambda b, g, i, j: (g, b * qt + i, 0)),
                pl.BlockSpec((1, head_dim, tk), lambda b, g, i, j: (g, 0, b * kt_tiles + j)),
                pl.BlockSpec((1, tk, head_dim), lambda b, g, i, j: (g, b * kt_tiles + j, 0))]
    args = [q, kt, v]
    if sink is not None:
        in_specs = [pl.BlockSpec(memory_space=pltpu.SMEM)] + in_specs
        args = [sink] + args
    return pl.pallas_call(
        kernel,
        grid=(bsz, n_kv, qt, kt_tiles),
        in_specs=in_specs,
        out_specs=pl.BlockSpec((tq, group), lambda b, g, i, j: (b * qt + i, g)),
        out_shape=jax.ShapeDtypeStruct((nq, n_kv * group), BF16),
        scratch_shapes=[pltpu.VMEM((n_rep * tq, LANES), F32),
                        pltpu.VMEM((n_rep * tq, head_dim + _ones_width(head_dim)), F32)],
        compiler_params=_params("parallel", "parallel", "parallel", "arbitrary"),
        name="flash_attention",
    )(*args)


def _window_kernel(sink_ref, q_ref, kp_ref, kc_ref, kn_ref, kx_ref, vp_ref, vc_ref, vn_ref, vx_ref, o_ref,
                   *, n_rep, head_dim, seq):
    kv_head, i = pl.program_id(1), pl.program_id(2)
    tq = q_ref.shape[1]
    band = tq + 2 * WINDOW
    kt = jnp.concatenate([kp_ref[0], kc_ref[0], kn_ref[0], kx_ref[0]], axis=1)
    v1 = _with_ones(jnp.concatenate([vp_ref[0], vc_ref[0], vn_ref[0], vx_ref[0]], axis=0), head_dim)
    n_keys = kt.shape[1]

    qi = lax.broadcasted_iota(jnp.int32, (tq, n_keys), 0)
    kj = lax.broadcasted_iota(jnp.int32, (tq, n_keys), 1)
    key_pos = i * tq - WINDOW + kj
    ok = (kj >= band) | ((jnp.abs(kj - WINDOW - qi) <= WINDOW) & (key_pos >= 0) & (key_pos < seq))

    q = q_ref[0]
    outs = []
    for r in range(n_rep):
        s = jnp.dot(q[:, r * head_dim:(r + 1) * head_dim], kt, preferred_element_type=F32)
        s = jnp.where(ok, s, NEG_INF)
        sink = sink_ref[kv_head * n_rep + r] * LOG2E
        m = jnp.maximum(jnp.max(s, axis=-1, keepdims=True), sink)
        p = jnp.exp2(s - m)
        pv = jnp.dot(p.astype(BF16), v1, preferred_element_type=F32)
        denom = pv[:, head_dim:head_dim + 1] + jnp.exp2(sink - m)
        outs.append(pv[:, :head_dim] * (1.0 / denom))
    o_ref[...] = jnp.concatenate(outs, axis=1).astype(o_ref.dtype)


def _window_attention(q, kt, v, kt_ctx, v_ctx, sink, bsz, head_dim, tq_target=512):
    n_kv, nq, group = q.shape
    n_rep = group // head_dim
    seq = nq // bsz
    n_ctx = v_ctx.shape[1] // bsz
    tq = _row_tile(seq, tq_target)
    qt = seq // tq
    ratio = tq // WINDOW
    halo_blocks = seq // WINDOW
    prev_blk = lambda b, i: b * halo_blocks + jnp.maximum(i * ratio - 1, 0)
    next_blk = lambda b, i: b * halo_blocks + jnp.minimum((i + 1) * ratio, halo_blocks - 1)
    q_map = lambda b, g, i: (g, b * qt + i, 0)
    k_specs = [pl.BlockSpec((1, head_dim, WINDOW), lambda b, g, i: (g, 0, prev_blk(b, i))),
               pl.BlockSpec((1, head_dim, tq), lambda b, g, i: (g, 0, b * qt + i)),
               pl.BlockSpec((1, head_dim, WINDOW), lambda b, g, i: (g, 0, next_blk(b, i))),
               pl.BlockSpec((1, head_dim, n_ctx), lambda b, g, i: (g, 0, b))]
    v_specs = [pl.BlockSpec((1, WINDOW, head_dim), lambda b, g, i: (g, prev_blk(b, i), 0)),
               pl.BlockSpec((1, tq, head_dim), q_map),
               pl.BlockSpec((1, WINDOW, head_dim), lambda b, g, i: (g, next_blk(b, i), 0)),
               pl.BlockSpec((1, n_ctx, head_dim), lambda b, g, i: (g, b, 0))]
    kernel = functools.partial(_window_kernel, n_rep=n_rep, head_dim=head_dim, seq=seq)
    return pl.pallas_call(
        kernel,
        grid=(bsz, n_kv, qt),
        in_specs=[pl.BlockSpec(memory_space=pltpu.SMEM), pl.BlockSpec((1, tq, group), q_map)] + k_specs + v_specs,
        out_specs=pl.BlockSpec((tq, group), lambda b, g, i: (b * qt + i, g)),
        out_shape=jax.ShapeDtypeStruct((nq, n_kv * group), BF16),
        compiler_params=_params("parallel", "parallel", "parallel"),
        name="window_attention",
    )(sink, q, kt, kt, kt, kt_ctx, v, v, v, v_ctx)


def _router_combine(logits_t, bias_ref):
    aff = [_sigmoid(logits_t[e:e + 1, :]) for e in range(N_EXPERTS)]
    sel = [aff[e] + bias_ref[e] for e in range(N_EXPERTS)]
    best_score, best_group = None, None
    for g in range(N_EXPERT_GROUPS):
        a, b, c, d = sel[g * EXPERTS_PER_GROUP:(g + 1) * EXPERTS_PER_GROUP]
        hi1, lo1 = jnp.maximum(a, b), jnp.minimum(a, b)
        hi2, lo2 = jnp.maximum(c, d), jnp.minimum(c, d)
        score = jnp.maximum(hi1, hi2) + jnp.maximum(jnp.minimum(hi1, hi2), jnp.maximum(lo1, lo2))
        if g == 0:
            best_score, best_group = score, jnp.zeros(score.shape, jnp.int32)
        else:
            better = score > best_score
            best_score = jnp.where(better, score, best_score)
            best_group = jnp.where(better, g, best_group)
    gates = []
    for e in range(N_EXPERTS):
        g = e // EXPERTS_PER_GROUP
        beaten = jnp.zeros(best_group.shape, jnp.int32)
        for o in range(g * EXPERTS_PER_GROUP, (g + 1) * EXPERTS_PER_GROUP):
            if o == e:
                continue
            wins = (sel[o] > sel[e]) | ((sel[o] == sel[e]) & (o < e))
            beaten = beaten + wins.astype(jnp.int32)
        chosen = (best_group == g) & (beaten < TOP_K)
        gates.append(jnp.where(chosen, aff[e], 0.0))
    total = functools.reduce(jnp.add, gates)
    inv = 1.0 / total
    return jnp.concatenate([gt * inv for gt in gates], axis=0), best_group


def _post_mixer_kernel(*refs, alpha, glu, route):
    it = iter(refs)
    rb_ref = next(it) if route else None
    a_ref, x_ref = next(it), next(it)
    if glu:
        sc1_ref, sh1_ref, dsk_ref, wv_ref, wg_ref = next(it), next(it), next(it), next(it), next(it)
    else:
        wo_ref = next(it)
    g1_ref, lng_ref, lnb_ref = next(it), next(it), next(it)
    if route:
        sc2_ref, sh2_ref, rw_ref = next(it), next(it), next(it)
    x1_ref = next(it)
    if route:
        h2_ref, comb_ref, grp_ref = next(it), next(it), next(it)

    x = x_ref[...]
    if glu:
        h = x * (1.0 + sc1_ref[0]) + sh1_ref[0]
        act = _gelu_tanh(dsk_ref[...] * h + a_ref[...]).astype(BF16)
        y = (jnp.dot(act, wv_ref[...], preferred_element_type=F32)
             * _sigmoid(jnp.dot(act, wg_ref[...], preferred_element_type=F32)))
    else:
        y = jnp.dot(a_ref[...], wo_ref[...], preferred_element_type=F32)
    x1 = _layer_norm_rows(alpha * x + g1_ref[0] * y, lng_ref[...], lnb_ref[...])
    x1_ref[...] = x1
    if route:
        h2 = x1 * (1.0 + sc2_ref[0]) + sh2_ref[0]
        h2_ref[...] = h2.astype(BF16)
        logits_t = lax.dot_general(rw_ref[...], h2, (((1,), (1,)), ((), ())),
                                   preferred_element_type=F32, precision=HIGHEST)
        comb_ref[...], grp_ref[...] = _router_combine(logits_t, rb_ref)


def _post_mixer(a, x, g1, ln_g, ln_b, alpha, *, w_o=None, glu=None, route=None, tm_target=512):
    n, d = x.shape
    bsz = g1.shape[0]
    per_batch = n // bsz
    tm = _row_tile(per_batch, tm_target)
    tiles_per_batch = per_batch // tm
    row = lambda i: (i, 0)
    batch_vec = lambda i: (i // tiles_per_batch, 0, 0)
    const2 = lambda i: (0, 0)
    vec = pl.BlockSpec((1, 1, d), batch_vec)
    in_specs, args = [], []
    if route is not None:
        in_specs.append(pl.BlockSpec(memory_space=pltpu.SMEM))
        args.append(route[3])
    in_specs += [pl.BlockSpec((tm, d), row), pl.BlockSpec((tm, d), row)]
    args += [a, x]
    if glu is not None:
        sc1, sh1, d_skip, w_val, w_gate = glu
        in_specs += [vec, vec, pl.BlockSpec((1, d), const2),
                     pl.BlockSpec((d, d), const2), pl.BlockSpec((d, d), const2)]
        args += [sc1, sh1, d_skip.reshape(1, d), w_val, w_gate]
    else:
        in_specs.append(pl.BlockSpec((d, d), const2))
        args.append(w_o)
    in_specs += [vec, pl.BlockSpec((1, d), const2), pl.BlockSpec((1, d), const2)]
    args += [g1, ln_g.reshape(1, d), ln_b.reshape(1, d)]
    out_specs = [pl.BlockSpec((tm, d), row)]
    out_shape = [jax.ShapeDtypeStruct((n, d), F32)]
    if route is not None:
        in_specs += [vec, vec, pl.BlockSpec((N_EXPERTS, d), const2)]
        args += [route[0], route[1], route[2]]
        out_specs += [pl.BlockSpec((tm, d), row), pl.BlockSpec((N_EXPERTS, tm), lambda i: (0, i)),
                      pl.BlockSpec((1, tm), lambda i: (0, i))]
        out_shape += [jax.ShapeDtypeStruct((n, d), BF16), jax.ShapeDtypeStruct((N_EXPERTS, n), F32),
                      jax.ShapeDtypeStruct((1, n), jnp.int32)]
    kernel = functools.partial(_post_mixer_kernel, alpha=alpha, glu=glu is not None, route=route is not None)
    return pl.pallas_call(
        kernel,
        grid=(n // tm,),
        in_specs=in_specs,
        out_specs=out_specs,
        out_shape=out_shape,
        compiler_params=_params("parallel"),
        name="post_mixer",
    )(*args)


MOE_BLOCK_ROWS = 256


def _moe_rank_kernel(grp_ref, key_ref, cnt_ref):
    grp = grp_ref[...]
    tm = grp.shape[1]
    member = [grp == g for g in range(N_EXPERT_GROUPS)]
    onehot = jnp.concatenate([m.astype(F32) for m in member]
                             + [jnp.zeros((8 - N_EXPERT_GROUPS, tm), F32)], axis=0)
    earlier = (lax.broadcasted_iota(jnp.int32, (tm, tm), 0)
               < lax.broadcasted_iota(jnp.int32, (tm, tm), 1)).astype(BF16)
    prefix = jnp.dot(onehot.astype(BF16), earlier, preferred_element_type=F32)
    rank = functools.reduce(jnp.add, [jnp.where(member[g], prefix[g:g + 1], 0.0)
                                      for g in range(N_EXPERT_GROUPS)])
    key_ref[...] = (grp * (2 * tm)).astype(F32) + rank
    cnt_ref[0] = jnp.broadcast_to(jnp.sum(onehot, axis=1, keepdims=True), (8, LANES))


def _moe_kernel(cnt_ref, h_ref, keyr_ref, keyc_ref, cg_ref, w1_ref, w3_ref, w2_ref, x_ref, g2_ref, lng_ref, lnb_ref,
                out_ref, acc_sc, *, alpha):
    i, g = pl.program_id(0), pl.program_id(1)
    tm = h_ref.shape[0]
    full, half = MOE_BLOCK_ROWS, MOE_BLOCK_ROWS // 2

    @pl.when(g == 0)
    def _():
        acc_sc[...] = jnp.zeros(acc_sc.shape, F32)

    count = cnt_ref[i * N_EXPERT_GROUPS + g]
    n_full = count // full
    tail = count - n_full * full
    key_row = keyr_ref[...]
    key_col = keyc_ref[...]
    h = h_ref[...]
    first_key = (g * (2 * tm)).astype(F32)

    def run_block(blk, rb):
        base = first_key + (blk * full).astype(F32)
        block_row = lax.broadcasted_iota(jnp.int32, (rb, tm), 0).astype(F32)
        block_col = lax.broadcasted_iota(jnp.int32, (tm, rb), 1).astype(F32)
        onehot = (key_row - base == block_row).astype(F32)
        scatter = (key_col - base == block_col).astype(BF16)
        xs = jnp.dot(onehot.astype(BF16), h, preferred_element_type=F32).astype(BF16)
        z = jnp.zeros((rb, h.shape[1]), F32)
        for e in range(EXPERTS_PER_GROUP):
            weight = jnp.sum(onehot * cg_ref[0, e:e + 1, :], axis=1, keepdims=True)
            a1 = jnp.dot(xs, w1_ref[e], preferred_element_type=F32)
            a3 = jnp.dot(xs, w3_ref[e], preferred_element_type=F32)
            act = (a1 * _sigmoid(a1) * a3 * weight).astype(BF16)
            z = z + jnp.dot(act, w2_ref[e], preferred_element_type=F32)
        acc_sc[...] += jnp.dot(scatter, z.astype(BF16), preferred_element_type=F32)

    def full_block(blk, carry):
        run_block(blk, full)
        return carry

    lax.fori_loop(0, n_full, full_block, 0)

    @pl.when(tail > half)
    def _():
        run_block(n_full, full)

    @pl.when((tail > 0) & (tail <= half))
    def _():
        run_block(n_full, half)

    @pl.when(g == pl.num_programs(1) - 1)
    def _():
        out_ref[...] = _layer_norm_rows(alpha * x_ref[...] + g2_ref[0] * acc_sc[...], lng_ref[...], lnb_ref[...])


def _moe(h2, comb_t, grp, w1, w3, w2, x1, g2, ln_g, ln_b, alpha, tm_target=1024):
    n, d = x1.shape
    bsz = g2.shape[0]
    per_batch = n // bsz
    tm = _row_tile(per_batch, tm_target)
    tiles_per_batch = per_batch // tm
    n_tiles = n // tm
    n_groups = N_EXPERT_GROUPS
    d_expert = w1.shape[2]
    expert_blk = lambda i, g: (g, 0, 0)

    key, cnt = pl.pallas_call(
        _moe_rank_kernel,
        grid=(n_tiles,),
        in_specs=[pl.BlockSpec((1, tm), lambda i: (0, i))],
        out_specs=[pl.BlockSpec((1, tm), lambda i: (0, i)),
                   pl.BlockSpec((1, 8, LANES), lambda i: (i, 0, 0))],
        out_shape=[jax.ShapeDtypeStruct((1, n), F32), jax.ShapeDtypeStruct((n_tiles, 8, LANES), F32)],
        compiler_params=_params("parallel"),
        name="moe_rank",
    )(grp)
    counts = cnt[:, :n_groups, 0].astype(jnp.int32).reshape(-1)
    comb_g = jnp.pad(comb_t.reshape(n_groups, EXPERTS_PER_GROUP, n), ((0, 0), (0, 8 - EXPERTS_PER_GROUP), (0, 0)))

    row = lambda i, g: (i, 0)
    const2 = lambda i, g: (0, 0)
    once = pl.Buffered(1)
    kernel = functools.partial(_moe_kernel, alpha=alpha)
    return pl.pallas_call(
        kernel,
        grid=(n_tiles, n_groups),
        in_specs=[pl.BlockSpec(memory_space=pltpu.SMEM),
                  pl.BlockSpec((tm, d), row),
                  pl.BlockSpec((1, tm), lambda i, g: (0, i)),
                  pl.BlockSpec((tm, 1), row),
                  pl.BlockSpec((1, 8, tm), lambda i, g: (g, 0, i)),
                  pl.BlockSpec((EXPERTS_PER_GROUP, d, d_expert), expert_blk),
                  pl.BlockSpec((EXPERTS_PER_GROUP, d, d_expert), expert_blk),
                  pl.BlockSpec((EXPERTS_PER_GROUP, d_expert, d), expert_blk),
                  pl.BlockSpec((tm, d), row, pipeline_mode=once),
                  pl.BlockSpec((1, 1, d), lambda i, g: (i // tiles_per_batch, 0, 0)),
                  pl.BlockSpec((1, d), const2),
                  pl.BlockSpec((1, d), const2)],
        out_specs=pl.BlockSpec((tm, d), row, pipeline_mode=once),
        out_shape=jax.ShapeDtypeStruct((n, d), F32),
        scratch_shapes=[pltpu.VMEM((tm, d), F32)],
        compiler_params=_params("parallel", "arbitrary"),
        name="moe",
    )(counts, h2, key, key.reshape(n, 1), comb_g, w1, w3, w2, x1, g2, ln_g.reshape(1, d), ln_b.reshape(1, d))


def _modulated_rows(lat_ref, ctx_ref, scale_ref, shift_ref, g, row, class_starts, d_model):
    slabs = []
    for c in range(S5_GROUP_CH):
        channel = g * S5_GROUP_CH + c
        u = jnp.concatenate([lat_ref[c], ctx_ref[c]], axis=0)
        scale = scale_ref[channel]
        shift = shift_ref[channel]
        for k, start in enumerate(class_starts[1:], start=1):
            later = row >= start
            scale = jnp.where(later, scale_ref[k * d_model + channel], scale)
            shift = jnp.where(later, shift_ref[k * d_model + channel], shift)
        slabs.append((u * scale + shift).astype(BF16))
    return jnp.concatenate(slabs, axis=1)


def _s5_inject_kernel(scale_ref, shift_ref, lat_ref, ctx_ref, pin_ref, out_ref, *, class_starts, d_model):
    rows = out_ref.shape[0]
    row = lax.broadcasted_iota(jnp.int32, (rows, SCAN_CHUNK), 0)
    u = _modulated_rows(lat_ref, ctx_ref, scale_ref, shift_ref, pl.program_id(0), row, class_starts, d_model)
    out_ref[...] = jnp.dot(u, pin_ref[0], preferred_element_type=F32)


def _s5_carry_kernel(sin_ref, a_ref, b_ref, h_ref, *, bsz, ctx_chunks, lat_chunks):
    half = 2 * S5_STATE
    ctx0 = bsz * lat_chunks

    def advance(h, row, lo):
        h_ref[row, :, lo:lo + half] = h
        a = a_ref[:, lo:lo + half]
        b = b_ref[:, lo:lo + half]
        return a * h + b * pltpu.roll(h, S5_STATE, 1) + sin_ref[row, :, lo:lo + half]

    zero = jnp.zeros((sin_ref.shape[1], half), F32)
    states = []
    for bi in range(bsz):
        hf, hb = zero, zero
        for n in range(ctx_chunks):
            hf = advance(hf, ctx0 + bi * ctx_chunks + n, 0)
            hb = advance(hb, ctx0 + bi * ctx_chunks + ctx_chunks - 1 - n, half)
        states += [hf, hb]

    def body(n, carry):
        out = []
        for bi in range(bsz):
            base = bi * lat_chunks
            out.append(advance(carry[2 * bi], base + n, 0))
            out.append(advance(carry[2 * bi + 1], base + lat_chunks - 1 - n, half))
        return tuple(out)

    lax.fori_loop(0, lat_chunks, body, tuple(states))


def _s5_output_kernel(scale_ref, shift_ref, lat_ref, ctx_ref, kc_ref, kc_next_ref, h_ref, pout_ref,
                      out_lat_ref, out_ctx_ref, m_even, m_odd, *, class_starts, d_model):
    g = pl.program_id(0)
    rows = h_ref.shape[0]
    lat_rows = lat_ref.shape[1]
    t = SCAN_CHUNK

    def toeplitz_block(lag_ref, cp, c):
        lags = lag_ref[0, cp, c:c + 1, :]
        shifted = pltpu.roll(jnp.broadcast_to(lags, (t, 2 * t)), 0, 1, stride=1, stride_axis=0)
        return shifted[:, t:].astype(BF16)

    @pl.when(g == 0)
    def _():
        def first(cp, carry):
            for c in range(S5_GROUP_CH):
                m_even[pl.ds(pl.multiple_of(cp * t, t), t), c * t:(c + 1) * t] = toeplitz_block(kc_ref, cp, c)
            return carry
        lax.fori_loop(0, S5_GROUP_CH, first, 0)

    row = lax.broadcasted_iota(jnp.int32, (rows, t), 0)
    u = _modulated_rows(lat_ref, ctx_ref, scale_ref, shift_ref, g, row, class_starts, d_model)
    h = h_ref[...].astype(BF16)

    def step(m_now, m_next):
        for cp in range(S5_GROUP_CH):
            for c in range(S5_GROUP_CH):
                m_next[cp * t:(cp + 1) * t, c * t:(c + 1) * t] = toeplitz_block(kc_next_ref, cp, c)
        acc = jnp.dot(u, m_now[...], preferred_element_type=F32)
        for c in range(S5_GROUP_CH):
            y = acc[:, c * t:(c + 1) * t] + jnp.dot(h, pout_ref[0, c], preferred_element_type=F32)
            out_lat_ref[c] = y[:lat_rows]
            out_ctx_ref[c] = y[lat_rows:]

    @pl.when(g % 2 == 0)
    def _():
        step(m_even, m_odd)

    @pl.when(g % 2 == 1)
    def _():
        step(m_odd, m_even)


def _s5_tables(a_re, a_im, log_dt, b_re, b_im, c_re, c_im):
    t = SCAN_CHUNK
    a_re, a_im = a_re.astype(F32), a_im.astype(F32)
    dt = jnp.exp(log_dt.astype(F32))[..., None]
    steps = jnp.arange(t + 1, dtype=F32)[None, None, :, None]
    mag = jnp.exp(steps * (a_re * dt)[:, :, None])
    ang = steps * (a_im * dt)[:, :, None]
    pw_re, pw_im = mag * jnp.cos(ang), mag * jnp.sin(ang)
    lam_re, lam_im = pw_re[:, :, 1], pw_im[:, :, 1]
    inv_den = 1.0 / (a_re * a_re + a_im * a_im)
    n_re = lam_re - 1.0
    f_re = (n_re * a_re + lam_im * a_im) * inv_den
    f_im = (lam_im * a_re - n_re * a_im) * inv_den
    bb_re = f_re[..., None] * b_re - f_im[..., None] * b_im
    bb_im = f_re[..., None] * b_im + f_im[..., None] * b_re

    pt_re = pw_re[:, :, :t].transpose(0, 1, 3, 2)[..., None]
    pt_im = pw_im[:, :, :t].transpose(0, 1, 3, 2)[..., None]
    lb_re = pt_re * bb_re[:, :, :, None] - pt_im * bb_im[:, :, :, None]
    lb_im = pt_re * bb_im[:, :, :, None] + pt_im * bb_re[:, :, :, None]
    resp = (jnp.einsum('dgcp,dgptk->dgkct', c_re, lb_re, precision=HIGHEST)
            - jnp.einsum('dgcp,dgptk->dgkct', c_im, lb_im, precision=HIGHEST))
    fwd, bwd = resp[0], resp[1]
    lags = jnp.concatenate([jnp.zeros_like(fwd[..., :1]), bwd[..., :0:-1],
                            fwd[..., :1] + bwd[..., :1], fwd[..., 1:]], axis=-1)

    def lanes(f_a, f_b, b_a, b_b):
        return jnp.concatenate([f_a, f_b, b_a, b_b], axis=-1)

    in_re = lanes(pw_re[0, :, t - 1::-1], pw_re[0, :, t - 1::-1], pw_re[1, :, :t], pw_re[1, :, :t])
    in_im = lanes(pw_im[0, :, t - 1::-1], pw_im[0, :, t - 1::-1], pw_im[1, :, :t], pw_im[1, :, :t])
    bt_re, bt_im = bb_re.transpose(0, 1, 3, 2), bb_im.transpose(0, 1, 3, 2)
    with_re = lanes(bt_re[0], bt_im[0], bt_re[1], bt_im[1])
    with_im = lanes(-bt_im[0], bt_re[0], -bt_im[1], bt_re[1])
    p_in = (in_re[:, None] * with_re[:, :, None] + in_im[:, None] * with_im[:, :, None]).astype(BF16)
    p_in = p_in.reshape(p_in.shape[0], -1, 4 * S5_STATE)

    def rows(f_a, f_b, b_a, b_b):
        return jnp.concatenate([f_a, f_b, b_a, b_b], axis=-2)

    fr, fi = pw_re[0, :, 1:].transpose(0, 2, 1), pw_im[0, :, 1:].transpose(0, 2, 1)
    br, bi = pw_re[1, :, :0:-1].transpose(0, 2, 1), pw_im[1, :, :0:-1].transpose(0, 2, 1)
    out_x, out_y = rows(fr, fi, br, bi), rows(fi, fr, bi, br)
    cr, ci = c_re[:, :, :, :, None], c_im[:, :, :, :, None]
    with_x = rows(cr[0], -cr[0], cr[1], -cr[1])
    with_y = rows(-ci[0], -ci[0], -ci[1], -ci[1])
    p_out = (with_x * out_x[:, None] + with_y * out_y[:, None]).astype(BF16)

    carry_a = lanes(pw_re[0, :, t], pw_re[0, :, t], pw_re[1, :, t], pw_re[1, :, t])
    carry_b = lanes(-pw_im[0, :, t], pw_im[0, :, t], -pw_im[1, :, t], pw_im[1, :, t])
    return lags, p_in, p_out, carry_a, carry_b


def _to_chunk_major(rows2d):
    n, d = rows2d.shape
    return rows2d.reshape(n // SCAN_CHUNK, SCAN_CHUNK, d).transpose(2, 0, 1)


def _from_chunk_major(xt):
    d, chunks, t = xt.shape
    return xt.transpose(1, 2, 0).reshape(chunks * t, d)


def _s5_scan(x_lat, x_ctx, scale, shift, tables, bsz):
    lags, p_in, p_out, carry_a, carry_b = tables
    d = x_lat.shape[1]
    groups = d // S5_GROUP_CH
    ctx_chunks = x_ctx.shape[0] // bsz // SCAN_CHUNK
    lat_chunks = x_lat.shape[0] // bsz // SCAN_CHUNK
    lat_rows, ctx_rows = bsz * lat_chunks, bsz * ctx_chunks
    rows = lat_rows + ctx_rows
    xt_lat, xt_ctx = _to_chunk_major(x_lat), _to_chunk_major(x_ctx)
    class_starts = tuple(b * lat_chunks for b in range(bsz)) + (lat_rows,)
    scale_flat, shift_flat = scale.reshape(-1), shift.reshape(-1)
    state_w = 4 * S5_STATE
    smem = pl.BlockSpec(memory_space=pltpu.SMEM)
    slab_lat = pl.BlockSpec((S5_GROUP_CH, lat_rows, SCAN_CHUNK), lambda g: (g, 0, 0))
    slab_ctx = pl.BlockSpec((S5_GROUP_CH, ctx_rows, SCAN_CHUNK), lambda g: (g, 0, 0))

    inject = pl.pallas_call(
        functools.partial(_s5_inject_kernel, class_starts=class_starts, d_model=d),
        grid=(groups,),
        in_specs=[smem, smem, slab_lat, slab_ctx,
                  pl.BlockSpec((1, S5_GROUP_CH * SCAN_CHUNK, state_w), lambda g: (g, 0, 0))],
        out_specs=pl.BlockSpec((rows, state_w), lambda g: (0, g)),
        out_shape=jax.ShapeDtypeStruct((rows, groups * state_w), F32),
        compiler_params=_params("parallel"),
        name="s5_inject",
    )(scale_flat, shift_flat, xt_lat, xt_ctx, p_in)

    gb = 8
    entering = pl.pallas_call(
        functools.partial(_s5_carry_kernel, bsz=bsz, ctx_chunks=ctx_chunks, lat_chunks=lat_chunks),
        grid=(groups // gb,),
        in_specs=[pl.BlockSpec((rows, gb, state_w), lambda g: (0, g, 0)),
                  pl.BlockSpec((gb, state_w), lambda g: (g, 0)),
                  pl.BlockSpec((gb, state_w), lambda g: (g, 0))],
        out_specs=pl.BlockSpec((rows, gb, state_w), lambda g: (0, g, 0)),
        out_shape=jax.ShapeDtypeStruct((rows, groups, state_w), F32),
        compiler_params=_params("parallel"),
        name="s5_carry",
    )(inject.reshape(rows, groups, state_w), carry_a, carry_b)

    yt_lat, yt_ctx = pl.pallas_call(
        functools.partial(_s5_output_kernel, class_starts=class_starts, d_model=d),
        grid=(groups,),
        in_specs=[smem, smem, slab_lat, slab_ctx,
                  pl.BlockSpec((1, S5_GROUP_CH, S5_GROUP_CH, 2 * SCAN_CHUNK), lambda g: (g, 0, 0, 0)),
                  pl.BlockSpec((1, S5_GROUP_CH, S5_GROUP_CH, 2 * SCAN_CHUNK),
                               lambda g: (jnp.minimum(g + 1, groups - 1), 0, 0, 0)),
                  pl.BlockSpec((rows, state_w), lambda g: (0, g)),
                  pl.BlockSpec((1, S5_GROUP_CH, state_w, SCAN_CHUNK), lambda g: (g, 0, 0, 0))],
        out_specs=[slab_lat, slab_ctx],
        out_shape=[jax.ShapeDtypeStruct((d, lat_rows, SCAN_CHUNK), F32),
                   jax.ShapeDtypeStruct((d, ctx_rows, SCAN_CHUNK), F32)],
        scratch_shapes=[pltpu.VMEM((S5_GROUP_CH * SCAN_CHUNK, S5_GROUP_CH * SCAN_CHUNK), BF16)] * 2,
        compiler_params=_params("arbitrary"),
        name="s5_output",
    )(scale_flat, shift_flat, xt_lat, xt_ctx, lags, lags, entering.reshape(rows, groups * state_w), p_out)

    return _from_chunk_major(yt_lat), _from_chunk_major(yt_ctx)


def kernel(x, c, ctx, c_ctx, mod_w, mod_b, ln_g, ln_b, router_w, router_b, moe_w1, moe_w3, moe_w2, s5_a_re, s5_a_im, s5_log_dt, s5_b_re, s5_b_im, s5_c_re, s5_c_im, s5_d, s5_w_gate, s5_w_val, swa_w_qkv, swa_w_o, swa_sink, gqa_w_qkv, gqa_w_o, gqa_q_norm, gqa_k_norm):
    bsz, seq, d = x.shape
    n_ctx = ctx.shape[1]
    depth = mod_w.shape[0]
    alpha = (2.0 * depth) ** 0.25
    heads_b, heads_c = d // HEAD_DIM_B, d // HEAD_DIM_C

    cond = jnp.concatenate([c_ctx[None, :], c], axis=0)
    mods = _modulation(cond, mod_w, mod_b).reshape(depth, bsz + 1, 6, d)
    xl = x.reshape(bsz * seq, d)
    xc = ctx.reshape(bsz * n_ctx, d)
    router_w_t = router_w.T
    rope_b = _rope_tables(seq, HEAD_DIM_B)
    rope_c = _rope_tables(seq, HEAD_DIM_C)

    for i in range(depth):
        kind, j = i % N_MIXERS, i // N_MIXERS
        ctx_out = i < depth - 1
        lat_mod = [mods[i, 1:, k][:, None, :] for k in range(6)]
        ctx_mod = [jnp.broadcast_to(mods[i, 0, k][None, None, :], (bsz, 1, d)) for k in range(6)]
        ln1 = (ln_g[i, 0], ln_b[i, 0])
        ln2 = (ln_g[i, 1], ln_b[i, 1])
        buffers = [(xl, lat_mod, True)] + ([(xc, ctx_mod, False)] if ctx_out else [])

        if kind == 0:
            tables = _s5_tables(s5_a_re[j], s5_a_im[j], s5_log_dt[j], s5_b_re[j], s5_b_im[j],
                                s5_c_re[j], s5_c_im[j])
            ctx_last = jnp.roll(mods[i], -1, axis=0)
            y_lat, y_ctx = _s5_scan(xl, xc, 1.0 + ctx_last[:, 1], ctx_last[:, 0], tables, bsz)
            glu_w = (s5_d[j], s5_w_val[j].astype(BF16), s5_w_gate[j].astype(BF16))
            mixed = {True: y_lat, False: y_ctx}
        else:
            if kind == 1:
                w_qkv, w_o = swa_w_qkv[j].astype(BF16), swa_w_o[j].astype(BF16)
                heads, kv, dh, norms, rope = heads_b, KV_HEADS_B, HEAD_DIM_B, None, rope_b
            else:
                w_qkv, w_o = gqa_w_qkv[j].astype(BF16), gqa_w_o[j].astype(BF16)
                heads, kv, dh, norms, rope = heads_c, KV_HEADS_C, HEAD_DIM_C, (gqa_q_norm[j], gqa_k_norm[j]), rope_c
            ql, kl, vl = _qkv_project(xl, lat_mod[1], lat_mod[0], w_qkv, heads, kv, dh, norms=norms, rope=rope)
            qc, kc, vc = _qkv_project(xc, ctx_mod[1], ctx_mod[0], w_qkv, heads, kv, dh, norms=norms, rope=None)
            if kind == 1:
                sink = swa_sink[j].astype(F32)
                o_lat = _window_attention(ql, kl, vl, kc, vc, sink, bsz, dh)
                o_ctx = _flash_attention(qc, kc, vc, bsz, dh, sink=sink) if ctx_out else None
            else:
                keys = jnp.concatenate([kl.reshape(kv, dh, bsz, seq), kc.reshape(kv, dh, bsz, n_ctx)], axis=3)
                vals = jnp.concatenate([vl.reshape(kv, bsz, seq, dh), vc.reshape(kv, bsz, n_ctx, dh)], axis=2)
                o_lat = _flash_attention(ql, keys.reshape(kv, dh, -1), vals.reshape(kv, -1, dh), bsz, dh)
                o_ctx = _flash_attention(qc, kc, vc, bsz, dh) if ctx_out else None
            mixed = {True: o_lat, False: o_ctx}

        w1, w3, w2 = moe_w1[i].astype(BF16), moe_w3[i].astype(BF16), moe_w2[i].astype(BF16)
        new = []
        for rows, mod, is_lat in buffers:
            sh1, sc1, g1, sh2, sc2, g2 = mod
            route = (sc2, sh2, router_w_t, router_b.astype(F32))
            if kind == 0:
                x1, h2, comb_t, grp = _post_mixer(mixed[is_lat], rows, g1, *ln1, alpha,
                                                  glu=(sc1, sh1) + glu_w, route=route)
            else:
                x1, h2, comb_t, grp = _post_mixer(mixed[is_lat], rows, g1, *ln1, alpha, w_o=w_o, route=route)
            new.append(_moe(h2, comb_t, grp, w1, w3, w2, x1, g2, *ln2, alpha))
        xl = new[0]
        if ctx_out:
            xc = new[1]
    return xl.reshape(bsz, seq, d)
```

```python
import functools
import math

import jax
import jax.numpy as jnp
from jax import lax
from jax.experimental import pallas as pl
from jax.experimental.pallas import tpu as pltpu

GRID_W = 64
N_MIXERS = 3
LN_EPS = 1e-6
RMS_EPS = 1e-6
NEG_INF = -1e30
ROPE_THETA = 10000.0
S5_GROUP_CH = 16
S5_STATE = 64
SCAN_CHUNK = 128
WINDOW = 128
HEAD_DIM_B = 64
KV_HEADS_B = 2
HEAD_DIM_C = 128
KV_HEADS_C = 2
N_EXPERTS = 16
N_EXPERT_GROUPS = 4
EXPERTS_PER_GROUP = N_EXPERTS // N_EXPERT_GROUPS
TOP_K = 2

LANES = 128
LOG2E = math.log2(math.e)
VMEM_LIMIT_BYTES = 56 * 1024 * 1024

F32 = jnp.float32
BF16 = jnp.bfloat16
HIGHEST = lax.Precision.HIGHEST


def _params(*semantics):
    return pltpu.CompilerParams(dimension_semantics=semantics,
                                vmem_limit_bytes=VMEM_LIMIT_BYTES)


def _row_tile(n_rows, target):
    t = min(target, n_rows)
    while n_rows % t:
        t //= 2
    return t


def _layer_norm_rows(v, g, b):
    vc = v - jnp.mean(v, axis=-1, keepdims=True)
    var = jnp.mean(vc * vc, axis=-1, keepdims=True)
    return vc * lax.rsqrt(var + LN_EPS) * g + b


def _sigmoid(v):
    return 1.0 / (1.0 + jnp.exp(-v))


def _gelu_tanh(v):
    return 0.5 * v * (1.0 + jnp.tanh(math.sqrt(2.0 / math.pi) * (v + 0.044715 * (v * v * v))))


def _modulation_kernel(cond_ref, w_ref, b_ref, out_ref):
    cond = cond_ref[...]
    act = cond * _sigmoid(cond)
    out_ref[0] = jnp.dot(act, w_ref[0], preferred_element_type=F32, precision=HIGHEST) + b_ref[0]


def _modulation(cond, mod_w, mod_b):
    depth, d, n_out = mod_w.shape
    rows = cond.shape[0]
    tn = 1536
    return pl.pallas_call(
        _modulation_kernel,
        grid=(depth, n_out // tn),
        in_specs=[pl.BlockSpec((rows, d), lambda i, j: (0, 0)),
                  pl.BlockSpec((1, d, tn), lambda i, j: (i, 0, j)),
                  pl.BlockSpec((1, 1, tn), lambda i, j: (i, 0, j))],
        out_specs=pl.BlockSpec((1, rows, tn), lambda i, j: (i, 0, j)),
        out_shape=jax.ShapeDtypeStruct((depth, rows, n_out), F32),
        compiler_params=_params("parallel", "parallel"),
        name="modulation",
    )(cond, mod_w, mod_b.reshape(depth, 1, n_out))


def _swap_rotary_pairs(v, half):
    width = v.shape[-1]
    lane = lax.broadcasted_iota(jnp.int32, v.shape, v.ndim - 1)
    first = (lane // half) % 2 == 0
    return jnp.where(first, pltpu.roll(v, width - half, v.ndim - 1), pltpu.roll(v, half, v.ndim - 1))


def _qkv_kernel(*refs, n_heads, n_kv, head_dim, qk_norm, use_rope):
    it = iter(refs)
    x_ref, sc_ref, sh_ref, w_ref = next(it), next(it), next(it), next(it)
    if qk_norm:
        qn_ref, kn_ref = next(it), next(it)
    if use_rope:
        cos_ref, sin_ref = next(it), next(it)
    q_ref, kt_ref, v_ref = next(it), next(it), next(it)

    h = (x_ref[...] * (1.0 + sc_ref[0]) + sh_ref[0]).astype(BF16)
    proj = jnp.dot(h, w_ref[...], preferred_element_type=F32)
    q_width, kv_width = n_heads * head_dim, n_kv * head_dim
    q = proj[:, :q_width]
    k = proj[:, q_width:q_width + kv_width]
    v = proj[:, q_width + kv_width:]

    def rms_heads(t, gain, count):
        parts = []
        for hd in range(count):
            th = t[:, hd * head_dim:(hd + 1) * head_dim]
            ms = jnp.mean(th * th, axis=-1, keepdims=True)
            parts.append(th * lax.rsqrt(ms + RMS_EPS) * gain)
        return jnp.concatenate(parts, axis=-1)

    if qk_norm:
        q = rms_heads(q, qn_ref[...], n_heads)
        k = rms_heads(k, kn_ref[...], n_kv)

    if use_rope:
        cos, sin = cos_ref[...], sin_ref[...]

        def rope(t):
            reps = t.shape[-1] // LANES
            c = jnp.concatenate([cos] * reps, axis=-1) if reps > 1 else cos
            s = jnp.concatenate([sin] * reps, axis=-1) if reps > 1 else sin
            return t * c + _swap_rotary_pairs(t, head_dim // 4) * s

        q, k = rope(q), rope(k)

    q = q * (head_dim ** -0.5 * LOG2E)
    k_t = k.T
    group = q_width // n_kv
    for g in range(n_kv):
        q_ref[g] = q[:, g * group:(g + 1) * group].astype(BF16)
        kt_ref[g] = k_t[g * head_dim:(g + 1) * head_dim, :].astype(BF16)
        v_ref[g] = v[:, g * head_dim:(g + 1) * head_dim].astype(BF16)


def _qkv_project(x, sc, sh, w, n_heads, n_kv, head_dim, norms=None, rope=None, tm_target=512):
    n, d = x.shape
    bsz = sc.shape[0]
    per_batch = n // bsz
    tm = _row_tile(per_batch, tm_target)
    tiles_per_batch = per_batch // tm
    n_out = w.shape[1]
    group = n_heads * head_dim // n_kv
    row = lambda i: (i, 0)
    batch_vec = lambda i: (i // tiles_per_batch, 0, 0)
    const2 = lambda i: (0, 0)
    in_specs = [pl.BlockSpec((tm, d), row),
                pl.BlockSpec((1, 1, d), batch_vec),
                pl.BlockSpec((1, 1, d), batch_vec),
                pl.BlockSpec((d, n_out), const2)]
    args = [x, sc, sh, w]
    if norms is not None:
        in_specs += [pl.BlockSpec((1, head_dim), const2)] * 2
        args += [norms[0].reshape(1, head_dim), norms[1].reshape(1, head_dim)]
    if rope is not None:
        in_specs += [pl.BlockSpec((tm, LANES), lambda i: (i % tiles_per_batch, 0))] * 2
        args += [rope[0], rope[1]]
    kernel = functools.partial(_qkv_kernel, n_heads=n_heads, n_kv=n_kv, head_dim=head_dim,
                               qk_norm=norms is not None, use_rope=rope is not None)
    return pl.pallas_call(
        kernel,
        grid=(n // tm,),
        in_specs=in_specs,
        out_specs=[pl.BlockSpec((n_kv, tm, group), lambda i: (0, i, 0)),
                   pl.BlockSpec((n_kv, head_dim, tm), lambda i: (0, 0, i)),
                   pl.BlockSpec((n_kv, tm, head_dim), lambda i: (0, i, 0))],
        out_shape=[jax.ShapeDtypeStruct((n_kv, n, group), BF16),
                   jax.ShapeDtypeStruct((n_kv, head_dim, n), BF16),
                   jax.ShapeDtypeStruct((n_kv, n, head_dim), BF16)],
        compiler_params=_params("parallel"),
        name="qkv_project",
    )(*args)


def _rope_tables(seq, head_dim):
    quarter = head_dim // 4
    inv_freq = ROPE_THETA ** (-jnp.arange(quarter, dtype=F32) / quarter)
    t = jnp.arange(seq)
    rows = (t // GRID_W).astype(F32)
    cols = (t % GRID_W).astype(F32)
    ang_r = rows[:, None] * inv_freq
    ang_c = cols[:, None] * inv_freq
    cos = jnp.concatenate([jnp.cos(ang_r)] * 2 + [jnp.cos(ang_c)] * 2, axis=-1)
    sin = jnp.concatenate([-jnp.sin(ang_r), jnp.sin(ang_r), -jnp.sin(ang_c), jnp.sin(ang_c)], axis=-1)
    reps = LANES // head_dim
    return jnp.tile(cos, (1, reps)), jnp.tile(sin, (1, reps))


def _stack_heads(q, n_rep, head_dim):
    if n_rep == 1:
        return q
    return jnp.concatenate([q[:, r * head_dim:(r + 1) * head_dim] for r in range(n_rep)], axis=0)


def _unstack_heads(o, n_rep, rows):
    if n_rep == 1:
        return o
    return jnp.concatenate([o[r * rows:(r + 1) * rows] for r in range(n_rep)], axis=-1)


def _sink_column(sink_ref, g, n_rep, rows):
    return jnp.concatenate([jnp.full((rows, 1), sink_ref[g * n_rep + r] * LOG2E, F32) for r in range(n_rep)],
                           axis=0)


def _ones_width(head_dim):
    return LANES - head_dim % LANES


def _with_ones(v, head_dim):
    return jnp.concatenate([v, jnp.ones((v.shape[0], _ones_width(head_dim)), v.dtype)], axis=1)


def _lane_repeat(col_block, width):
    reps = width // col_block.shape[1]
    return col_block if reps == 1 else jnp.concatenate([col_block] * reps, axis=1)


def _flash_kernel(*refs, n_rep, head_dim, has_sink, n_split):
    if has_sink:
        sink_ref, q_ref, kt_ref, v_ref, o_ref, m_sc, acc_sc = refs
    else:
        q_ref, kt_ref, v_ref, o_ref, m_sc, acc_sc = refs
    kv_head, kj = pl.program_id(1), pl.program_id(3)
    tq = q_ref.shape[1]
    width = acc_sc.shape[1]

    @pl.when(kj == 0)
    def _():
        m_sc[...] = jnp.full(m_sc.shape, NEG_INF, F32)
        acc_sc[...] = jnp.zeros(acc_sc.shape, F32)

    m_all = m_sc[...]
    acc_all = acc_sc[...]
    q = _stack_heads(q_ref[0], n_rep, head_dim)
    kt = kt_ref[0]
    v1 = _with_ones(v_ref[0], head_dim)
    part = n_rep * tq // n_split
    m_out, acc_out = [], []
    for h in range(n_split):
        rs = slice(h * part, (h + 1) * part)
        s = jnp.dot(q[rs], kt, preferred_element_type=F32)
        m_prev = m_all[rs]
        m_new = jnp.maximum(m_prev, jnp.max(s, axis=-1, keepdims=True))
        p = jnp.exp2(s - _lane_repeat(m_new, s.shape[1]))
        alpha = jnp.exp2(m_prev - m_new)
        acc_out.append(_lane_repeat(alpha, width) * acc_all[rs]
                       + jnp.dot(p.astype(BF16), v1, preferred_element_type=F32))
        m_out.append(m_new)
    m_sc[...] = jnp.concatenate(m_out, axis=0)
    acc_sc[...] = jnp.concatenate(acc_out, axis=0)

    @pl.when(kj == pl.num_programs(3) - 1)
    def _():
        acc = acc_sc[...]
        num, l = acc[:, :head_dim], acc[:, head_dim:head_dim + 1]
        if has_sink:
            m = m_sc[:, :1]
            sink = _sink_column(sink_ref, kv_head, n_rep, tq)
            m_fin = jnp.maximum(m, sink)
            scale = jnp.exp2(m - m_fin)
            l = l * scale + jnp.exp2(sink - m_fin)
            num = num * scale
        o_ref[...] = _unstack_heads(num * (1.0 / l), n_rep, tq).astype(o_ref.dtype)


def _flash_attention(q, kt, v, bsz, head_dim, sink=None, tq_target=1024, tk_target=3328, n_split=32):
    n_kv, nq, group = q.shape
    n_rep = group // head_dim
    lq, lk = nq // bsz, v.shape[1] // bsz
    tq = _row_tile(lq, tq_target)
    tk = tk_target if lk % tk_target == 0 else _row_tile(lk, 256)
    qt, kt_tiles = lq // tq, lk // tk
    kernel = functools.partial(_flash_kernel, n_rep=n_rep, head_dim=head_dim, has_sink=sink is not None,
                               n_split=n_split)
    in_specs = [pl.BlockSpec((1, tq, group), lambda b, g, i, j: (g, b * qt + i, 0)),
                pl.BlockSpec((1, head_dim, tk), lambda b, g, i, j: (g, 0, b * kt_tiles + j)),
                pl.BlockSpec((1, tk, head_dim), lambda b, g, i, j: (g, b * kt_tiles + j, 0))]
    args = [q, kt, v]
    if sink is not None:
        in_specs = [pl.BlockSpec(memory_space=pltpu.SMEM)] + in_specs
        args = [sink] + args
    return pl.pallas_call(
        kernel,
        grid=(bsz, n_kv, qt, kt_tiles),
        in_specs=in_specs,
        out_specs=pl.BlockSpec((tq, group), lambda b, g, i, j: (b * qt + i, g)),
        out_shape=jax.ShapeDtypeStruct((nq, n_kv * group), BF16),
        scratch_shapes=[pltpu.VMEM((n_rep * tq, LANES), F32),
                        pltpu.VMEM((n_rep * tq, head_dim + _ones_width(head_dim)), F32)],
        compiler_params=_params("parallel", "parallel", "parallel", "arbitrary"),
        name="flash_attention",
    )(*args)


def _window_kernel(sink_ref, q_ref, kp_ref, kc_ref, kn_ref, kx_ref, vp_ref, vc_ref, vn_ref, vx_ref, o_ref,
                   *, n_rep, head_dim, seq):
    kv_head, i = pl.program_id(1), pl.program_id(2)
    tq = q_ref.shape[1]
    band = tq + 2 * WINDOW
    kt = jnp.concatenate([kp_ref[0], kc_ref[0], kn_ref[0], kx_ref[0]], axis=1)
    v1 = _with_ones(jnp.concatenate([vp_ref[0], vc_ref[0], vn_ref[0], vx_ref[0]], axis=0), head_dim)
    n_keys = kt.shape[1]

    qi = lax.broadcasted_iota(jnp.int32, (tq, n_keys), 0)
    kj = lax.broadcasted_iota(jnp.int32, (tq, n_keys), 1)
    key_pos = i * tq - WINDOW + kj
    ok = (kj >= band) | ((jnp.abs(kj - WINDOW - qi) <= WINDOW) & (key_pos >= 0) & (key_pos < seq))

    q = q_ref[0]
    outs = []
    for r in range(n_rep):
        s = jnp.dot(q[:, r * head_dim:(r + 1) * head_dim], kt, preferred_element_type=F32)
        s = jnp.where(ok, s, NEG_INF)
        sink = sink_ref[kv_head * n_rep + r] * LOG2E
        m = jnp.maximum(jnp.max(s, axis=-1, keepdims=True), sink)
        p = jnp.exp2(s - m)
        pv = jnp.dot(p.astype(BF16), v1, preferred_element_type=F32)
        denom = pv[:, head_dim:head_dim + 1] + jnp.exp2(sink - m)
        outs.append(pv[:, :head_dim] * (1.0 / denom))
    o_ref[...] = jnp.concatenate(outs, axis=1).astype(o_ref.dtype)


def _window_attention(q, kt, v, kt_ctx, v_ctx, sink, bsz, head_dim, tq_target=512):
    n_kv, nq, group = q.shape
    n_rep = group // head_dim
    seq = nq // bsz
    n_ctx = v_ctx.shape[1] // bsz
    tq = _row_tile(seq, tq_target)
    qt = seq // tq
    ratio = tq // WINDOW
    halo_blocks = seq // WINDOW
    prev_blk = lambda b, i: b * halo_blocks + jnp.maximum(i * ratio - 1, 0)
    next_blk = lambda b, i: b * halo_blocks + jnp.minimum((i + 1) * ratio, halo_blocks - 1)
    q_map = lambda b, g, i: (g, b * qt + i, 0)
    k_specs = [pl.BlockSpec((1, head_dim, WINDOW), lambda b, g, i: (g, 0, prev_blk(b, i))),
               pl.BlockSpec((1, head_dim, tq), lambda b, g, i: (g, 0, b * qt + i)),
               pl.BlockSpec((1, head_dim, WINDOW), lambda b, g, i: (g, 0, next_blk(b, i))),
               pl.BlockSpec((1, head_dim, n_ctx), lambda b, g, i: (g, 0, b))]
    v_specs = [pl.BlockSpec((1, WINDOW, head_dim), lambda b, g, i: (g, prev_blk(b, i), 0)),
               pl.BlockSpec((1, tq, head_dim), q_map),
               pl.BlockSpec((1, WINDOW, head_dim), lambda b, g, i: (g, next_blk(b, i), 0)),
               pl.BlockSpec((1, n_ctx, head_dim), lambda b, g, i: (g, b, 0))]
    kernel = functools.partial(_window_kernel, n_rep=n_rep, head_dim=head_dim, seq=seq)
    return pl.pallas_call(
        kernel,
        grid=(bsz, n_kv, qt),
        in_specs=[pl.BlockSpec(memory_space=pltpu.SMEM), pl.BlockSpec((1, tq, group), q_map)] + k_specs + v_specs,
        out_specs=pl.BlockSpec((tq, group), lambda b, g, i: (b * qt + i, g)),
        out_shape=jax.ShapeDtypeStruct((nq, n_kv * group), BF16),
        compiler_params=_params("parallel", "parallel", "parallel"),
        name="window_attention",
    )(sink, q, kt, kt, kt, kt_ctx, v, v, v, v_ctx)


def _router_combine(logits_t, bias_ref):
    aff = [_sigmoid(logits_t[e:e + 1, :]) for e in range(N_EXPERTS)]
    sel = [aff[e] + bias_ref[e] for e in range(N_EXPERTS)]
    best_score, best_group = None, None
    for g in range(N_EXPERT_GROUPS):
        a, b, c, d = sel[g * EXPERTS_PER_GROUP:(g + 1) * EXPERTS_PER_GROUP]
        hi1, lo1 = jnp.maximum(a, b), jnp.minimum(a, b)
        hi2, lo2 = jnp.maximum(c, d), jnp.minimum(c, d)
        score = jnp.maximum(hi1, hi2) + jnp.maximum(jnp.minimum(hi1, hi2), jnp.maximum(lo1, lo2))
        if g == 0:
            best_score, best_group = score, jnp.zeros(score.shape, jnp.int32)
        else:
            better = score > best_score
            best_score = jnp.where(better, score, best_score)
            best_group = jnp.where(better, g, best_group)
    gates = []
    for e in range(N_EXPERTS):
        g = e // EXPERTS_PER_GROUP
        beaten = jnp.zeros(best_group.shape, jnp.int32)
        for o in range(g * EXPERTS_PER_GROUP, (g + 1) * EXPERTS_PER_GROUP):
            if o == e:
                continue
            wins = (sel[o] > sel[e]) | ((sel[o] == sel[e]) & (o < e))
            beaten = beaten + wins.astype(jnp.int32)
        chosen = (best_group == g) & (beaten < TOP_K)
        gates.append(jnp.where(chosen, aff[e], 0.0))
    total = functools.reduce(jnp.add, gates)
    inv = 1.0 / total
    return jnp.concatenate([gt * inv for gt in gates], axis=0), best_group


POST_MIXER_ROWS = 16


def _post_mixer_kernel(*refs, alpha, glu, route):
    it = iter(refs)
    rb_ref = next(it) if route else None
    a_ref, x_ref = next(it), next(it)
    if glu:
        sc1_ref, sh1_ref, dsk_ref, wv_ref, wg_ref = next(it), next(it), next(it), next(it), next(it)
    else:
        wo_ref = next(it)
    g1_ref, lng_ref, lnb_ref = next(it), next(it), next(it)
    if route:
        sc2_ref, sh2_ref, rw_ref = next(it), next(it), next(it)
    x1_ref = next(it)
    if route:
        h2_ref, comb_ref, grp_ref, lo_sc = next(it), next(it), next(it), next(it)

    tm = x_ref.shape[0]
    blocks = [slice(r, r + POST_MIXER_ROWS) for r in range(0, tm, POST_MIXER_ROWS)]
    if glu:
        scale1, shift1, d_skip = 1.0 + sc1_ref[0], sh1_ref[0], dsk_ref[...]
        act = jnp.concatenate([_gelu_tanh(d_skip * (x_ref[b, :] * scale1 + shift1) + a_ref[b, :]).astype(BF16)
                               for b in blocks], axis=0)
        val = jnp.dot(act, wv_ref[...], preferred_element_type=F32)
        gate = jnp.dot(act, wg_ref[...], preferred_element_type=F32)
    else:
        y = jnp.dot(a_ref[...], wo_ref[...], preferred_element_type=F32)
    g1, ln_g, ln_b = g1_ref[0], lng_ref[...], lnb_ref[...]
    if route:
        scale2, shift2 = 1.0 + sc2_ref[0], sh2_ref[0]
    for b in blocks:
        yb = val[b] * _sigmoid(gate[b]) if glu else y[b]
        x1 = _layer_norm_rows(alpha * x_ref[b, :] + g1 * yb, ln_g, ln_b)
        x1_ref[b, :] = x1
        if route:
            h2 = x1 * scale2 + shift2
            hi = h2.astype(BF16)
            h2_ref[b, :] = hi
            lo_sc[b, :] = (h2 - hi.astype(F32)).astype(BF16)
    if route:
        rw = rw_ref[...]
        rw_hi = rw.astype(BF16)
        rw_lo = (rw - rw_hi.astype(F32)).astype(BF16)
        nt = (((1,), (1,)), ((), ()))
        hi_all, lo_all = h2_ref[...], lo_sc[...]
        logits_t = (lax.dot_general(rw_hi, hi_all, nt, preferred_element_type=F32)
                    + lax.dot_general(rw_lo, hi_all, nt, preferred_element_type=F32)
                    + lax.dot_general(rw_hi, lo_all, nt, preferred_element_type=F32))
        comb_ref[...], grp_ref[...] = _router_combine(logits_t, rb_ref)


def _post_mixer(a, x, g1, ln_g, ln_b, alpha, *, w_o=None, glu=None, route=None, tm_target=512):
    n, d = x.shape
    bsz = g1.shape[0]
    per_batch = n // bsz
    tm = _row_tile(per_batch, tm_target)
    tiles_per_batch = per_batch // tm
    row = lambda i: (i, 0)
    batch_vec = lambda i: (i // tiles_per_batch, 0, 0)
    const2 = lambda i: (0, 0)
    vec = pl.BlockSpec((1, 1, d), batch_vec)
    in_specs, args = [], []
    if route is not None:
        in_specs.append(pl.BlockSpec(memory_space=pltpu.SMEM))
        args.append(route[3])
    in_specs += [pl.BlockSpec((tm, d), row), pl.BlockSpec((tm, d), row)]
    args += [a, x]
    if glu is not None:
        sc1, sh1, d_skip, w_val, w_gate = glu
        in_specs += [vec, vec, pl.BlockSpec((1, d), const2),
                     pl.BlockSpec((d, d), const2), pl.BlockSpec((d, d), const2)]
        args += [sc1, sh1, d_skip.reshape(1, d), w_val, w_gate]
    else:
        in_specs.append(pl.BlockSpec((d, d), const2))
        args.append(w_o)
    in_specs += [vec, pl.BlockSpec((1, d), const2), pl.BlockSpec((1, d), const2)]
    args += [g1, ln_g.reshape(1, d), ln_b.reshape(1, d)]
    out_specs = [pl.BlockSpec((tm, d), row)]
    out_shape = [jax.ShapeDtypeStruct((n, d), F32)]
    if route is not None:
        in_specs += [vec, vec, pl.BlockSpec((N_EXPERTS, d), const2)]
        args += [route[0], route[1], route[2]]
        out_specs += [pl.BlockSpec((tm, d), row), pl.BlockSpec((N_EXPERTS, tm), lambda i: (0, i)),
                      pl.BlockSpec((1, tm), lambda i: (0, i))]
        out_shape += [jax.ShapeDtypeStruct((n, d), BF16), jax.ShapeDtypeStruct((N_EXPERTS, n), F32),
                      jax.ShapeDtypeStruct((1, n), jnp.int32)]
    kernel = functools.partial(_post_mixer_kernel, alpha=alpha, glu=glu is not None, route=route is not None)
    return pl.pallas_call(
        kernel,
        grid=(n // tm,),
        in_specs=in_specs,
        out_specs=out_specs,
        out_shape=out_shape,
        scratch_shapes=[pltpu.VMEM((tm, d), BF16)] if route is not None else [],
        compiler_params=_params("parallel"),
        name="post_mixer",
    )(*args)


MOE_BLOCK_ROWS = 256


def _moe_rank_kernel(grp_ref, key_ref, cnt_ref):
    grp = grp_ref[...]
    tm = grp.shape[1]
    member = [grp == g for g in range(N_EXPERT_GROUPS)]
    onehot = jnp.concatenate([m.astype(F32) for m in member]
                             + [jnp.zeros((8 - N_EXPERT_GROUPS, tm), F32)], axis=0)
    earlier = (lax.broadcasted_iota(jnp.int32, (tm, tm), 0)
               < lax.broadcasted_iota(jnp.int32, (tm, tm), 1)).astype(BF16)
    prefix = jnp.dot(onehot.astype(BF16), earlier, preferred_element_type=F32)
    rank = functools.reduce(jnp.add, [jnp.where(member[g], prefix[g:g + 1], 0.0)
                                      for g in range(N_EXPERT_GROUPS)])
    key_ref[...] = (grp * (2 * tm)).astype(F32) + rank
    cnt_ref[0] = jnp.broadcast_to(jnp.sum(onehot, axis=1, keepdims=True), (8, LANES))


def _moe_kernel(cnt_ref, h_ref, keyr_ref, keyc_ref, cg_ref, w1_ref, w3_ref, w2_ref, x_ref, g2_ref, lng_ref, lnb_ref,
                out_ref, acc_sc, *, alpha):
    i, g = pl.program_id(0), pl.program_id(1)
    tm = h_ref.shape[0]
    full, half = MOE_BLOCK_ROWS, MOE_BLOCK_ROWS // 2

    @pl.when(g == 0)
    def _():
        acc_sc[...] = jnp.zeros(acc_sc.shape, F32)

    count = cnt_ref[i * N_EXPERT_GROUPS + g]
    n_full = count // full
    tail = count - n_full * full
    key_row = keyr_ref[...]
    key_col = keyc_ref[...]
    h = h_ref[...]
    first_key = (g * (2 * tm)).astype(F32)

    def run_block(blk, rb):
        base = first_key + (blk * full).astype(F32)
        block_row = lax.broadcasted_iota(jnp.int32, (rb, tm), 0).astype(F32)
        block_col = lax.broadcasted_iota(jnp.int32, (tm, rb), 1).astype(F32)
        onehot = (key_row - base == block_row).astype(F32)
        scatter = (key_col - base == block_col).astype(BF16)
        xs = jnp.dot(onehot.astype(BF16), h, preferred_element_type=F32).astype(BF16)
        z = jnp.zeros((rb, h.shape[1]), F32)
        for e in range(EXPERTS_PER_GROUP):
            weight = jnp.sum(onehot * cg_ref[0, e:e + 1, :], axis=1, keepdims=True)
            a1 = jnp.dot(xs, w1_ref[e], preferred_element_type=F32)
            a3 = jnp.dot(xs, w3_ref[e], preferred_element_type=F32)
            act = (a1 * _sigmoid(a1) * a3 * weight).astype(BF16)
            z = z + jnp.dot(act, w2_ref[e], preferred_element_type=F32)
        acc_sc[...] += jnp.dot(scatter, z.astype(BF16), preferred_element_type=F32)

    def full_block(blk, carry):
        run_block(blk, full)
        return carry

    lax.fori_loop(0, n_full, full_block, 0)

    @pl.when(tail > half)
    def _():
        run_block(n_full, full)

    @pl.when((tail > 0) & (tail <= half))
    def _():
        run_block(n_full, half)

    @pl.when(g == pl.num_programs(1) - 1)
    def _():
        out_ref[...] = _layer_norm_rows(alpha * x_ref[...] + g2_ref[0] * acc_sc[...], lng_ref[...], lnb_ref[...])


def _moe(h2, comb_t, grp, w1, w3, w2, x1, g2, ln_g, ln_b, alpha, tm_target=1024):
    n, d = x1.shape
    bsz = g2.shape[0]
    per_batch = n // bsz
    tm = _row_tile(per_batch, tm_target)
    tiles_per_batch = per_batch // tm
    n_tiles = n // tm
    n_groups = N_EXPERT_GROUPS
    d_expert = w1.shape[2]
    expert_blk = lambda i, g: (g, 0, 0)

    key, cnt = pl.pallas_call(
        _moe_rank_kernel,
        grid=(n_tiles,),
        in_specs=[pl.BlockSpec((1, tm), lambda i: (0, i))],
        out_specs=[pl.BlockSpec((1, tm), lambda i: (0, i)),
                   pl.BlockSpec((1, 8, LANES), lambda i: (i, 0, 0))],
        out_shape=[jax.ShapeDtypeStruct((1, n), F32), jax.ShapeDtypeStruct((n_tiles, 8, LANES), F32)],
        compiler_params=_params("parallel"),
        name="moe_rank",
    )(grp)
    counts = cnt[:, :n_groups, 0].astype(jnp.int32).reshape(-1)
    comb_g = jnp.pad(comb_t.reshape(n_groups, EXPERTS_PER_GROUP, n), ((0, 0), (0, 8 - EXPERTS_PER_GROUP), (0, 0)))

    row = lambda i, g: (i, 0)
    const2 = lambda i, g: (0, 0)
    once = pl.Buffered(1)
    kernel = functools.partial(_moe_kernel, alpha=alpha)
    return pl.pallas_call(
        kernel,
        grid=(n_tiles, n_groups),
        in_specs=[pl.BlockSpec(memory_space=pltpu.SMEM),
                  pl.BlockSpec((tm, d), row),
                  pl.BlockSpec((1, tm), lambda i, g: (0, i)),
                  pl.BlockSpec((tm, 1), row),
                  pl.BlockSpec((1, 8, tm), lambda i, g: (g, 0, i)),
                  pl.BlockSpec((EXPERTS_PER_GROUP, d, d_expert), expert_blk),
                  pl.BlockSpec((EXPERTS_PER_GROUP, d, d_expert), expert_blk),
                  pl.BlockSpec((EXPERTS_PER_GROUP, d_expert, d), expert_blk),
                  pl.BlockSpec((tm, d), row, pipeline_mode=once),
                  pl.BlockSpec((1, 1, d), lambda i, g: (i // tiles_per_batch, 0, 0)),
                  pl.BlockSpec((1, d), const2),
                  pl.BlockSpec((1, d), const2)],
        out_specs=pl.BlockSpec((tm, d), row, pipeline_mode=once),
        out_shape=jax.ShapeDtypeStruct((n, d), F32),
        scratch_shapes=[pltpu.VMEM((tm, d), F32)],
        compiler_params=_params("parallel", "arbitrary"),
        name="moe",
    )(counts, h2, key, key.reshape(n, 1), comb_g, w1, w3, w2, x1, g2, ln_g.reshape(1, d), ln_b.reshape(1, d))


def _modulated_rows(lat_ref, ctx_ref, scale_ref, shift_ref, g, row, class_starts, d_model):
    slabs = []
    for c in range(S5_GROUP_CH):
        channel = g * S5_GROUP_CH + c
        u = jnp.concatenate([lat_ref[c], ctx_ref[c]], axis=0)
        scale = scale_ref[channel]
        shift = shift_ref[channel]
        for k, start in enumerate(class_starts[1:], start=1):
            later = row >= start
            scale = jnp.where(later, scale_ref[k * d_model + channel], scale)
            shift = jnp.where(later, shift_ref[k * d_model + channel], shift)
        slabs.append((u * scale + shift).astype(BF16))
    return jnp.concatenate(slabs, axis=1)


def _s5_inject_kernel(scale_ref, shift_ref, lat_ref, ctx_ref, pin_ref, out_ref, *, class_starts, d_model):
    rows = out_ref.shape[0]
    row = lax.broadcasted_iota(jnp.int32, (rows, SCAN_CHUNK), 0)
    u = _modulated_rows(lat_ref, ctx_ref, scale_ref, shift_ref, pl.program_id(0), row, class_starts, d_model)
    out_ref[...] = jnp.dot(u, pin_ref[0], preferred_element_type=F32)


def _s5_carry_kernel(sin_ref, a_ref, b_ref, h_ref, *, bsz, ctx_chunks, lat_chunks):
    half = 2 * S5_STATE
    ctx0 = bsz * lat_chunks

    def advance(h, row, lo):
        h_ref[row, :, lo:lo + half] = h
        a = a_ref[:, lo:lo + half]
        b = b_ref[:, lo:lo + half]
        return a * h + b * pltpu.roll(h, S5_STATE, 1) + sin_ref[row, :, lo:lo + half]

    zero = jnp.zeros((sin_ref.shape[1], half), F32)
    states = []
    for bi in range(bsz):
        hf, hb = zero, zero
        for n in range(ctx_chunks):
            hf = advance(hf, ctx0 + bi * ctx_chunks + n, 0)
            hb = advance(hb, ctx0 + bi * ctx_chunks + ctx_chunks - 1 - n, half)
        states += [hf, hb]

    def body(n, carry):
        out = []
        for bi in range(bsz):
            base = bi * lat_chunks
            out.append(advance(carry[2 * bi], base + n, 0))
            out.append(advance(carry[2 * bi + 1], base + lat_chunks - 1 - n, half))
        return tuple(out)

    lax.fori_loop(0, lat_chunks, body, tuple(states))


def _s5_output_kernel(scale_ref, shift_ref, lat_ref, ctx_ref, kc_ref, kc_next_ref, h_ref, pout_ref,
                      out_lat_ref, out_ctx_ref, m_even, m_odd, *, class_starts, d_model):
    g = pl.program_id(0)
    rows = h_ref.shape[0]
    lat_rows = lat_ref.shape[1]
    t = SCAN_CHUNK

    def toeplitz_block(lag_ref, cp, c):
        lags = lag_ref[0, cp, c:c + 1, :]
        shifted = pltpu.roll(jnp.broadcast_to(lags, (t, 2 * t)), 0, 1, stride=1, stride_axis=0)
        return shifted[:, t:].astype(BF16)

    @pl.when(g == 0)
    def _():
        def first(cp, carry):
            for c in range(S5_GROUP_CH):
                m_even[pl.ds(pl.multiple_of(cp * t, t), t), c * t:(c + 1) * t] = toeplitz_block(kc_ref, cp, c)
            return carry
        lax.fori_loop(0, S5_GROUP_CH, first, 0)

    row = lax.broadcasted_iota(jnp.int32, (rows, t), 0)
    u = _modulated_rows(lat_ref, ctx_ref, scale_ref, shift_ref, g, row, class_starts, d_model)
    h = h_ref[...].astype(BF16)

    def step(m_now, m_next):
        for cp in range(S5_GROUP_CH):
            for c in range(S5_GROUP_CH):
                m_next[cp * t:(cp + 1) * t, c * t:(c + 1) * t] = toeplitz_block(kc_next_ref, cp, c)
        acc = jnp.dot(u, m_now[...], preferred_element_type=F32)
        for c in range(S5_GROUP_CH):
            y = acc[:, c * t:(c + 1) * t] + jnp.dot(h, pout_ref[0, c], preferred_element_type=F32)
            out_lat_ref[c] = y[:lat_rows]
            out_ctx_ref[c] = y[lat_rows:]

    @pl.when(g % 2 == 0)
    def _():
        step(m_even, m_odd)

    @pl.when(g % 2 == 1)
    def _():
        step(m_odd, m_even)


def _s5_tables(a_re, a_im, log_dt, b_re, b_im, c_re, c_im):
    t = SCAN_CHUNK
    a_re, a_im = a_re.astype(F32), a_im.astype(F32)
    dt = jnp.exp(log_dt.astype(F32))[..., None]
    steps = jnp.arange(t + 1, dtype=F32)[None, None, :, None]
    mag = jnp.exp(steps * (a_re * dt)[:, :, None])
    ang = steps * (a_im * dt)[:, :, None]
    pw_re, pw_im = mag * jnp.cos(ang), mag * jnp.sin(ang)
    lam_re, lam_im = pw_re[:, :, 1], pw_im[:, :, 1]
    inv_den = 1.0 / (a_re * a_re + a_im * a_im)
    n_re = lam_re - 1.0
    f_re = (n_re * a_re + lam_im * a_im) * inv_den
    f_im = (lam_im * a_re - n_re * a_im) * inv_den
    bb_re = f_re[..., None] * b_re - f_im[..., None] * b_im
    bb_im = f_re[..., None] * b_im + f_im[..., None] * b_re

    pt_re = pw_re[:, :, :t].transpose(0, 1, 3, 2)[..., None]
    pt_im = pw_im[:, :, :t].transpose(0, 1, 3, 2)[..., None]
    lb_re = pt_re * bb_re[:, :, :, None] - pt_im * bb_im[:, :, :, None]
    lb_im = pt_re * bb_im[:, :, :, None] + pt_im * bb_re[:, :, :, None]
    resp = (jnp.einsum('dgcp,dgptk->dgkct', c_re, lb_re, precision=HIGHEST)
            - jnp.einsum('dgcp,dgptk->dgkct', c_im, lb_im, precision=HIGHEST))
    fwd, bwd = resp[0], resp[1]
    lags = jnp.concatenate([jnp.zeros_like(fwd[..., :1]), bwd[..., :0:-1],
                            fwd[..., :1] + bwd[..., :1], fwd[..., 1:]], axis=-1)

    def lanes(f_a, f_b, b_a, b_b):
        return jnp.concatenate([f_a, f_b, b_a, b_b], axis=-1)

    in_re = lanes(pw_re[0, :, t - 1::-1], pw_re[0, :, t - 1::-1], pw_re[1, :, :t], pw_re[1, :, :t])
    in_im = lanes(pw_im[0, :, t - 1::-1], pw_im[0, :, t - 1::-1], pw_im[1, :, :t], pw_im[1, :, :t])
    bt_re, bt_im = bb_re.transpose(0, 1, 3, 2), bb_im.transpose(0, 1, 3, 2)
    with_re = lanes(bt_re[0], bt_im[0], bt_re[1], bt_im[1])
    with_im = lanes(-bt_im[0], bt_re[0], -bt_im[1], bt_re[1])
    p_in = (in_re[:, None] * with_re[:, :, None] + in_im[:, None] * with_im[:, :, None]).astype(BF16)
    p_in = p_in.reshape(p_in.shape[0], -1, 4 * S5_STATE)

    def rows(f_a, f_b, b_a, b_b):
        return jnp.concatenate([f_a, f_b, b_a, b_b], axis=-2)

    fr, fi = pw_re[0, :, 1:].transpose(0, 2, 1), pw_im[0, :, 1:].transpose(0, 2, 1)
    br, bi = pw_re[1, :, :0:-1].transpose(0, 2, 1), pw_im[1, :, :0:-1].transpose(0, 2, 1)
    out_x, out_y = rows(fr, fi, br, bi), rows(fi, fr, bi, br)
    cr, ci = c_re[:, :, :, :, None], c_im[:, :, :, :, None]
    with_x = rows(cr[0], -cr[0], cr[1], -cr[1])
    with_y = rows(-ci[0], -ci[0], -ci[1], -ci[1])
    p_out = (with_x * out_x[:, None] + with_y * out_y[:, None]).astype(BF16)

    carry_a = lanes(pw_re[0, :, t], pw_re[0, :, t], pw_re[1, :, t], pw_re[1, :, t])
    carry_b = lanes(-pw_im[0, :, t], pw_im[0, :, t], -pw_im[1, :, t], pw_im[1, :, t])
    return lags, p_in, p_out, carry_a, carry_b


def _to_chunk_major(rows2d):
    n, d = rows2d.shape
    return rows2d.reshape(n // SCAN_CHUNK, SCAN_CHUNK, d).transpose(2, 0, 1)


def _from_chunk_major(xt):
    d, chunks, t = xt.shape
    return xt.transpose(1, 2, 0).reshape(chunks * t, d)


def _s5_scan(x_lat, x_ctx, scale, shift, tables, bsz):
    lags, p_in, p_out, carry_a, carry_b = tables
    d = x_lat.shape[1]
    groups = d // S5_GROUP_CH
    ctx_chunks = x_ctx.shape[0] // bsz // SCAN_CHUNK
    lat_chunks = x_lat.shape[0] // bsz // SCAN_CHUNK
    lat_rows, ctx_rows = bsz * lat_chunks, bsz * ctx_chunks
    rows = lat_rows + ctx_rows
    xt_lat, xt_ctx = _to_chunk_major(x_lat), _to_chunk_major(x_ctx)
    class_starts = tuple(b * lat_chunks for b in range(bsz)) + (lat_rows,)
    scale_flat, shift_flat = scale.reshape(-1), shift.reshape(-1)
    state_w = 4 * S5_STATE
    smem = pl.BlockSpec(memory_space=pltpu.SMEM)
    slab_lat = pl.BlockSpec((S5_GROUP_CH, lat_rows, SCAN_CHUNK), lambda g: (g, 0, 0))
    slab_ctx = pl.BlockSpec((S5_GROUP_CH, ctx_rows, SCAN_CHUNK), lambda g: (g, 0, 0))

    inject = pl.pallas_call(
        functools.partial(_s5_inject_kernel, class_starts=class_starts, d_model=d),
        grid=(groups,),
        in_specs=[smem, smem, slab_lat, slab_ctx,
                  pl.BlockSpec((1, S5_GROUP_CH * SCAN_CHUNK, state_w), lambda g: (g, 0, 0))],
        out_specs=pl.BlockSpec((rows, state_w), lambda g: (0, g)),
        out_shape=jax.ShapeDtypeStruct((rows, groups * state_w), F32),
        compiler_params=_params("parallel"),
        name="s5_inject",
    )(scale_flat, shift_flat, xt_lat, xt_ctx, p_in)

    gb = 8
    entering = pl.pallas_call(
        functools.partial(_s5_carry_kernel, bsz=bsz, ctx_chunks=ctx_chunks, lat_chunks=lat_chunks),
        grid=(groups // gb,),
        in_specs=[pl.BlockSpec((rows, gb, state_w), lambda g: (0, g, 0)),
                  pl.BlockSpec((gb, state_w), lambda g: (g, 0)),
                  pl.BlockSpec((gb, state_w), lambda g: (g, 0))],
        out_specs=pl.BlockSpec((rows, gb, state_w), lambda g: (0, g, 0)),
        out_shape=jax.ShapeDtypeStruct((rows, groups, state_w), F32),
        compiler_params=_params("parallel"),
        name="s5_carry",
    )(inject.reshape(rows, groups, state_w), carry_a, carry_b)

    yt_lat, yt_ctx = pl.pallas_call(
        functools.partial(_s5_output_kernel, class_starts=class_starts, d_model=d),
        grid=(groups,),
        in_specs=[smem, smem, slab_lat, slab_ctx,
                  pl.BlockSpec((1, S5_GROUP_CH, S5_GROUP_CH, 2 * SCAN_CHUNK), lambda g: (g, 0, 0, 0)),
                  pl.BlockSpec((1, S5_GROUP_CH, S5_GROUP_CH, 2 * SCAN_CHUNK),
                               lambda g: (jnp.minimum(g + 1, groups - 1), 0, 0, 0)),
                  pl.BlockSpec((rows, state_w), lambda g: (0, g)),
                  pl.BlockSpec((1, S5_GROUP_CH, state_w, SCAN_CHUNK), lambda g: (g, 0, 0, 0))],
        out_specs=[slab_lat, slab_ctx],
        out_shape=[jax.ShapeDtypeStruct((d, lat_rows, SCAN_CHUNK), F32),
                   jax.ShapeDtypeStruct((d, ctx_rows, SCAN_CHUNK), F32)],
        scratch_shapes=[pltpu.VMEM((S5_GROUP_CH * SCAN_CHUNK, S5_GROUP_CH * SCAN_CHUNK), BF16)] * 2,
        compiler_params=_params("arbitrary"),
        name="s5_output",
    )(scale_flat, shift_flat, xt_lat, xt_ctx, lags, lags, entering.reshape(rows, groups * state_w), p_out)

    return _from_chunk_major(yt_lat), _from_chunk_major(yt_ctx)


def kernel(x, c, ctx, c_ctx, mod_w, mod_b, ln_g, ln_b, router_w, router_b, moe_w1, moe_w3, moe_w2, s5_a_re, s5_a_im, s5_log_dt, s5_b_re, s5_b_im, s5_c_re, s5_c_im, s5_d, s5_w_gate, s5_w_val, swa_w_qkv, swa_w_o, swa_sink, gqa_w_qkv, gqa_w_o, gqa_q_norm, gqa_k_norm):
    bsz, seq, d = x.shape
    n_ctx = ctx.shape[1]
    depth = mod_w.shape[0]
    alpha = (2.0 * depth) ** 0.25
    heads_b, heads_c = d // HEAD_DIM_B, d // HEAD_DIM_C

    cond = jnp.concatenate([c_ctx[None, :], c], axis=0)
    mods = _modulation(cond, mod_w, mod_b).reshape(depth, bsz + 1, 6, d)
    xl = x.reshape(bsz * seq, d)
    xc = ctx.reshape(bsz * n_ctx, d)
    router_w_t = router_w.T
    rope_b = _rope_tables(seq, HEAD_DIM_B)
    rope_c = _rope_tables(seq, HEAD_DIM_C)

    for i in range(depth):
        kind, j = i % N_MIXERS, i // N_MIXERS
        ctx_out = i < depth - 1
        lat_mod = [mods[i, 1:, k][:, None, :] for k in range(6)]
        ctx_mod = [jnp.broadcast_to(mods[i, 0, k][None, None, :], (bsz, 1, d)) for k in range(6)]
        ln1 = (ln_g[i, 0], ln_b[i, 0])
        ln2 = (ln_g[i, 1], ln_b[i, 1])
        buffers = [(xl, lat_mod, True)] + ([(xc, ctx_mod, False)] if ctx_out else [])

        if kind == 0:
            tables = _s5_tables(s5_a_re[j], s5_a_im[j], s5_log_dt[j], s5_b_re[j], s5_b_im[j],
                                s5_c_re[j], s5_c_im[j])
            ctx_last = jnp.roll(mods[i], -1, axis=0)
            y_lat, y_ctx = _s5_scan(xl, xc, 1.0 + ctx_last[:, 1], ctx_last[:, 0], tables, bsz)
            glu_w = (s5_d[j], s5_w_val[j].astype(BF16), s5_w_gate[j].astype(BF16))
            mixed = {True: y_lat, False: y_ctx}
        else:
            if kind == 1:
                w_qkv, w_o = swa_w_qkv[j].astype(BF16), swa_w_o[j].astype(BF16)
                heads, kv, dh, norms, rope = heads_b, KV_HEADS_B, HEAD_DIM_B, None, rope_b
            else:
                w_qkv, w_o = gqa_w_qkv[j].astype(BF16), gqa_w_o[j].astype(BF16)
                heads, kv, dh, norms, rope = heads_c, KV_HEADS_C, HEAD_DIM_C, (gqa_q_norm[j], gqa_k_norm[j]), rope_c
            ql, kl, vl = _qkv_project(xl, lat_mod[1], lat_mod[0], w_qkv, heads, kv, dh, norms=norms, rope=rope)
            qc, kc, vc = _qkv_project(xc, ctx_mod[1], ctx_mod[0], w_qkv, heads, kv, dh, norms=norms, rope=None)
            if kind == 1:
                sink = swa_sink[j].astype(F32)
                o_lat = _window_attention(ql, kl, vl, kc, vc, sink, bsz, dh)
                o_ctx = _flash_attention(qc, kc, vc, bsz, dh, sink=sink) if ctx_out else None
            else:
                keys = jnp.concatenate([kl.reshape(kv, dh, bsz, seq), kc.reshape(kv, dh, bsz, n_ctx)], axis=3)
                vals = jnp.concatenate([vl.reshape(kv, bsz, seq, dh), vc.reshape(kv, bsz, n_ctx, dh)], axis=2)
                o_lat = _flash_attention(ql, keys.reshape(kv, dh, -1), vals.reshape(kv, -1, dh), bsz, dh)
                o_ctx = _flash_attention(qc, kc, vc, bsz, dh) if ctx_out else None
            mixed = {True: o_lat, False: o_ctx}

        w1, w3, w2 = moe_w1[i].astype(BF16), moe_w3[i].astype(BF16), moe_w2[i].astype(BF16)
        new = []
        for rows, mod, is_lat in buffers:
            sh1, sc1, g1, sh2, sc2, g2 = mod
            route = (sc2, sh2, router_w_t, router_b.astype(F32))
            if kind == 0:
                x1, h2, comb_t, grp = _post_mixer(mixed[is_lat], rows, g1, *ln1, alpha,
                                                  glu=(sc1, sh1) + glu_w, route=route)
            else:
                x1, h2, comb_t, grp = _post_mixer(mixed[is_lat], rows, g1, *ln1, alpha, w_o=w_o, route=route)
            new.append(_moe(h2, comb_t, grp, w1, w3, w2, x1, g2, *ln2, alpha))
        xl = new[0]
        if ctx_out:
            xc = new[1]
    return xl.reshape(bsz, seq, d)
```

```python
import functools
import math

import jax
import jax.numpy as jnp
from jax import lax
from jax.experimental import pallas as pl
from jax.experimental.pallas import tpu as pltpu

GRID_W = 64
N_MIXERS = 3
LN_EPS = 1e-6
RMS_EPS = 1e-6
NEG_INF = -1e30
ROPE_THETA = 10000.0
S5_GROUP_CH = 16
S5_STATE = 64
SCAN_CHUNK = 128
WINDOW = 128
HEAD_DIM_B = 64
KV_HEADS_B = 2
HEAD_DIM_C = 128
KV_HEADS_C = 2
N_EXPERTS = 16
N_EXPERT_GROUPS = 4
EXPERTS_PER_GROUP = N_EXPERTS // N_EXPERT_GROUPS
TOP_K = 2

LANES = 128
LOG2E = math.log2(math.e)
VMEM_LIMIT_BYTES = 56 * 1024 * 1024

F32 = jnp.float32
BF16 = jnp.bfloat16
HIGHEST = lax.Precision.HIGHEST


def _params(*semantics):
    return pltpu.CompilerParams(dimension_semantics=semantics,
                                vmem_limit_bytes=VMEM_LIMIT_BYTES)


def _row_tile(n_rows, target):
    t = min(target, n_rows)
    while n_rows % t:
        t //= 2
    return t


def _layer_norm_rows(v, g, b):
    vc = v - jnp.mean(v, axis=-1, keepdims=True)
    var = jnp.mean(vc * vc, axis=-1, keepdims=True)
    return vc * lax.rsqrt(var + LN_EPS) * g + b


def _sigmoid(v):
    return 1.0 / (1.0 + jnp.exp(-v))


def _gelu_tanh(v):
    return 0.5 * v * (1.0 + jnp.tanh(math.sqrt(2.0 / math.pi) * (v + 0.044715 * (v * v * v))))


def _modulation_kernel(cond_t_ref, w_ref, b_ref, out_ref):
    cond = cond_t_ref[...]
    act = cond * _sigmoid(cond)
    w = w_ref[0]
    rows = [jnp.sum(w * act[:, r:r + 1], axis=0, keepdims=True) for r in range(act.shape[1])]
    out_ref[0] = jnp.concatenate(rows, axis=0) + b_ref[0]


def _modulation(cond, mod_w, mod_b):
    depth, d, n_out = mod_w.shape
    rows = cond.shape[0]
    tn = 1536
    return pl.pallas_call(
        _modulation_kernel,
        grid=(depth, n_out // tn),
        in_specs=[pl.BlockSpec((d, rows), lambda i, j: (0, 0)),
                  pl.BlockSpec((1, d, tn), lambda i, j: (i, 0, j)),
                  pl.BlockSpec((1, 1, tn), lambda i, j: (i, 0, j))],
        out_specs=pl.BlockSpec((1, rows, tn), lambda i, j: (i, 0, j)),
        out_shape=jax.ShapeDtypeStruct((depth, rows, n_out), F32),
        compiler_params=_params("parallel", "parallel"),
        name="modulation",
    )(cond.T, mod_w, mod_b.reshape(depth, 1, n_out))


def _swap_rotary_pairs(v, half):
    width = v.shape[-1]
    lane = lax.broadcasted_iota(jnp.int32, v.shape, v.ndim - 1)
    first = (lane // half) % 2 == 0
    return jnp.where(first, pltpu.roll(v, width - half, v.ndim - 1), pltpu.roll(v, half, v.ndim - 1))


def _qkv_kernel(*refs, n_heads, n_kv, head_dim, qk_norm, use_rope):
    it = iter(refs)
    x_ref, sc_ref, sh_ref, w_ref = next(it), next(it), next(it), next(it)
    if qk_norm:
        qn_ref, kn_ref = next(it), next(it)
    if use_rope:
        cos_ref, sin_ref = next(it), next(it)
    q_ref, kt_ref, v_ref = next(it), next(it), next(it)

    h = (x_ref[...] * (1.0 + sc_ref[0]) + sh_ref[0]).astype(BF16)
    proj = jnp.dot(h, w_ref[...], preferred_element_type=F32)
    q_width, kv_width = n_heads * head_dim, n_kv * head_dim
    q = proj[:, :q_width]
    k = proj[:, q_width:q_width + kv_width]
    v = proj[:, q_width + kv_width:]

    def rms_heads(t, gain, count):
        parts = []
        for hd in range(count):
            th = t[:, hd * head_dim:(hd + 1) * head_dim]
            ms = jnp.mean(th * th, axis=-1, keepdims=True)
            parts.append(th * lax.rsqrt(ms + RMS_EPS) * gain)
        return jnp.concatenate(parts, axis=-1)

    if qk_norm:
        q = rms_heads(q, qn_ref[...], n_heads)
        k = rms_heads(k, kn_ref[...], n_kv)

    if use_rope:
        cos, sin = cos_ref[...], sin_ref[...]

        def rope(t):
            reps = t.shape[-1] // LANES
            c = jnp.concatenate([cos] * reps, axis=-1) if reps > 1 else cos
            s = jnp.concatenate([sin] * reps, axis=-1) if reps > 1 else sin
            return t * c + _swap_rotary_pairs(t, head_dim // 4) * s

        q, k = rope(q), rope(k)

    q = q * (head_dim ** -0.5 * LOG2E)
    k_t = k.T
    group = q_width // n_kv
    for g in range(n_kv):
        q_ref[g] = q[:, g * group:(g + 1) * group].astype(BF16)
        kt_ref[g] = k_t[g * head_dim:(g + 1) * head_dim, :].astype(BF16)
        v_ref[g] = v[:, g * head_dim:(g + 1) * head_dim].astype(BF16)


def _qkv_project(x, sc, sh, w, n_heads, n_kv, head_dim, norms=None, rope=None, tm_target=512):
    n, d = x.shape
    bsz = sc.shape[0]
    per_batch = n // bsz
    tm = _row_tile(per_batch, tm_target)
    tiles_per_batch = per_batch // tm
    n_out = w.shape[1]
    group = n_heads * head_dim // n_kv
    row = lambda i: (i, 0)
    batch_vec = lambda i: (i // tiles_per_batch, 0, 0)
    const2 = lambda i: (0, 0)
    in_specs = [pl.BlockSpec((tm, d), row),
                pl.BlockSpec((1, 1, d), batch_vec),
                pl.BlockSpec((1, 1, d), batch_vec),
                pl.BlockSpec((d, n_out), const2)]
    args = [x, sc, sh, w]
    if norms is not None:
        in_specs += [pl.BlockSpec((1, head_dim), const2)] * 2
        args += [norms[0].reshape(1, head_dim), norms[1].reshape(1, head_dim)]
    if rope is not None:
        in_specs += [pl.BlockSpec((tm, LANES), lambda i: (i % tiles_per_batch, 0))] * 2
        args += [rope[0], rope[1]]
    kernel = functools.partial(_qkv_kernel, n_heads=n_heads, n_kv=n_kv, head_dim=head_dim,
                               qk_norm=norms is not None, use_rope=rope is not None)
    return pl.pallas_call(
        kernel,
        grid=(n // tm,),
        in_specs=in_specs,
        out_specs=[pl.BlockSpec((n_kv, tm, group), lambda i: (0, i, 0)),
                   pl.BlockSpec((n_kv, head_dim, tm), lambda i: (0, 0, i)),
                   pl.BlockSpec((n_kv, tm, head_dim), lambda i: (0, i, 0))],
        out_shape=[jax.ShapeDtypeStruct((n_kv, n, group), BF16),
                   jax.ShapeDtypeStruct((n_kv, head_dim, n), BF16),
                   jax.ShapeDtypeStruct((n_kv, n, head_dim), BF16)],
        compiler_params=_params("parallel"),
        name="qkv_project",
    )(*args)


def _rope_tables(seq, head_dim):
    quarter = head_dim // 4
    inv_freq = ROPE_THETA ** (-jnp.arange(quarter, dtype=F32) / quarter)
    t = jnp.arange(seq)
    rows = (t // GRID_W).astype(F32)
    cols = (t % GRID_W).astype(F32)
    ang_r = rows[:, None] * inv_freq
    ang_c = cols[:, None] * inv_freq
    cos = jnp.concatenate([jnp.cos(ang_r)] * 2 + [jnp.cos(ang_c)] * 2, axis=-1)
    sin = jnp.concatenate([-jnp.sin(ang_r), jnp.sin(ang_r), -jnp.sin(ang_c), jnp.sin(ang_c)], axis=-1)
    reps = LANES // head_dim
    return jnp.tile(cos, (1, reps)), jnp.tile(sin, (1, reps))


def _stack_heads(q, n_rep, head_dim):
    if n_rep == 1:
        return q
    return jnp.concatenate([q[:, r * head_dim:(r + 1) * head_dim] for r in range(n_rep)], axis=0)


def _unstack_heads(o, n_rep, rows):
    if n_rep == 1:
        return o
    return jnp.concatenate([o[r * rows:(r + 1) * rows] for r in range(n_rep)], axis=-1)


def _sink_column(sink_ref, g, n_rep, rows):
    return jnp.concatenate([jnp.full((rows, 1), sink_ref[g * n_rep + r] * LOG2E, F32) for r in range(n_rep)],
                           axis=0)


def _ones_width(head_dim):
    return LANES - head_dim % LANES


def _with_ones(v, head_dim):
    return jnp.concatenate([v, jnp.ones((v.shape[0], _ones_width(head_dim)), v.dtype)], axis=1)


def _lane_repeat(col_block, width):
    reps = width // col_block.shape[1]
    return col_block if reps == 1 else jnp.concatenate([col_block] * reps, axis=1)


def _flash_kernel(*refs, n_rep, head_dim, has_sink, n_split):
    if has_sink:
        sink_ref, q_ref, kt_ref, v_ref, o_ref, m_sc, acc_sc = refs
    else:
        q_ref, kt_ref, v_ref, o_ref, m_sc, acc_sc = refs
    kv_head, kj = pl.program_id(1), pl.program_id(3)
    tq = q_ref.shape[1]
    width = acc_sc.shape[1]

    @pl.when(kj == 0)
    def _():
        m_sc[...] = jnp.full(m_sc.shape, NEG_INF, F32)
        acc_sc[...] = jnp.zeros(acc_sc.shape, F32)

    m_all = m_sc[...]
    acc_all = acc_sc[...]
    q = _stack_heads(q_ref[0], n_rep, head_dim)
    kt = kt_ref[0]
    v1 = _with_ones(v_ref[0], head_dim)
    part = n_rep * tq // n_split
    m_out, acc_out = [], []
    for h in range(n_split):
        rs = slice(h * part, (h + 1) * part)
        s = jnp.dot(q[rs], kt, preferred_element_type=F32)
        m_prev = m_all[rs]
        m_new = jnp.maximum(m_prev, jnp.max(s, axis=-1, keepdims=True))
        p = jnp.exp2(s - _lane_repeat(m_new, s.shape[1]))
        alpha = jnp.exp2(m_prev - m_new)
        acc_out.append(_lane_repeat(alpha, width) * acc_all[rs]
                       + jnp.dot(p.astype(BF16), v1, preferred_element_type=F32))
        m_out.append(m_new)
    m_sc[...] = jnp.concatenate(m_out, axis=0)
    acc_sc[...] = jnp.concatenate(acc_out, axis=0)

    @pl.when(kj == pl.num_programs(3) - 1)
    def _():
        acc = acc_sc[...]
        num, l = acc[:, :head_dim], acc[:, head_dim:head_dim + 1]
        if has_sink:
            m = m_sc[:, :1]
            sink = _sink_column(sink_ref, kv_head, n_rep, tq)
            m_fin = jnp.maximum(m, sink)
            scale = jnp.exp2(m - m_fin)
            l = l * scale + jnp.exp2(sink - m_fin)
            num = num * scale
        o_ref[...] = _unstack_heads(num * (1.0 / l), n_rep, tq).astype(o_ref.dtype)


def _flash_attention(q, kt, v, bsz, head_dim, sink=None, tq_target=1024, tk_target=3328, n_split=32):
    n_kv, nq, group = q.shape
    n_rep = group // head_dim
    lq, lk = nq // bsz, v.shape[1] // bsz
    tq = _row_tile(lq, tq_target)
    tk = tk_target if lk % tk_target == 0 else _row_tile(lk, 256)
    qt, kt_tiles = lq // tq, lk // tk
    kernel = functools.partial(_flash_kernel, n_rep=n_rep, head_dim=head_dim, has_sink=sink is not None,
                               n_split=n_split)
    in_specs = [pl.BlockSpec((1, tq, group), lambda b, g, i, j: (g, b * qt + i, 0)),
                pl.BlockSpec((1, head_dim, tk), lambda b, g, i, j: (g, 0, b * kt_tiles + j)),
                pl.BlockSpec((1, tk, head_dim), lambda b, g, i, j: (g, b * kt_tiles + j, 0))]
    args = [q, kt, v]
    if sink is not None:
        in_specs = [pl.BlockSpec(memory_space=pltpu.SMEM)] + in_specs
        args = [sink] + args
    return pl.pallas_call(
        kernel,
        grid=(bsz, n_kv, qt, kt_tiles),
        in_specs=in_specs,
        out_specs=pl.BlockSpec((tq, group), lambda b, g, i, j: (b * qt + i, g)),
        out_shape=jax.ShapeDtypeStruct((nq, n_kv * group), BF16),
        scratch_shapes=[pltpu.VMEM((n_rep * tq, LANES), F32),
                        pltpu.VMEM((n_rep * tq, head_dim + _ones_width(head_dim)), F32)],
        compiler_params=_params("parallel", "parallel", "parallel", "arbitrary"),
        name="flash_attention",
    )(*args)


def _window_kernel(sink_ref, q_ref, kp_ref, kc_ref, kn_ref, kx_ref, vp_ref, vc_ref, vn_ref, vx_ref, o_ref,
                   *, n_rep, head_dim, seq):
    kv_head, i = pl.program_id(1), pl.program_id(2)
    tq = q_ref.shape[1]
    band = tq + 2 * WINDOW
    kt = jnp.concatenate([kp_ref[0], kc_ref[0], kn_ref[0], kx_ref[0]], axis=1)
    v1 = _with_ones(jnp.concatenate([vp_ref[0], vc_ref[0], vn_ref[0], vx_ref[0]], axis=0), head_dim)
    n_keys = kt.shape[1]

    qi = lax.broadcasted_iota(jnp.int32, (tq, n_keys), 0)
    kj = lax.broadcasted_iota(jnp.int32, (tq, n_keys), 1)
    key_pos = i * tq - WINDOW + kj
    ok = (kj >= band) | ((jnp.abs(kj - WINDOW - qi) <= WINDOW) & (key_pos >= 0) & (key_pos < seq))

    q = q_ref[0]
    outs = []
    for r in range(n_rep):
        s = jnp.dot(q[:, r * head_dim:(r + 1) * head_dim], kt, preferred_element_type=F32)
        s = jnp.where(ok, s, NEG_INF)
        sink = sink_ref[kv_head * n_rep + r] * LOG2E
        m = jnp.maximum(jnp.max(s, axis=-1, keepdims=True), sink)
        p = jnp.exp2(s - m)
        pv = jnp.dot(p.astype(BF16), v1, preferred_element_type=F32)
        denom = pv[:, head_dim:head_dim + 1] + jnp.exp2(sink - m)
        outs.append(pv[:, :head_dim] * (1.0 / denom))
    o_ref[...] = jnp.concatenate(outs, axis=1).astype(o_ref.dtype)


def _window_attention(q, kt, v, kt_ctx, v_ctx, sink, bsz, head_dim, tq_target=512):
    n_kv, nq, group = q.shape
    n_rep = group // head_dim
    seq = nq // bsz
    n_ctx = v_ctx.shape[1] // bsz
    tq = _row_tile(seq, tq_target)
    qt = seq // tq
    ratio = tq // WINDOW
    halo_blocks = seq // WINDOW
    prev_blk = lambda b, i: b * halo_blocks + jnp.maximum(i * ratio - 1, 0)
    next_blk = lambda b, i: b * halo_blocks + jnp.minimum((i + 1) * ratio, halo_blocks - 1)
    q_map = lambda b, g, i: (g, b * qt + i, 0)
    k_specs = [pl.BlockSpec((1, head_dim, WINDOW), lambda b, g, i: (g, 0, prev_blk(b, i))),
               pl.BlockSpec((1, head_dim, tq), lambda b, g, i: (g, 0, b * qt + i)),
               pl.BlockSpec((1, head_dim, WINDOW), lambda b, g, i: (g, 0, next_blk(b, i))),
               pl.BlockSpec((1, head_dim, n_ctx), lambda b, g, i: (g, 0, b))]
    v_specs = [pl.BlockSpec((1, WINDOW, head_dim), lambda b, g, i: (g, prev_blk(b, i), 0)),
               pl.BlockSpec((1, tq, head_dim), q_map),
               pl.BlockSpec((1, WINDOW, head_dim), lambda b, g, i: (g, next_blk(b, i), 0)),
               pl.BlockSpec((1, n_ctx, head_dim), lambda b, g, i: (g, b, 0))]
    kernel = functools.partial(_window_kernel, n_rep=n_rep, head_dim=head_dim, seq=seq)
    return pl.pallas_call(
        kernel,
        grid=(bsz, n_kv, qt),
        in_specs=[pl.BlockSpec(memory_space=pltpu.SMEM), pl.BlockSpec((1, tq, group), q_map)] + k_specs + v_specs,
        out_specs=pl.BlockSpec((tq, group), lambda b, g, i: (b * qt + i, g)),
        out_shape=jax.ShapeDtypeStruct((nq, n_kv * group), BF16),
        compiler_params=_params("parallel", "parallel", "parallel"),
        name="window_attention",
    )(sink, q, kt, kt, kt, kt_ctx, v, v, v, v_ctx)


def _router_combine(logits_t, bias_ref):
    aff = [_sigmoid(logits_t[e:e + 1, :]) for e in range(N_EXPERTS)]
    sel = [aff[e] + bias_ref[e] for e in range(N_EXPERTS)]
    best_score, best_group = None, None
    for g in range(N_EXPERT_GROUPS):
        a, b, c, d = sel[g * EXPERTS_PER_GROUP:(g + 1) * EXPERTS_PER_GROUP]
        hi1, lo1 = jnp.maximum(a, b), jnp.minimum(a, b)
        hi2, lo2 = jnp.maximum(c, d), jnp.minimum(c, d)
        score = jnp.maximum(hi1, hi2) + jnp.maximum(jnp.minimum(hi1, hi2), jnp.maximum(lo1, lo2))
        if g == 0:
            best_score, best_group = score, jnp.zeros(score.shape, jnp.int32)
        else:
            better = score > best_score
            best_score = jnp.where(better, score, best_score)
            best_group = jnp.where(better, g, best_group)
    gates = []
    for e in range(N_EXPERTS):
        g = e // EXPERTS_PER_GROUP
        beaten = jnp.zeros(best_group.shape, jnp.int32)
        for o in range(g * EXPERTS_PER_GROUP, (g + 1) * EXPERTS_PER_GROUP):
            if o == e:
                continue
            wins = (sel[o] > sel[e]) | ((sel[o] == sel[e]) & (o < e))
            beaten = beaten + wins.astype(jnp.int32)
        chosen = (best_group == g) & (beaten < TOP_K)
        gates.append(jnp.where(chosen, aff[e], 0.0))
    total = functools.reduce(jnp.add, gates)
    inv = 1.0 / total
    return jnp.concatenate([gt * inv for gt in gates], axis=0), best_group


POST_MIXER_ROWS = 16


def _post_mixer_kernel(*refs, alpha, glu, route):
    it = iter(refs)
    rb_ref = next(it) if route else None
    a_ref, x_ref = next(it), next(it)
    if glu:
        sc1_ref, sh1_ref, dsk_ref, wv_ref, wg_ref = next(it), next(it), next(it), next(it), next(it)
    else:
        wo_ref = next(it)
    g1_ref, lng_ref, lnb_ref = next(it), next(it), next(it)
    if route:
        sc2_ref, sh2_ref, rw_ref = next(it), next(it), next(it)
    x1_ref = next(it)
    if route:
        h2_ref, comb_ref, grp_ref, lo_sc = next(it), next(it), next(it), next(it)

    tm = x_ref.shape[0]
    blocks = [slice(r, r + POST_MIXER_ROWS) for r in range(0, tm, POST_MIXER_ROWS)]
    if glu:
        scale1, shift1, d_skip = 1.0 + sc1_ref[0], sh1_ref[0], dsk_ref[...]
        act = jnp.concatenate([_gelu_tanh(d_skip * (x_ref[b, :] * scale1 + shift1) + a_ref[b, :]).astype(BF16)
                               for b in blocks], axis=0)
        val = jnp.dot(act, wv_ref[...], preferred_element_type=F32)
        gate = jnp.dot(act, wg_ref[...], preferred_element_type=F32)
    else:
        y = jnp.dot(a_ref[...], wo_ref[...], preferred_element_type=F32)
    g1, ln_g, ln_b = g1_ref[0], lng_ref[...], lnb_ref[...]
    if route:
        scale2, shift2 = 1.0 + sc2_ref[0], sh2_ref[0]
    for b in blocks:
        yb = val[b] * _sigmoid(gate[b]) if glu else y[b]
        x1 = _layer_norm_rows(alpha * x_ref[b, :] + g1 * yb, ln_g, ln_b)
        x1_ref[b, :] = x1
        if route:
            h2 = x1 * scale2 + shift2
            hi = h2.astype(BF16)
            h2_ref[b, :] = hi
            lo_sc[b, :] = (h2 - hi.astype(F32)).astype(BF16)
    if route:
        rw = rw_ref[...]
        rw_hi = rw.astype(BF16)
        rw_lo = (rw - rw_hi.astype(F32)).astype(BF16)
        nt = (((1,), (1,)), ((), ()))
        hi_all, lo_all = h2_ref[...], lo_sc[...]
        logits_t = (lax.dot_general(rw_hi, hi_all, nt, preferred_element_type=F32)
                    + lax.dot_general(rw_lo, hi_all, nt, preferred_element_type=F32)
                    + lax.dot_general(rw_hi, lo_all, nt, preferred_element_type=F32))
        comb_ref[...], grp_ref[...] = _router_combine(logits_t, rb_ref)


def _post_mixer(a, x, g1, ln_g, ln_b, alpha, *, w_o=None, glu=None, route=None, tm_target=512):
    n, d = x.shape
    bsz = g1.shape[0]
    per_batch = n // bsz
    tm = _row_tile(per_batch, tm_target)
    tiles_per_batch = per_batch // tm
    row = lambda i: (i, 0)
    batch_vec = lambda i: (i // tiles_per_batch, 0, 0)
    const2 = lambda i: (0, 0)
    vec = pl.BlockSpec((1, 1, d), batch_vec)
    in_specs, args = [], []
    if route is not None:
        in_specs.append(pl.BlockSpec(memory_space=pltpu.SMEM))
        args.append(route[3])
    in_specs += [pl.BlockSpec((tm, d), row), pl.BlockSpec((tm, d), row)]
    args += [a, x]
    if glu is not None:
        sc1, sh1, d_skip, w_val, w_gate = glu
        in_specs += [vec, vec, pl.BlockSpec((1, d), const2),
                     pl.BlockSpec((d, d), const2), pl.BlockSpec((d, d), const2)]
        args += [sc1, sh1, d_skip.reshape(1, d), w_val, w_gate]
    else:
        in_specs.append(pl.BlockSpec((d, d), const2))
        args.append(w_o)
    in_specs += [vec, pl.BlockSpec((1, d), const2), pl.BlockSpec((1, d), const2)]
    args += [g1, ln_g.reshape(1, d), ln_b.reshape(1, d)]
    out_specs = [pl.BlockSpec((tm, d), row)]
    out_shape = [jax.ShapeDtypeStruct((n, d), F32)]
    if route is not None:
        in_specs += [vec, vec, pl.BlockSpec((N_EXPERTS, d), const2)]
        args += [route[0], route[1], route[2]]
        out_specs += [pl.BlockSpec((tm, d), row), pl.BlockSpec((N_EXPERTS, tm), lambda i: (0, i)),
                      pl.BlockSpec((1, tm), lambda i: (0, i))]
        out_shape += [jax.ShapeDtypeStruct((n, d), BF16), jax.ShapeDtypeStruct((N_EXPERTS, n), F32),
                      jax.ShapeDtypeStruct((1, n), jnp.int32)]
    kernel = functools.partial(_post_mixer_kernel, alpha=alpha, glu=glu is not None, route=route is not None)
    return pl.pallas_call(
        kernel,
        grid=(n // tm,),
        in_specs=in_specs,
        out_specs=out_specs,
        out_shape=out_shape,
        scratch_shapes=[pltpu.VMEM((tm, d), BF16)] if route is not None else [],
        compiler_params=_params("parallel"),
        name="post_mixer",
    )(*args)


MOE_BLOCK_ROWS = 256


def _moe_rank_kernel(grp_ref, key_ref, cnt_ref):
    grp = grp_ref[...]
    tm = grp.shape[1]
    member = [grp == g for g in range(N_EXPERT_GROUPS)]
    onehot = jnp.concatenate([m.astype(F32) for m in member]
                             + [jnp.zeros((8 - N_EXPERT_GROUPS, tm), F32)], axis=0)
    earlier = (lax.broadcasted_iota(jnp.int32, (tm, tm), 0)
               < lax.broadcasted_iota(jnp.int32, (tm, tm), 1)).astype(BF16)
    prefix = jnp.dot(onehot.astype(BF16), earlier, preferred_element_type=F32)
    rank = functools.reduce(jnp.add, [jnp.where(member[g], prefix[g:g + 1], 0.0)
                                      for g in range(N_EXPERT_GROUPS)])
    key_ref[...] = (grp * (2 * tm)).astype(F32) + rank
    cnt_ref[0] = jnp.broadcast_to(jnp.sum(onehot, axis=1, keepdims=True), (8, LANES))


def _moe_kernel(cnt_ref, h_ref, keyr_ref, keyc_ref, cg_ref, w1_ref, w3_ref, w2_ref, x_ref, g2_ref, lng_ref, lnb_ref,
                out_ref, acc_sc, *, alpha):
    i, g = pl.program_id(0), pl.program_id(1)
    tm = h_ref.shape[0]
    full, half = MOE_BLOCK_ROWS, MOE_BLOCK_ROWS // 2

    @pl.when(g == 0)
    def _():
        acc_sc[...] = jnp.zeros(acc_sc.shape, F32)

    count = cnt_ref[i * N_EXPERT_GROUPS + g]
    n_full = count // full
    tail = count - n_full * full
    key_row = keyr_ref[...]
    key_col = keyc_ref[...]
    h = h_ref[...]
    first_key = (g * (2 * tm)).astype(F32)

    def run_block(blk, rb):
        base = first_key + (blk * full).astype(F32)
        block_row = lax.broadcasted_iota(jnp.int32, (rb, tm), 0).astype(F32)
        block_col = lax.broadcasted_iota(jnp.int32, (tm, rb), 1).astype(F32)
        onehot = (key_row - base == block_row).astype(F32)
        scatter = (key_col - base == block_col).astype(BF16)
        xs = jnp.dot(onehot.astype(BF16), h, preferred_element_type=F32).astype(BF16)
        z = jnp.zeros((rb, h.shape[1]), F32)
        for e in range(EXPERTS_PER_GROUP):
            weight = jnp.sum(onehot * cg_ref[0, e:e + 1, :], axis=1, keepdims=True)
            a1 = jnp.dot(xs, w1_ref[e], preferred_element_type=F32)
            a3 = jnp.dot(xs, w3_ref[e], preferred_element_type=F32)
            act = (a1 * _sigmoid(a1) * a3 * weight).astype(BF16)
            z = z + jnp.dot(act, w2_ref[e], preferred_element_type=F32)
        acc_sc[...] += jnp.dot(scatter, z.astype(BF16), preferred_element_type=F32)

    def full_block(blk, carry):
        run_block(blk, full)
        return carry

    lax.fori_loop(0, n_full, full_block, 0)

    @pl.when(tail > half)
    def _():
        run_block(n_full, full)

    @pl.when((tail > 0) & (tail <= half))
    def _():
        run_block(n_full, half)

    @pl.when(g == pl.num_programs(1) - 1)
    def _():
        out_ref[...] = _layer_norm_rows(alpha * x_ref[...] + g2_ref[0] * acc_sc[...], lng_ref[...], lnb_ref[...])


def _moe(h2, comb_t, grp, w1, w3, w2, x1, g2, ln_g, ln_b, alpha, tm_target=1024):
    n, d = x1.shape
    bsz = g2.shape[0]
    per_batch = n // bsz
    tm = _row_tile(per_batch, tm_target)
    tiles_per_batch = per_batch // tm
    n_tiles = n // tm
    n_groups = N_EXPERT_GROUPS
    d_expert = w1.shape[2]
    expert_blk = lambda i, g: (g, 0, 0)

    key, cnt = pl.pallas_call(
        _moe_rank_kernel,
        grid=(n_tiles,),
        in_specs=[pl.BlockSpec((1, tm), lambda i: (0, i))],
        out_specs=[pl.BlockSpec((1, tm), lambda i: (0, i)),
                   pl.BlockSpec((1, 8, LANES), lambda i: (i, 0, 0))],
        out_shape=[jax.ShapeDtypeStruct((1, n), F32), jax.ShapeDtypeStruct((n_tiles, 8, LANES), F32)],
        compiler_params=_params("parallel"),
        name="moe_rank",
    )(grp)
    counts = cnt[:, :n_groups, 0].astype(jnp.int32).reshape(-1)
    comb_g = jnp.pad(comb_t.reshape(n_groups, EXPERTS_PER_GROUP, n), ((0, 0), (0, 8 - EXPERTS_PER_GROUP), (0, 0)))

    row = lambda i, g: (i, 0)
    const2 = lambda i, g: (0, 0)
    once = pl.Buffered(1)
    kernel = functools.partial(_moe_kernel, alpha=alpha)
    return pl.pallas_call(
        kernel,
        grid=(n_tiles, n_groups),
        in_specs=[pl.BlockSpec(memory_space=pltpu.SMEM),
                  pl.BlockSpec((tm, d), row),
                  pl.BlockSpec((1, tm), lambda i, g: (0, i)),
                  pl.BlockSpec((tm, 1), row),
                  pl.BlockSpec((1, 8, tm), lambda i, g: (g, 0, i)),
                  pl.BlockSpec((EXPERTS_PER_GROUP, d, d_expert), expert_blk),
                  pl.BlockSpec((EXPERTS_PER_GROUP, d, d_expert), expert_blk),
                  pl.BlockSpec((EXPERTS_PER_GROUP, d_expert, d), expert_blk),
                  pl.BlockSpec((tm, d), row, pipeline_mode=once),
                  pl.BlockSpec((1, 1, d), lambda i, g: (i // tiles_per_batch, 0, 0)),
                  pl.BlockSpec((1, d), const2),
                  pl.BlockSpec((1, d), const2)],
        out_specs=pl.BlockSpec((tm, d), row, pipeline_mode=once),
        out_shape=jax.ShapeDtypeStruct((n, d), F32),
        scratch_shapes=[pltpu.VMEM((tm, d), F32)],
        compiler_params=_params("parallel", "arbitrary"),
        name="moe",
    )(counts, h2, key, key.reshape(n, 1), comb_g, w1, w3, w2, x1, g2, ln_g.reshape(1, d), ln_b.reshape(1, d))


def _modulated_rows(lat_ref, ctx_ref, scale_ref, shift_ref, g, row, class_starts, d_model):
    slabs = []
    for c in range(S5_GROUP_CH):
        channel = g * S5_GROUP_CH + c
        u = jnp.concatenate([lat_ref[c], ctx_ref[c]], axis=0)
        scale = scale_ref[channel]
        shift = shift_ref[channel]
        for k, start in enumerate(class_starts[1:], start=1):
            later = row >= start
            scale = jnp.where(later, scale_ref[k * d_model + channel], scale)
            shift = jnp.where(later, shift_ref[k * d_model + channel], shift)
        slabs.append((u * scale + shift).astype(BF16))
    return jnp.concatenate(slabs, axis=1)


def _s5_inject_kernel(scale_ref, shift_ref, lat_ref, ctx_ref, pin_ref, out_ref, *, class_starts, d_model):
    rows = out_ref.shape[0]
    row = lax.broadcasted_iota(jnp.int32, (rows, SCAN_CHUNK), 0)
    u = _modulated_rows(lat_ref, ctx_ref, scale_ref, shift_ref, pl.program_id(0), row, class_starts, d_model)
    out_ref[...] = jnp.dot(u, pin_ref[0], preferred_element_type=F32)


def _s5_carry_kernel(sin_ref, a_ref, b_ref, h_ref, *, bsz, ctx_chunks, lat_chunks):
    half = 2 * S5_STATE
    ctx0 = bsz * lat_chunks

    def advance(h, row, lo):
        h_ref[row, :, lo:lo + half] = h
        a = a_ref[:, lo:lo + half]
        b = b_ref[:, lo:lo + half]
        return a * h + b * pltpu.roll(h, S5_STATE, 1) + sin_ref[row, :, lo:lo + half]

    zero = jnp.zeros((sin_ref.shape[1], half), F32)
    states = []
    for bi in range(bsz):
        hf, hb = zero, zero
        for n in range(ctx_chunks):
            hf = advance(hf, ctx0 + bi * ctx_chunks + n, 0)
            hb = advance(hb, ctx0 + bi * ctx_chunks + ctx_chunks - 1 - n, half)
        states += [hf, hb]

    def body(n, carry):
        out = []
        for bi in range(bsz):
            base = bi * lat_chunks
            out.append(advance(carry[2 * bi], base + n, 0))
            out.append(advance(carry[2 * bi + 1], base + lat_chunks - 1 - n, half))
        return tuple(out)

    lax.fori_loop(0, lat_chunks, body, tuple(states))


def _s5_output_kernel(scale_ref, shift_ref, lat_ref, ctx_ref, kc_ref, kc_next_ref, h_ref, pout_ref,
                      out_lat_ref, out_ctx_ref, m_even, m_odd, *, class_starts, d_model):
    g = pl.program_id(0)
    rows = h_ref.shape[0]
    lat_rows = lat_ref.shape[1]
    t = SCAN_CHUNK

    def toeplitz_block(lag_ref, cp, c):
        lags = lag_ref[0, cp, c:c + 1, :]
        shifted = pltpu.roll(jnp.broadcast_to(lags, (t, 2 * t)), 0, 1, stride=1, stride_axis=0)
        return shifted[:, t:].astype(BF16)

    @pl.when(g == 0)
    def _():
        def first(cp, carry):
            for c in range(S5_GROUP_CH):
                m_even[pl.ds(pl.multiple_of(cp * t, t), t), c * t:(c + 1) * t] = toeplitz_block(kc_ref, cp, c)
            return carry
        lax.fori_loop(0, S5_GROUP_CH, first, 0)

    row = lax.broadcasted_iota(jnp.int32, (rows, t), 0)
    u = _modulated_rows(lat_ref, ctx_ref, scale_ref, shift_ref, g, row, class_starts, d_model)
    h = h_ref[...].astype(BF16)

    def step(m_now, m_next):
        for cp in range(S5_GROUP_CH):
            for c in range(S5_GROUP_CH):
                m_next[cp * t:(cp + 1) * t, c * t:(c + 1) * t] = toeplitz_block(kc_next_ref, cp, c)
        acc = jnp.dot(u, m_now[...], preferred_element_type=F32)
        for c in range(S5_GROUP_CH):
            y = acc[:, c * t:(c + 1) * t] + jnp.dot(h, pout_ref[0, c], preferred_element_type=F32)
            out_lat_ref[c] = y[:lat_rows]
            out_ctx_ref[c] = y[lat_rows:]

    @pl.when(g % 2 == 0)
    def _():
        step(m_even, m_odd)

    @pl.when(g % 2 == 1)
    def _():
        step(m_odd, m_even)


def _s5_tables(a_re, a_im, log_dt, b_re, b_im, c_re, c_im):
    t = SCAN_CHUNK
    a_re, a_im = a_re.astype(F32), a_im.astype(F32)
    dt = jnp.exp(log_dt.astype(F32))[..., None]
    steps = jnp.arange(t + 1, dtype=F32)[None, None, :, None]
    mag = jnp.exp(steps * (a_re * dt)[:, :, None])
    ang = steps * (a_im * dt)[:, :, None]
    pw_re, pw_im = mag * jnp.cos(ang), mag * jnp.sin(ang)
    lam_re, lam_im = pw_re[:, :, 1], pw_im[:, :, 1]
    inv_den = 1.0 / (a_re * a_re + a_im * a_im)
    n_re = lam_re - 1.0
    f_re = (n_re * a_re + lam_im * a_im) * inv_den
    f_im = (lam_im * a_re - n_re * a_im) * inv_den
    bb_re = f_re[..., None] * b_re - f_im[..., None] * b_im
    bb_im = f_re[..., None] * b_im + f_im[..., None] * b_re

    cp_re, cp_im = c_re.transpose(0, 1, 3, 2)[..., None], c_im.transpose(0, 1, 3, 2)[..., None]
    cb_re = cp_re * bb_re[:, :, :, None] - cp_im * bb_im[:, :, :, None]
    cb_im = cp_re * bb_im[:, :, :, None] + cp_im * bb_re[:, :, :, None]
    resp = (jnp.einsum('dgtp,dgpck->dgkct', pw_re[:, :, :t], cb_re, precision=HIGHEST)
            - jnp.einsum('dgtp,dgpck->dgkct', pw_im[:, :, :t], cb_im, precision=HIGHEST))
    fwd, bwd = resp[0], resp[1]
    lags = jnp.concatenate([jnp.zeros_like(fwd[..., :1]), bwd[..., :0:-1],
                            fwd[..., :1] + bwd[..., :1], fwd[..., 1:]], axis=-1)

    def lanes(f_a, f_b, b_a, b_b):
        return jnp.concatenate([f_a, f_b, b_a, b_b], axis=-1)

    in_re = lanes(pw_re[0, :, t - 1::-1], pw_re[0, :, t - 1::-1], pw_re[1, :, :t], pw_re[1, :, :t])
    in_im = lanes(pw_im[0, :, t - 1::-1], pw_im[0, :, t - 1::-1], pw_im[1, :, :t], pw_im[1, :, :t])
    bt_re, bt_im = bb_re.transpose(0, 1, 3, 2), bb_im.transpose(0, 1, 3, 2)
    with_re = lanes(bt_re[0], bt_im[0], bt_re[1], bt_im[1])
    with_im = lanes(-bt_im[0], bt_re[0], -bt_im[1], bt_re[1])
    p_in = (in_re[:, None] * with_re[:, :, None] + in_im[:, None] * with_im[:, :, None]).astype(BF16)
    p_in = p_in.reshape(p_in.shape[0], -1, 4 * S5_STATE)

    def rows(f_a, f_b, b_a, b_b):
        return jnp.concatenate([f_a, f_b, b_a, b_b], axis=-2)

    fr, fi = pw_re[0, :, 1:].transpose(0, 2, 1), pw_im[0, :, 1:].transpose(0, 2, 1)
    br, bi = pw_re[1, :, :0:-1].transpose(0, 2, 1), pw_im[1, :, :0:-1].transpose(0, 2, 1)
    out_x, out_y = rows(fr, fi, br, bi), rows(fi, fr, bi, br)
    cr, ci = c_re[:, :, :, :, None], c_im[:, :, :, :, None]
    with_x = rows(cr[0], -cr[0], cr[1], -cr[1])
    with_y = rows(-ci[0], -ci[0], -ci[1], -ci[1])
    p_out = (with_x * out_x[:, None] + with_y * out_y[:, None]).astype(BF16)

    carry_a = lanes(pw_re[0, :, t], pw_re[0, :, t], pw_re[1, :, t], pw_re[1, :, t])
    carry_b = lanes(-pw_im[0, :, t], pw_im[0, :, t], -pw_im[1, :, t], pw_im[1, :, t])
    return lags, p_in, p_out, carry_a, carry_b


def _to_chunk_major(rows2d):
    n, d = rows2d.shape
    return rows2d.reshape(n // SCAN_CHUNK, SCAN_CHUNK, d).transpose(2, 0, 1)


def _from_chunk_major(xt):
    d, chunks, t = xt.shape
    return xt.transpose(1, 2, 0).reshape(chunks * t, d)


def _s5_scan(x_lat, x_ctx, scale, shift, tables, bsz):
    lags, p_in, p_out, carry_a, carry_b = tables
    d = x_lat.shape[1]
    groups = d // S5_GROUP_CH
    ctx_chunks = x_ctx.shape[0] // bsz // SCAN_CHUNK
    lat_chunks = x_lat.shape[0] // bsz // SCAN_CHUNK
    lat_rows, ctx_rows = bsz * lat_chunks, bsz * ctx_chunks
    rows = lat_rows + ctx_rows
    xt_lat, xt_ctx = _to_chunk_major(x_lat), _to_chunk_major(x_ctx)
    class_starts = tuple(b * lat_chunks for b in range(bsz)) + (lat_rows,)
    scale_flat, shift_flat = scale.reshape(-1), shift.reshape(-1)
    state_w = 4 * S5_STATE
    smem = pl.BlockSpec(memory_space=pltpu.SMEM)
    slab_lat = pl.BlockSpec((S5_GROUP_CH, lat_rows, SCAN_CHUNK), lambda g: (g, 0, 0))
    slab_ctx = pl.BlockSpec((S5_GROUP_CH, ctx_rows, SCAN_CHUNK), lambda g: (g, 0, 0))

    inject = pl.pallas_call(
        functools.partial(_s5_inject_kernel, class_starts=class_starts, d_model=d),
        grid=(groups,),
        in_specs=[smem, smem, slab_lat, slab_ctx,
                  pl.BlockSpec((1, S5_GROUP_CH * SCAN_CHUNK, state_w), lambda g: (g, 0, 0))],
        out_specs=pl.BlockSpec((rows, state_w), lambda g: (0, g)),
        out_shape=jax.ShapeDtypeStruct((rows, groups * state_w), F32),
        compiler_params=_params("parallel"),
        name="s5_inject",
    )(scale_flat, shift_flat, xt_lat, xt_ctx, p_in)

    gb = 16
    entering = pl.pallas_call(
        functools.partial(_s5_carry_kernel, bsz=bsz, ctx_chunks=ctx_chunks, lat_chunks=lat_chunks),
        grid=(groups // gb,),
        in_specs=[pl.BlockSpec((rows, gb, state_w), lambda g: (0, g, 0)),
                  pl.BlockSpec((gb, state_w), lambda g: (g, 0)),
                  pl.BlockSpec((gb, state_w), lambda g: (g, 0))],
        out_specs=pl.BlockSpec((rows, gb, state_w), lambda g: (0, g, 0)),
        out_shape=jax.ShapeDtypeStruct((rows, groups, state_w), F32),
        compiler_params=_params("parallel"),
        name="s5_carry",
    )(inject.reshape(rows, groups, state_w), carry_a, carry_b)

    yt_lat, yt_ctx = pl.pallas_call(
        functools.partial(_s5_output_kernel, class_starts=class_starts, d_model=d),
        grid=(groups,),
        in_specs=[smem, smem, slab_lat, slab_ctx,
                  pl.BlockSpec((1, S5_GROUP_CH, S5_GROUP_CH, 2 * SCAN_CHUNK), lambda g: (g, 0, 0, 0)),
                  pl.BlockSpec((1, S5_GROUP_CH, S5_GROUP_CH, 2 * SCAN_CHUNK),
                               lambda g: (jnp.minimum(g + 1, groups - 1), 0, 0, 0)),
                  pl.BlockSpec((rows, state_w), lambda g: (0, g)),
                  pl.BlockSpec((1, S5_GROUP_CH, state_w, SCAN_CHUNK), lambda g: (g, 0, 0, 0))],
        out_specs=[slab_lat, slab_ctx],
        out_shape=[jax.ShapeDtypeStruct((d, lat_rows, SCAN_CHUNK), F32),
                   jax.ShapeDtypeStruct((d, ctx_rows, SCAN_CHUNK), F32)],
        scratch_shapes=[pltpu.VMEM((S5_GROUP_CH * SCAN_CHUNK, S5_GROUP_CH * SCAN_CHUNK), BF16)] * 2,
        compiler_params=_params("arbitrary"),
        name="s5_output",
    )(scale_flat, shift_flat, xt_lat, xt_ctx, lags, lags, entering.reshape(rows, groups * state_w), p_out)

    return _from_chunk_major(yt_lat), _from_chunk_major(yt_ctx)


def kernel(x, c, ctx, c_ctx, mod_w, mod_b, ln_g, ln_b, router_w, router_b, moe_w1, moe_w3, moe_w2, s5_a_re, s5_a_im, s5_log_dt, s5_b_re, s5_b_im, s5_c_re, s5_c_im, s5_d, s5_w_gate, s5_w_val, swa_w_qkv, swa_w_o, swa_sink, gqa_w_qkv, gqa_w_o, gqa_q_norm, gqa_k_norm):
    bsz, seq, d = x.shape
    n_ctx = ctx.shape[1]
    depth = mod_w.shape[0]
    alpha = (2.0 * depth) ** 0.25
    heads_b, heads_c = d // HEAD_DIM_B, d // HEAD_DIM_C

    cond = jnp.concatenate([c_ctx[None, :], c], axis=0)
    mods = _modulation(cond, mod_w, mod_b).reshape(depth, bsz + 1, 6, d)
    xl = x.reshape(bsz * seq, d)
    xc = ctx.reshape(bsz * n_ctx, d)
    router_w_t = router_w.T
    rope_b = _rope_tables(seq, HEAD_DIM_B)
    rope_c = _rope_tables(seq, HEAD_DIM_C)

    for i in range(depth):
        kind, j = i % N_MIXERS, i // N_MIXERS
        ctx_out = i < depth - 1
        lat_mod = [mods[i, 1:, k][:, None, :] for k in range(6)]
        ctx_mod = [jnp.broadcast_to(mods[i, 0, k][None, None, :], (bsz, 1, d)) for k in range(6)]
        ln1 = (ln_g[i, 0], ln_b[i, 0])
        ln2 = (ln_g[i, 1], ln_b[i, 1])
        buffers = [(xl, lat_mod, True)] + ([(xc, ctx_mod, False)] if ctx_out else [])

        if kind == 0:
            tables = _s5_tables(s5_a_re[j], s5_a_im[j], s5_log_dt[j], s5_b_re[j], s5_b_im[j],
                                s5_c_re[j], s5_c_im[j])
            ctx_last = jnp.roll(mods[i], -1, axis=0)
            y_lat, y_ctx = _s5_scan(xl, xc, 1.0 + ctx_last[:, 1], ctx_last[:, 0], tables, bsz)
            glu_w = (s5_d[j], s5_w_val[j].astype(BF16), s5_w_gate[j].astype(BF16))
            mixed = {True: y_lat, False: y_ctx}
        else:
            if kind == 1:
                w_qkv, w_o = swa_w_qkv[j].astype(BF16), swa_w_o[j].astype(BF16)
                heads, kv, dh, norms, rope = heads_b, KV_HEADS_B, HEAD_DIM_B, None, rope_b
            else:
                w_qkv, w_o = gqa_w_qkv[j].astype(BF16), gqa_w_o[j].astype(BF16)
                heads, kv, dh, norms, rope = heads_c, KV_HEADS_C, HEAD_DIM_C, (gqa_q_norm[j], gqa_k_norm[j]), rope_c
            ql, kl, vl = _qkv_project(xl, lat_mod[1], lat_mod[0], w_qkv, heads, kv, dh, norms=norms, rope=rope)
            qc, kc, vc = _qkv_project(xc, ctx_mod[1], ctx_mod[0], w_qkv, heads, kv, dh, norms=norms, rope=None)
            if kind == 1:
                sink = swa_sink[j].astype(F32)
                o_lat = _window_attention(ql, kl, vl, kc, vc, sink, bsz, dh)
                o_ctx = _flash_attention(qc, kc, vc, bsz, dh, sink=sink) if ctx_out else None
            else:
                keys = jnp.concatenate([kl.reshape(kv, dh, bsz, seq), kc.reshape(kv, dh, bsz, n_ctx)], axis=3)
                vals = jnp.concatenate([vl.reshape(kv, bsz, seq, dh), vc.reshape(kv, bsz, n_ctx, dh)], axis=2)
                o_lat = _flash_attention(ql, keys.reshape(kv, dh, -1), vals.reshape(kv, -1, dh), bsz, dh)
                o_ctx = _flash_attention(qc, kc, vc, bsz, dh) if ctx_out else None
            mixed = {True: o_lat, False: o_ctx}

        w1, w3, w2 = moe_w1[i].astype(BF16), moe_w3[i].astype(BF16), moe_w2[i].astype(BF16)
        new = []
        for rows, mod, is_lat in buffers:
            sh1, sc1, g1, sh2, sc2, g2 = mod
            route = (sc2, sh2, router_w_t, router_b.astype(F32))
            if kind == 0:
                x1, h2, comb_t, grp = _post_mixer(mixed[is_lat], rows, g1, *ln1, alpha,
                                                  glu=(sc1, sh1) + glu_w, route=route)
            else:
                x1, h2, comb_t, grp = _post_mixer(mixed[is_lat], rows, g1, *ln1, alpha, w_o=w_o, route=route)
            new.append(_moe(h2, comb_t, grp, w1, w3, w2, x1, g2, *ln2, alpha))
        xl = new[0]
        if ctx_out:
            xc = new[1]
    return xl.reshape(bsz, seq, d)
```

```python
import functools
import math

import jax
import jax.numpy as jnp
from jax import lax
from jax.experimental import pallas as pl
from jax.experimental.pallas import tpu as pltpu

GRID_W = 64
N_MIXERS = 3
LN_EPS = 1e-6
RMS_EPS = 1e-6
NEG_INF = -1e30
ROPE_THETA = 10000.0
S5_GROUP_CH = 16
S5_STATE = 64
SCAN_CHUNK = 128
WINDOW = 128
HEAD_DIM_B = 64
KV_HEADS_B = 2
HEAD_DIM_C = 128
KV_HEADS_C = 2
N_EXPERTS = 16
N_EXPERT_GROUPS = 4
EXPERTS_PER_GROUP = N_EXPERTS // N_EXPERT_GROUPS
TOP_K = 2

LANES = 128
LOG2E = math.log2(math.e)
VMEM_LIMIT_BYTES = 56 * 1024 * 1024

F32 = jnp.float32
BF16 = jnp.bfloat16
HIGHEST = lax.Precision.HIGHEST


def _params(*semantics):
    return pltpu.CompilerParams(dimension_semantics=semantics,
                                vmem_limit_bytes=VMEM_LIMIT_BYTES)


def _row_tile(n_rows, target):
    t = min(target, n_rows)
    while n_rows % t:
        t //= 2
    return t


def _layer_norm_rows(v, g, b):
    vc = v - jnp.mean(v, axis=-1, keepdims=True)
    var = jnp.mean(vc * vc, axis=-1, keepdims=True)
    return vc * lax.rsqrt(var + LN_EPS) * g + b


def _sigmoid(v):
    return 1.0 / (1.0 + jnp.exp(-v))


def _gelu_tanh(v):
    return 0.5 * v * (1.0 + jnp.tanh(math.sqrt(2.0 / math.pi) * (v + 0.044715 * (v * v * v))))


def _modulation_kernel(cond_t_ref, w_ref, b_ref, out_ref):
    cond = cond_t_ref[...]
    act = cond * _sigmoid(cond)
    w = w_ref[0]
    rows = [jnp.sum(w * act[:, r:r + 1], axis=0, keepdims=True) for r in range(act.shape[1])]
    out_ref[0] = jnp.concatenate(rows, axis=0) + b_ref[0]


def _modulation(cond, mod_w, mod_b):
    depth, d, n_out = mod_w.shape
    rows = cond.shape[0]
    tn = 1536
    return pl.pallas_call(
        _modulation_kernel,
        grid=(depth, n_out // tn),
        in_specs=[pl.BlockSpec((d, rows), lambda i, j: (0, 0)),
                  pl.BlockSpec((1, d, tn), lambda i, j: (i, 0, j)),
                  pl.BlockSpec((1, 1, tn), lambda i, j: (i, 0, j))],
        out_specs=pl.BlockSpec((1, rows, tn), lambda i, j: (i, 0, j)),
        out_shape=jax.ShapeDtypeStruct((depth, rows, n_out), F32),
        compiler_params=_params("parallel", "parallel"),
        name="modulation",
    )(cond.T, mod_w, mod_b.reshape(depth, 1, n_out))


def _swap_rotary_pairs(v, half):
    width = v.shape[-1]
    lane = lax.broadcasted_iota(jnp.int32, v.shape, v.ndim - 1)
    first = (lane // half) % 2 == 0
    return jnp.where(first, pltpu.roll(v, width - half, v.ndim - 1), pltpu.roll(v, half, v.ndim - 1))


def _qkv_kernel(*refs, n_heads, n_kv, head_dim, qk_norm, use_rope):
    it = iter(refs)
    x_ref, sc_ref, sh_ref, w_ref = next(it), next(it), next(it), next(it)
    if qk_norm:
        qn_ref, kn_ref = next(it), next(it)
    if use_rope:
        cos_ref, sin_ref = next(it), next(it)
    q_ref, kt_ref, v_ref = next(it), next(it), next(it)

    h = (x_ref[...] * (1.0 + sc_ref[0]) + sh_ref[0]).astype(BF16)
    proj = jnp.dot(h, w_ref[...], preferred_element_type=F32)
    q_width, kv_width = n_heads * head_dim, n_kv * head_dim
    q = proj[:, :q_width]
    k = proj[:, q_width:q_width + kv_width]
    v = proj[:, q_width + kv_width:]

    def rms_heads(t, gain, count):
        parts = []
        for hd in range(count):
            th = t[:, hd * head_dim:(hd + 1) * head_dim]
            ms = jnp.mean(th * th, axis=-1, keepdims=True)
            parts.append(th * lax.rsqrt(ms + RMS_EPS) * gain)
        return jnp.concatenate(parts, axis=-1)

    if qk_norm:
        q = rms_heads(q, qn_ref[...], n_heads)
        k = rms_heads(k, kn_ref[...], n_kv)

    if use_rope:
        cos, sin = cos_ref[...], sin_ref[...]

        def rope(t):
            reps = t.shape[-1] // LANES
            c = jnp.concatenate([cos] * reps, axis=-1) if reps > 1 else cos
            s = jnp.concatenate([sin] * reps, axis=-1) if reps > 1 else sin
            return t * c + _swap_rotary_pairs(t, head_dim // 4) * s

        q, k = rope(q), rope(k)

    q = q * (head_dim ** -0.5 * LOG2E)
    k_t = k.T
    group = q_width // n_kv
    for g in range(n_kv):
        q_ref[g] = q[:, g * group:(g + 1) * group].astype(BF16)
        kt_ref[g] = k_t[g * head_dim:(g + 1) * head_dim, :].astype(BF16)
        v_ref[g] = v[:, g * head_dim:(g + 1) * head_dim].astype(BF16)


def _qkv_project(x, sc, sh, w, n_heads, n_kv, head_dim, norms=None, rope=None, tm_target=512):
    n, d = x.shape
    bsz = sc.shape[0]
    per_batch = n // bsz
    tm = _row_tile(per_batch, tm_target)
    tiles_per_batch = per_batch // tm
    n_out = w.shape[1]
    group = n_heads * head_dim // n_kv
    row = lambda i: (i, 0)
    batch_vec = lambda i: (i // tiles_per_batch, 0, 0)
    const2 = lambda i: (0, 0)
    in_specs = [pl.BlockSpec((tm, d), row),
                pl.BlockSpec((1, 1, d), batch_vec),
                pl.BlockSpec((1, 1, d), batch_vec),
                pl.BlockSpec((d, n_out), const2)]
    args = [x, sc, sh, w]
    if norms is not None:
        in_specs += [pl.BlockSpec((1, head_dim), const2)] * 2
        args += [norms[0].reshape(1, head_dim), norms[1].reshape(1, head_dim)]
    if rope is not None:
        in_specs += [pl.BlockSpec((tm, LANES), lambda i: (i % tiles_per_batch, 0))] * 2
        args += [rope[0], rope[1]]
    kernel = functools.partial(_qkv_kernel, n_heads=n_heads, n_kv=n_kv, head_dim=head_dim,
                               qk_norm=norms is not None, use_rope=rope is not None)
    return pl.pallas_call(
        kernel,
        grid=(n // tm,),
        in_specs=in_specs,
        out_specs=[pl.BlockSpec((n_kv, tm, group), lambda i: (0, i, 0)),
                   pl.BlockSpec((n_kv, head_dim, tm), lambda i: (0, 0, i)),
                   pl.BlockSpec((n_kv, tm, head_dim), lambda i: (0, i, 0))],
        out_shape=[jax.ShapeDtypeStruct((n_kv, n, group), BF16),
                   jax.ShapeDtypeStruct((n_kv, head_dim, n), BF16),
                   jax.ShapeDtypeStruct((n_kv, n, head_dim), BF16)],
        compiler_params=_params("parallel"),
        name="qkv_project",
    )(*args)


def _rope_tables(seq, head_dim):
    quarter = head_dim // 4
    inv_freq = ROPE_THETA ** (-jnp.arange(quarter, dtype=F32) / quarter)
    t = jnp.arange(seq)
    rows = (t // GRID_W).astype(F32)
    cols = (t % GRID_W).astype(F32)
    ang_r = rows[:, None] * inv_freq
    ang_c = cols[:, None] * inv_freq
    cos = jnp.concatenate([jnp.cos(ang_r)] * 2 + [jnp.cos(ang_c)] * 2, axis=-1)
    sin = jnp.concatenate([-jnp.sin(ang_r), jnp.sin(ang_r), -jnp.sin(ang_c), jnp.sin(ang_c)], axis=-1)
    reps = LANES // head_dim
    return jnp.tile(cos, (1, reps)), jnp.tile(sin, (1, reps))


def _stack_heads(q, n_rep, head_dim):
    if n_rep == 1:
        return q
    return jnp.concatenate([q[:, r * head_dim:(r + 1) * head_dim] for r in range(n_rep)], axis=0)


def _unstack_heads(o, n_rep, rows):
    if n_rep == 1:
        return o
    return jnp.concatenate([o[r * rows:(r + 1) * rows] for r in range(n_rep)], axis=-1)


def _sink_column(sink_ref, g, n_rep, rows):
    return jnp.concatenate([jnp.full((rows, 1), sink_ref[g * n_rep + r] * LOG2E, F32) for r in range(n_rep)],
                           axis=0)


def _ones_width(head_dim):
    return LANES - head_dim % LANES


def _with_ones(v, head_dim):
    return jnp.concatenate([v, jnp.ones((v.shape[0], _ones_width(head_dim)), v.dtype)], axis=1)


def _lane_repeat(col_block, width):
    reps = width // col_block.shape[1]
    return col_block if reps == 1 else jnp.concatenate([col_block] * reps, axis=1)


def _flash_kernel(*refs, n_rep, head_dim, has_sink, n_split):
    if has_sink:
        sink_ref, q_ref, kt_ref, v_ref, o_ref, m_sc, acc_sc = refs
    else:
        q_ref, kt_ref, v_ref, o_ref, m_sc, acc_sc = refs
    kv_head, kj = pl.program_id(1), pl.program_id(3)
    tq = q_ref.shape[1]
    width = acc_sc.shape[1]

    @pl.when(kj == 0)
    def _():
        m_sc[...] = jnp.full(m_sc.shape, NEG_INF, F32)
        acc_sc[...] = jnp.zeros(acc_sc.shape, F32)

    m_all = m_sc[...]
    acc_all = acc_sc[...]
    q = _stack_heads(q_ref[0], n_rep, head_dim)
    kt = kt_ref[0]
    v1 = _with_ones(v_ref[0], head_dim)
    part = n_rep * tq // n_split
    m_out, acc_out = [], []
    for h in range(n_split):
        rs = slice(h * part, (h + 1) * part)
        s = jnp.dot(q[rs], kt, preferred_element_type=F32)
        m_prev = m_all[rs]
        m_new = jnp.maximum(m_prev, jnp.max(s, axis=-1, keepdims=True))
        p = jnp.exp2(s - _lane_repeat(m_new, s.shape[1]))
        alpha = jnp.exp2(m_prev - m_new)
        acc_out.append(_lane_repeat(alpha, width) * acc_all[rs]
                       + jnp.dot(p.astype(BF16), v1, preferred_element_type=F32))
        m_out.append(m_new)
    m_sc[...] = jnp.concatenate(m_out, axis=0)
    acc_sc[...] = jnp.concatenate(acc_out, axis=0)

    @pl.when(kj == pl.num_programs(3) - 1)
    def _():
        acc = acc_sc[...]
        num, l = acc[:, :head_dim], acc[:, head_dim:head_dim + 1]
        if has_sink:
            m = m_sc[:, :1]
            sink = _sink_column(sink_ref, kv_head, n_rep, tq)
            m_fin = jnp.maximum(m, sink)
            scale = jnp.exp2(m - m_fin)
            l = l * scale + jnp.exp2(sink - m_fin)
            num = num * scale
        o_ref[...] = _unstack_heads(num * (1.0 / l), n_rep, tq).astype(o_ref.dtype)


def _flash_attention(q, kt, v, bsz, head_dim, sink=None, tq_target=1024, tk_target=3328, n_split=32):
    n_kv, nq, group = q.shape
    n_rep = group // head_dim
    lq, lk = nq // bsz, v.shape[1] // bsz
    tq = _row_tile(lq, tq_target)
    tk = tk_target if lk % tk_target == 0 else _row_tile(lk, 256)
    qt, kt_tiles = lq // tq, lk // tk
    kernel = functools.partial(_flash_kernel, n_rep=n_rep, head_dim=head_dim, has_sink=sink is not None,
                               n_split=n_split)
    in_specs = [pl.BlockSpec((1, tq, group), lambda b, g, i, j: (g, b * qt + i, 0)),
                pl.BlockSpec((1, head_dim, tk), lambda b, g, i, j: (g, 0, b * kt_tiles + j)),
                pl.BlockSpec((1, tk, head_dim), lambda b, g, i, j: (g, b * kt_tiles + j, 0))]
    args = [q, kt, v]
    if sink is not None:
        in_specs = [pl.BlockSpec(memory_space=pltpu.SMEM)] + in_specs
        args = [sink] + args
    return pl.pallas_call(
        kernel,
        grid=(bsz, n_kv, qt, kt_tiles),
        in_specs=in_specs,
        out_specs=pl.BlockSpec((tq, group), lambda b, g, i, j: (b * qt + i, g)),
        out_shape=jax.ShapeDtypeStruct((nq, n_kv * group), BF16),
        scratch_shapes=[pltpu.VMEM((n_rep * tq, LANES), F32),
                        pltpu.VMEM((n_rep * tq, head_dim + _ones_width(head_dim)), F32)],
        compiler_params=_params("parallel", "parallel", "parallel", "arbitrary"),
        name="flash_attention",
    )(*args)


def _window_kernel(sink_ref, q_ref, kp_ref, kc_ref, kn_ref, kx_ref, vp_ref, vc_ref, vn_ref, vx_ref, o_ref,
                   *, n_rep, head_dim, seq):
    kv_head, i = pl.program_id(1), pl.program_id(2)
    tq = q_ref.shape[1]
    band = tq + 2 * WINDOW
    kt = jnp.concatenate([kp_ref[0], kc_ref[0], kn_ref[0], kx_ref[0]], axis=1)
    v1 = _with_ones(jnp.concatenate([vp_ref[0], vc_ref[0], vn_ref[0], vx_ref[0]], axis=0), head_dim)
    n_keys = kt.shape[1]

    qi = lax.broadcasted_iota(jnp.int32, (tq, n_keys), 0)
    kj = lax.broadcasted_iota(jnp.int32, (tq, n_keys), 1)
    key_pos = i * tq - WINDOW + kj
    ok = (kj >= band) | ((jnp.abs(kj - WINDOW - qi) <= WINDOW) & (key_pos >= 0) & (key_pos < seq))

    q = q_ref[0]
    outs = []
    for r in range(n_rep):
        s = jnp.dot(q[:, r * head_dim:(r + 1) * head_dim], kt, preferred_element_type=F32)
        s = jnp.where(ok, s, NEG_INF)
        sink = sink_ref[kv_head * n_rep + r] * LOG2E
        m = jnp.maximum(jnp.max(s, axis=-1, keepdims=True), sink)
        p = jnp.exp2(s - m)
        pv = jnp.dot(p.astype(BF16), v1, preferred_element_type=F32)
        denom = pv[:, head_dim:head_dim + 1] + jnp.exp2(sink - m)
        outs.append(pv[:, :head_dim] * (1.0 / denom))
    o_ref[...] = jnp.concatenate(outs, axis=1).astype(o_ref.dtype)


def _window_attention(q, kt, v, kt_ctx, v_ctx, sink, bsz, head_dim, tq_target=512):
    n_kv, nq, group = q.shape
    n_rep = group // head_dim
    seq = nq // bsz
    n_ctx = v_ctx.shape[1] // bsz
    tq = _row_tile(seq, tq_target)
    qt = seq // tq
    ratio = tq // WINDOW
    halo_blocks = seq // WINDOW
    prev_blk = lambda b, i: b * halo_blocks + jnp.maximum(i * ratio - 1, 0)
    next_blk = lambda b, i: b * halo_blocks + jnp.minimum((i + 1) * ratio, halo_blocks - 1)
    q_map = lambda b, g, i: (g, b * qt + i, 0)
    k_specs = [pl.BlockSpec((1, head_dim, WINDOW), lambda b, g, i: (g, 0, prev_blk(b, i))),
               pl.BlockSpec((1, head_dim, tq), lambda b, g, i: (g, 0, b * qt + i)),
               pl.BlockSpec((1, head_dim, WINDOW), lambda b, g, i: (g, 0, next_blk(b, i))),
               pl.BlockSpec((1, head_dim, n_ctx), lambda b, g, i: (g, 0, b))]
    v_specs = [pl.BlockSpec((1, WINDOW, head_dim), lambda b, g, i: (g, prev_blk(b, i), 0)),
               pl.BlockSpec((1, tq, head_dim), q_map),
               pl.BlockSpec((1, WINDOW, head_dim), lambda b, g, i: (g, next_blk(b, i), 0)),
               pl.BlockSpec((1, n_ctx, head_dim), lambda b, g, i: (g, b, 0))]
    kernel = functools.partial(_window_kernel, n_rep=n_rep, head_dim=head_dim, seq=seq)
    return pl.pallas_call(
        kernel,
        grid=(bsz, n_kv, qt),
        in_specs=[pl.BlockSpec(memory_space=pltpu.SMEM), pl.BlockSpec((1, tq, group), q_map)] + k_specs + v_specs,
        out_specs=pl.BlockSpec((tq, group), lambda b, g, i: (b * qt + i, g)),
        out_shape=jax.ShapeDtypeStruct((nq, n_kv * group), BF16),
        compiler_params=_params("parallel", "parallel", "parallel"),
        name="window_attention",
    )(sink, q, kt, kt, kt, kt_ctx, v, v, v, v_ctx)


def _router_combine(logits_t, bias_ref):
    aff = [_sigmoid(logits_t[e:e + 1, :]) for e in range(N_EXPERTS)]
    sel = [aff[e] + bias_ref[e] for e in range(N_EXPERTS)]
    best_score, best_group = None, None
    for g in range(N_EXPERT_GROUPS):
        a, b, c, d = sel[g * EXPERTS_PER_GROUP:(g + 1) * EXPERTS_PER_GROUP]
        hi1, lo1 = jnp.maximum(a, b), jnp.minimum(a, b)
        hi2, lo2 = jnp.maximum(c, d), jnp.minimum(c, d)
        score = jnp.maximum(hi1, hi2) + jnp.maximum(jnp.minimum(hi1, hi2), jnp.maximum(lo1, lo2))
        if g == 0:
            best_score, best_group = score, jnp.zeros(score.shape, jnp.int32)
        else:
            better = score > best_score
            best_score = jnp.where(better, score, best_score)
            best_group = jnp.where(better, g, best_group)
    gates = []
    for e in range(N_EXPERTS):
        g = e // EXPERTS_PER_GROUP
        beaten = jnp.zeros(best_group.shape, jnp.int32)
        for o in range(g * EXPERTS_PER_GROUP, (g + 1) * EXPERTS_PER_GROUP):
            if o == e:
                continue
            wins = (sel[o] > sel[e]) | ((sel[o] == sel[e]) & (o < e))
            beaten = beaten + wins.astype(jnp.int32)
        chosen = (best_group == g) & (beaten < TOP_K)
        gates.append(jnp.where(chosen, aff[e], 0.0))
    total = functools.reduce(jnp.add, gates)
    inv = 1.0 / total
    return jnp.concatenate([gt * inv for gt in gates], axis=0), best_group


POST_MIXER_ROWS = 16


def _post_mixer_kernel(*refs, alpha, glu, route):
    it = iter(refs)
    rb_ref = next(it) if route else None
    a_ref, x_ref = next(it), next(it)
    if glu:
        sc1_ref, sh1_ref, dsk_ref, wv_ref, wg_ref = next(it), next(it), next(it), next(it), next(it)
    else:
        wo_ref = next(it)
    g1_ref, lng_ref, lnb_ref = next(it), next(it), next(it)
    if route:
        sc2_ref, sh2_ref, rw_ref = next(it), next(it), next(it)
    x1_ref = next(it)
    if route:
        h2_ref, comb_ref, grp_ref, lo_sc = next(it), next(it), next(it), next(it)

    tm = x_ref.shape[0]
    blocks = [slice(r, r + POST_MIXER_ROWS) for r in range(0, tm, POST_MIXER_ROWS)]
    if glu:
        scale1, shift1, d_skip = 1.0 + sc1_ref[0], sh1_ref[0], dsk_ref[...]
        act = jnp.concatenate([_gelu_tanh(d_skip * (x_ref[b, :] * scale1 + shift1) + a_ref[b, :]).astype(BF16)
                               for b in blocks], axis=0)
        val = jnp.dot(act, wv_ref[...], preferred_element_type=F32)
        gate = jnp.dot(act, wg_ref[...], preferred_element_type=F32)
    else:
        y = jnp.dot(a_ref[...], wo_ref[...], preferred_element_type=F32)
    g1, ln_g, ln_b = g1_ref[0], lng_ref[...], lnb_ref[...]
    if route:
        scale2, shift2 = 1.0 + sc2_ref[0], sh2_ref[0]
    for b in blocks:
        yb = val[b] * _sigmoid(gate[b]) if glu else y[b]
        x1 = _layer_norm_rows(alpha * x_ref[b, :] + g1 * yb, ln_g, ln_b)
        x1_ref[b, :] = x1
        if route:
            h2 = x1 * scale2 + shift2
            hi = h2.astype(BF16)
            h2_ref[b, :] = hi
            lo_sc[b, :] = (h2 - hi.astype(F32)).astype(BF16)
    if route:
        rw = rw_ref[...]
        rw_hi = rw.astype(BF16)
        rw_lo = (rw - rw_hi.astype(F32)).astype(BF16)
        nt = (((1,), (1,)), ((), ()))
        hi_all, lo_all = h2_ref[...], lo_sc[...]
        logits_t = (lax.dot_general(rw_hi, hi_all, nt, preferred_element_type=F32)
                    + lax.dot_general(rw_lo, hi_all, nt, preferred_element_type=F32)
                    + lax.dot_general(rw_hi, lo_all, nt, preferred_element_type=F32))
        comb_ref[...], grp_ref[...] = _router_combine(logits_t, rb_ref)


def _post_mixer(a, x, g1, ln_g, ln_b, alpha, *, w_o=None, glu=None, route=None, tm_target=512):
    n, d = x.shape
    bsz = g1.shape[0]
    per_batch = n // bsz
    tm = _row_tile(per_batch, tm_target)
    tiles_per_batch = per_batch // tm
    row = lambda i: (i, 0)
    batch_vec = lambda i: (i // tiles_per_batch, 0, 0)
    const2 = lambda i: (0, 0)
    vec = pl.BlockSpec((1, 1, d), batch_vec)
    in_specs, args = [], []
    if route is not None:
        in_specs.append(pl.BlockSpec(memory_space=pltpu.SMEM))
        args.append(route[3])
    in_specs += [pl.BlockSpec((tm, d), row), pl.BlockSpec((tm, d), row)]
    args += [a, x]
    if glu is not None:
        sc1, sh1, d_skip, w_val, w_gate = glu
        in_specs += [vec, vec, pl.BlockSpec((1, d), const2),
                     pl.BlockSpec((d, d), const2), pl.BlockSpec((d, d), const2)]
        args += [sc1, sh1, d_skip.reshape(1, d), w_val, w_gate]
    else:
        in_specs.append(pl.BlockSpec((d, d), const2))
        args.append(w_o)
    in_specs += [vec, pl.BlockSpec((1, d), const2), pl.BlockSpec((1, d), const2)]
    args += [g1, ln_g.reshape(1, d), ln_b.reshape(1, d)]
    out_specs = [pl.BlockSpec((tm, d), row)]
    out_shape = [jax.ShapeDtypeStruct((n, d), F32)]
    if route is not None:
        in_specs += [vec, vec, pl.BlockSpec((N_EXPERTS, d), const2)]
        args += [route[0], route[1], route[2]]
        out_specs += [pl.BlockSpec((tm, d), row), pl.BlockSpec((N_EXPERTS, tm), lambda i: (0, i)),
                      pl.BlockSpec((1, tm), lambda i: (0, i))]
        out_shape += [jax.ShapeDtypeStruct((n, d), BF16), jax.ShapeDtypeStruct((N_EXPERTS, n), F32),
                      jax.ShapeDtypeStruct((1, n), jnp.int32)]
    kernel = functools.partial(_post_mixer_kernel, alpha=alpha, glu=glu is not None, route=route is not None)
    return pl.pallas_call(
        kernel,
        grid=(n // tm,),
        in_specs=in_specs,
        out_specs=out_specs,
        out_shape=out_shape,
        scratch_shapes=[pltpu.VMEM((tm, d), BF16)] if route is not None else [],
        compiler_params=_params("parallel"),
        name="post_mixer",
    )(*args)


MOE_BLOCK_ROWS = 256


def _moe_rank_kernel(grp_ref, key_ref, cnt_ref):
    grp = grp_ref[...]
    tm = grp.shape[1]
    member = [grp == g for g in range(N_EXPERT_GROUPS)]
    onehot = jnp.concatenate([m.astype(F32) for m in member]
                             + [jnp.zeros((8 - N_EXPERT_GROUPS, tm), F32)], axis=0)
    earlier = (lax.broadcasted_iota(jnp.int32, (tm, tm), 0)
               < lax.broadcasted_iota(jnp.int32, (tm, tm), 1)).astype(BF16)
    prefix = jnp.dot(onehot.astype(BF16), earlier, preferred_element_type=F32)
    rank = functools.reduce(jnp.add, [jnp.where(member[g], prefix[g:g + 1], 0.0)
                                      for g in range(N_EXPERT_GROUPS)])
    key_ref[...] = (grp * (2 * tm)).astype(F32) + rank
    cnt_ref[0] = jnp.broadcast_to(jnp.sum(onehot, axis=1, keepdims=True), (8, LANES))


def _moe_kernel(cnt_ref, h_ref, keyr_ref, keyc_ref, cg_ref, w1_ref, w3_ref, w2_ref, x_ref, g2_ref, lng_ref, lnb_ref,
                out_ref, acc_sc, *, alpha):
    i, g = pl.program_id(0), pl.program_id(1)
    tm = h_ref.shape[0]
    full, half = MOE_BLOCK_ROWS, MOE_BLOCK_ROWS // 2

    @pl.when(g == 0)
    def _():
        acc_sc[...] = jnp.zeros(acc_sc.shape, F32)

    count = cnt_ref[i * N_EXPERT_GROUPS + g]
    n_full = count // full
    tail = count - n_full * full
    key_row = keyr_ref[...]
    key_col = keyc_ref[...]
    h = h_ref[...]
    first_key = (g * (2 * tm)).astype(F32)

    def run_block(blk, rb):
        base = first_key + (blk * full).astype(F32)
        block_row = lax.broadcasted_iota(jnp.int32, (rb, tm), 0).astype(F32)
        block_col = lax.broadcasted_iota(jnp.int32, (tm, rb), 1).astype(F32)
        onehot = (key_row - base == block_row).astype(F32)
        scatter = (key_col - base == block_col).astype(BF16)
        xs = jnp.dot(onehot.astype(BF16), h, preferred_element_type=F32).astype(BF16)
        z = jnp.zeros((rb, h.shape[1]), F32)
        for e in range(EXPERTS_PER_GROUP):
            weight = jnp.sum(onehot * cg_ref[0, e:e + 1, :], axis=1, keepdims=True)
            a1 = jnp.dot(xs, w1_ref[e], preferred_element_type=F32)
            a3 = jnp.dot(xs, w3_ref[e], preferred_element_type=F32)
            act = (a1 * _sigmoid(a1) * a3 * weight).astype(BF16)
            z = z + jnp.dot(act, w2_ref[e], preferred_element_type=F32)
        acc_sc[...] += jnp.dot(scatter, z.astype(BF16), preferred_element_type=F32)

    def full_block(blk, carry):
        run_block(blk, full)
        return carry

    lax.fori_loop(0, n_full, full_block, 0)

    @pl.when(tail > half)
    def _():
        run_block(n_full, full)

    @pl.when((tail > 0) & (tail <= half))
    def _():
        run_block(n_full, half)

    @pl.when(g == pl.num_programs(1) - 1)
    def _():
        out_ref[...] = _layer_norm_rows(alpha * x_ref[...] + g2_ref[0] * acc_sc[...], lng_ref[...], lnb_ref[...])


def _moe(h2, comb_t, grp, w1, w3, w2, x1, g2, ln_g, ln_b, alpha, tm_target=1024):
    n, d = x1.shape
    bsz = g2.shape[0]
    per_batch = n // bsz
    tm = _row_tile(per_batch, tm_target)
    tiles_per_batch = per_batch // tm
    n_tiles = n // tm
    n_groups = N_EXPERT_GROUPS
    d_expert = w1.shape[2]
    expert_blk = lambda i, g: (g, 0, 0)

    key, cnt = pl.pallas_call(
        _moe_rank_kernel,
        grid=(n_tiles,),
        in_specs=[pl.BlockSpec((1, tm), lambda i: (0, i))],
        out_specs=[pl.BlockSpec((1, tm), lambda i: (0, i)),
                   pl.BlockSpec((1, 8, LANES), lambda i: (i, 0, 0))],
        out_shape=[jax.ShapeDtypeStruct((1, n), F32), jax.ShapeDtypeStruct((n_tiles, 8, LANES), F32)],
        compiler_params=_params("parallel"),
        name="moe_rank",
    )(grp)
    counts = cnt[:, :n_groups, 0].astype(jnp.int32).reshape(-1)
    comb_g = jnp.pad(comb_t.reshape(n_groups, EXPERTS_PER_GROUP, n), ((0, 0), (0, 8 - EXPERTS_PER_GROUP), (0, 0)))

    row = lambda i, g: (i, 0)
    const2 = lambda i, g: (0, 0)
    once = pl.Buffered(1)
    kernel = functools.partial(_moe_kernel, alpha=alpha)
    return pl.pallas_call(
        kernel,
        grid=(n_tiles, n_groups),
        in_specs=[pl.BlockSpec(memory_space=pltpu.SMEM),
                  pl.BlockSpec((tm, d), row),
                  pl.BlockSpec((1, tm), lambda i, g: (0, i)),
                  pl.BlockSpec((tm, 1), row),
                  pl.BlockSpec((1, 8, tm), lambda i, g: (g, 0, i)),
                  pl.BlockSpec((EXPERTS_PER_GROUP, d, d_expert), expert_blk),
                  pl.BlockSpec((EXPERTS_PER_GROUP, d, d_expert), expert_blk),
                  pl.BlockSpec((EXPERTS_PER_GROUP, d_expert, d), expert_blk),
                  pl.BlockSpec((tm, d), row, pipeline_mode=once),
                  pl.BlockSpec((1, 1, d), lambda i, g: (i // tiles_per_batch, 0, 0)),
                  pl.BlockSpec((1, d), const2),
                  pl.BlockSpec((1, d), const2)],
        out_specs=pl.BlockSpec((tm, d), row, pipeline_mode=once),
        out_shape=jax.ShapeDtypeStruct((n, d), F32),
        scratch_shapes=[pltpu.VMEM((tm, d), F32)],
        compiler_params=_params("parallel", "arbitrary"),
        name="moe",
    )(counts, h2, key, key.reshape(n, 1), comb_g, w1, w3, w2, x1, g2, ln_g.reshape(1, d), ln_b.reshape(1, d))


def _modulated_rows(lat_ref, ctx_ref, scale_ref, shift_ref, g, row, class_starts, d_model):
    slabs = []
    for c in range(S5_GROUP_CH):
        channel = g * S5_GROUP_CH + c
        u = jnp.concatenate([lat_ref[c], ctx_ref[c]], axis=0)
        scale = scale_ref[channel]
        shift = shift_ref[channel]
        for k, start in enumerate(class_starts[1:], start=1):
            later = row >= start
            scale = jnp.where(later, scale_ref[k * d_model + channel], scale)
            shift = jnp.where(later, shift_ref[k * d_model + channel], shift)
        slabs.append((u * scale + shift).astype(BF16))
    return jnp.concatenate(slabs, axis=1)


def _s5_inject_kernel(scale_ref, shift_ref, lat_ref, ctx_ref, pin_ref, out_ref, *, class_starts, d_model):
    rows = out_ref.shape[0]
    row = lax.broadcasted_iota(jnp.int32, (rows, SCAN_CHUNK), 0)
    u = _modulated_rows(lat_ref, ctx_ref, scale_ref, shift_ref, pl.program_id(0), row, class_starts, d_model)
    out_ref[...] = jnp.dot(u, pin_ref[0], preferred_element_type=F32)


def _s5_carry_kernel(sin_ref, a_ref, b_ref, h_ref, *, bsz, ctx_chunks, lat_chunks):
    half = 2 * S5_STATE
    ctx0 = bsz * lat_chunks

    def advance(h, row, lo):
        h_ref[row, :, lo:lo + half] = h
        a = a_ref[:, lo:lo + half]
        b = b_ref[:, lo:lo + half]
        return a * h + b * pltpu.roll(h, S5_STATE, 1) + sin_ref[row, :, lo:lo + half]

    zero = jnp.zeros((sin_ref.shape[1], half), F32)
    states = []
    for bi in range(bsz):
        hf, hb = zero, zero
        for n in range(ctx_chunks):
            hf = advance(hf, ctx0 + bi * ctx_chunks + n, 0)
            hb = advance(hb, ctx0 + bi * ctx_chunks + ctx_chunks - 1 - n, half)
        states += [hf, hb]

    def body(n, carry):
        out = []
        for bi in range(bsz):
            base = bi * lat_chunks
            out.append(advance(carry[2 * bi], base + n, 0))
            out.append(advance(carry[2 * bi + 1], base + lat_chunks - 1 - n, half))
        return tuple(out)

    lax.fori_loop(0, lat_chunks, body, tuple(states))


def _s5_output_kernel(scale_ref, shift_ref, lat_ref, ctx_ref, kc_ref, kc_next_ref, h_ref, pout_ref,
                      out_lat_ref, out_ctx_ref, m_even, m_odd, *, class_starts, d_model):
    g = pl.program_id(0)
    rows = h_ref.shape[0]
    lat_rows = lat_ref.shape[1]
    t = SCAN_CHUNK

    def toeplitz_block(lag_ref, cp, c):
        lags = lag_ref[0, cp, c:c + 1, :]
        shifted = pltpu.roll(jnp.broadcast_to(lags, (t, 2 * t)), 0, 1, stride=1, stride_axis=0)
        return shifted[:, t:].astype(BF16)

    @pl.when(g == 0)
    def _():
        def first(cp, carry):
            for c in range(S5_GROUP_CH):
                m_even[pl.ds(pl.multiple_of(cp * t, t), t), c * t:(c + 1) * t] = toeplitz_block(kc_ref, cp, c)
            return carry
        lax.fori_loop(0, S5_GROUP_CH, first, 0)

    row = lax.broadcasted_iota(jnp.int32, (rows, t), 0)
    u = _modulated_rows(lat_ref, ctx_ref, scale_ref, shift_ref, g, row, class_starts, d_model)
    h = h_ref[...].astype(BF16)

    def step(m_now, m_next):
        for cp in range(S5_GROUP_CH):
            for c in range(S5_GROUP_CH):
                m_next[cp * t:(cp + 1) * t, c * t:(c + 1) * t] = toeplitz_block(kc_next_ref, cp, c)
        acc = jnp.dot(u, m_now[...], preferred_element_type=F32)
        for c in range(S5_GROUP_CH):
            y = acc[:, c * t:(c + 1) * t] + jnp.dot(h, pout_ref[0, c], preferred_element_type=F32)
            out_lat_ref[c] = y[:lat_rows]
            out_ctx_ref[c] = y[lat_rows:]

    @pl.when(g % 2 == 0)
    def _():
        step(m_even, m_odd)

    @pl.when(g % 2 == 1)
    def _():
        step(m_odd, m_even)


def _s5_expand_kernel(in_re_ref, in_im_ref, wre_ref, wim_ref, x_ref, y_ref, wx_ref, wy_ref, pin_ref, pout_ref):
    t = SCAN_CHUNK
    in_re, in_im = in_re_ref[0], in_im_ref[0]
    for c in range(S5_GROUP_CH):
        block = in_re * wre_ref[0, c:c + 1, :] + in_im * wim_ref[0, c:c + 1, :]
        pin_ref[0, c * t:(c + 1) * t, :] = block.astype(BF16)
    x, y = x_ref[0], y_ref[0]
    wx, wy = wx_ref[0], wy_ref[0]
    for c in range(S5_GROUP_CH):
        pout_ref[0, c] = (wx[:, c:c + 1] * x + wy[:, c:c + 1] * y).astype(BF16)


def _s5_expand_tables(in_re, in_im, with_re, with_im, out_x, out_y, with_x, with_y):
    groups, t, state_w = in_re.shape
    ch = with_re.shape[1]
    per_group = lambda shape: pl.BlockSpec((1,) + shape, lambda g: (g,) + (0,) * len(shape))
    return pl.pallas_call(
        _s5_expand_kernel,
        grid=(groups,),
        in_specs=[per_group((t, state_w)), per_group((t, state_w)), per_group((ch, state_w)), per_group((ch, state_w)),
                  per_group((state_w, t)), per_group((state_w, t)), per_group((state_w, ch)), per_group((state_w, ch))],
        out_specs=[per_group((ch * t, state_w)), per_group((ch, state_w, t))],
        out_shape=[jax.ShapeDtypeStruct((groups, ch * t, state_w), BF16),
                   jax.ShapeDtypeStruct((groups, ch, state_w, t), BF16)],
        compiler_params=_params("parallel"),
        name="s5_expand_tables",
    )(in_re, in_im, with_re, with_im, out_x, out_y, with_x, with_y)


def _s5_tables(a_re, a_im, log_dt, b_re, b_im, c_re, c_im):
    t = SCAN_CHUNK
    a_re, a_im = a_re.astype(F32), a_im.astype(F32)
    dt = jnp.exp(log_dt.astype(F32))[..., None]
    steps = jnp.arange(t + 1, dtype=F32)[None, None, :, None]
    mag = jnp.exp(steps * (a_re * dt)[:, :, None])
    ang = steps * (a_im * dt)[:, :, None]
    pw_re, pw_im = mag * jnp.cos(ang), mag * jnp.sin(ang)
    lam_re, lam_im = pw_re[:, :, 1], pw_im[:, :, 1]
    inv_den = 1.0 / (a_re * a_re + a_im * a_im)
    n_re = lam_re - 1.0
    f_re = (n_re * a_re + lam_im * a_im) * inv_den
    f_im = (lam_im * a_re - n_re * a_im) * inv_den
    bb_re = f_re[..., None] * b_re - f_im[..., None] * b_im
    bb_im = f_re[..., None] * b_im + f_im[..., None] * b_re

    cp_re, cp_im = c_re.transpose(0, 1, 3, 2)[..., None], c_im.transpose(0, 1, 3, 2)[..., None]
    cb_re = cp_re * bb_re[:, :, :, None] - cp_im * bb_im[:, :, :, None]
    cb_im = cp_re * bb_im[:, :, :, None] + cp_im * bb_re[:, :, :, None]
    resp = (jnp.einsum('dgtp,dgpck->dgkct', pw_re[:, :, :t], cb_re, precision=HIGHEST)
            - jnp.einsum('dgtp,dgpck->dgkct', pw_im[:, :, :t], cb_im, precision=HIGHEST))
    fwd, bwd = resp[0], resp[1]
    lags = jnp.concatenate([jnp.zeros_like(fwd[..., :1]), bwd[..., :0:-1],
                            fwd[..., :1] + bwd[..., :1], fwd[..., 1:]], axis=-1)

    def lanes(f_a, f_b, b_a, b_b):
        return jnp.concatenate([f_a, f_b, b_a, b_b], axis=-1)

    in_re = lanes(pw_re[0, :, t - 1::-1], pw_re[0, :, t - 1::-1], pw_re[1, :, :t], pw_re[1, :, :t])
    in_im = lanes(pw_im[0, :, t - 1::-1], pw_im[0, :, t - 1::-1], pw_im[1, :, :t], pw_im[1, :, :t])
    bt_re, bt_im = bb_re.transpose(0, 1, 3, 2), bb_im.transpose(0, 1, 3, 2)
    with_re = lanes(bt_re[0], bt_im[0], bt_re[1], bt_im[1])
    with_im = lanes(-bt_im[0], bt_re[0], -bt_im[1], bt_re[1])

    def rows(f_a, f_b, b_a, b_b):
        return jnp.concatenate([f_a, f_b, b_a, b_b], axis=-2)

    fr, fi = pw_re[0, :, 1:].transpose(0, 2, 1), pw_im[0, :, 1:].transpose(0, 2, 1)
    br, bi = pw_re[1, :, :0:-1].transpose(0, 2, 1), pw_im[1, :, :0:-1].transpose(0, 2, 1)
    out_x, out_y = rows(fr, fi, br, bi), rows(fi, fr, bi, br)
    cr, ci = c_re.transpose(0, 1, 3, 2), c_im.transpose(0, 1, 3, 2)
    with_x = rows(cr[0], -cr[0], cr[1], -cr[1])
    with_y = rows(-ci[0], -ci[0], -ci[1], -ci[1])
    p_in, p_out = _s5_expand_tables(in_re, in_im, with_re, with_im, out_x, out_y, with_x, with_y)

    carry_a = lanes(pw_re[0, :, t], pw_re[0, :, t], pw_re[1, :, t], pw_re[1, :, t])
    carry_b = lanes(-pw_im[0, :, t], pw_im[0, :, t], -pw_im[1, :, t], pw_im[1, :, t])
    return lags, p_in, p_out, carry_a, carry_b


def _to_chunk_major(rows2d):
    n, d = rows2d.shape
    return rows2d.reshape(n // SCAN_CHUNK, SCAN_CHUNK, d).transpose(2, 0, 1)


def _from_chunk_major(xt):
    d, chunks, t = xt.shape
    return xt.transpose(1, 2, 0).reshape(chunks * t, d)


def _s5_scan(x_lat, x_ctx, scale, shift, tables, bsz):
    lags, p_in, p_out, carry_a, carry_b = tables
    d = x_lat.shape[1]
    groups = d // S5_GROUP_CH
    ctx_chunks = x_ctx.shape[0] // bsz // SCAN_CHUNK
    lat_chunks = x_lat.shape[0] // bsz // SCAN_CHUNK
    lat_rows, ctx_rows = bsz * lat_chunks, bsz * ctx_chunks
    rows = lat_rows + ctx_rows
    xt_lat, xt_ctx = _to_chunk_major(x_lat), _to_chunk_major(x_ctx)
    class_starts = tuple(b * lat_chunks for b in range(bsz)) + (lat_rows,)
    scale_flat, shift_flat = scale.reshape(-1), shift.reshape(-1)
    state_w = 4 * S5_STATE
    smem = pl.BlockSpec(memory_space=pltpu.SMEM)
    slab_lat = pl.BlockSpec((S5_GROUP_CH, lat_rows, SCAN_CHUNK), lambda g: (g, 0, 0))
    slab_ctx = pl.BlockSpec((S5_GROUP_CH, ctx_rows, SCAN_CHUNK), lambda g: (g, 0, 0))

    inject = pl.pallas_call(
        functools.partial(_s5_inject_kernel, class_starts=class_starts, d_model=d),
        grid=(groups,),
        in_specs=[smem, smem, slab_lat, slab_ctx,
                  pl.BlockSpec((1, S5_GROUP_CH * SCAN_CHUNK, state_w), lambda g: (g, 0, 0))],
        out_specs=pl.BlockSpec((rows, state_w), lambda g: (0, g)),
        out_shape=jax.ShapeDtypeStruct((rows, groups * state_w), F32),
        compiler_params=_params("parallel"),
        name="s5_inject",
    )(scale_flat, shift_flat, xt_lat, xt_ctx, p_in)

    gb = 16
    entering = pl.pallas_call(
        functools.partial(_s5_carry_kernel, bsz=bsz, ctx_chunks=ctx_chunks, lat_chunks=lat_chunks),
        grid=(groups // gb,),
        in_specs=[pl.BlockSpec((rows, gb, state_w), lambda g: (0, g, 0)),
                  pl.BlockSpec((gb, state_w), lambda g: (g, 0)),
                  pl.BlockSpec((gb, state_w), lambda g: (g, 0))],
        out_specs=pl.BlockSpec((rows, gb, state_w), lambda g: (0, g, 0)),
        out_shape=jax.ShapeDtypeStruct((rows, groups, state_w), F32),
        compiler_params=_params("parallel"),
        name="s5_carry",
    )(inject.reshape(rows, groups, state_w), carry_a, carry_b)

    yt_lat, yt_ctx = pl.pallas_call(
        functools.partial(_s5_output_kernel, class_starts=class_starts, d_model=d),
        grid=(groups,),
        in_specs=[smem, smem, slab_lat, slab_ctx,
                  pl.BlockSpec((1, S5_GROUP_CH, S5_GROUP_CH, 2 * SCAN_CHUNK), lambda g: (g, 0, 0, 0)),
                  pl.BlockSpec((1, S5_GROUP_CH, S5_GROUP_CH, 2 * SCAN_CHUNK),
                               lambda g: (jnp.minimum(g + 1, groups - 1), 0, 0, 0)),
                  pl.BlockSpec((rows, state_w), lambda g: (0, g)),
                  pl.BlockSpec((1, S5_GROUP_CH, state_w, SCAN_CHUNK), lambda g: (g, 0, 0, 0))],
        out_specs=[slab_lat, slab_ctx],
        out_shape=[jax.ShapeDtypeStruct((d, lat_rows, SCAN_CHUNK), F32),
                   jax.ShapeDtypeStruct((d, ctx_rows, SCAN_CHUNK), F32)],
        scratch_shapes=[pltpu.VMEM((S5_GROUP_CH * SCAN_CHUNK, S5_GROUP_CH * SCAN_CHUNK), BF16)] * 2,
        compiler_params=_params("arbitrary"),
        name="s5_output",
    )(scale_flat, shift_flat, xt_lat, xt_ctx, lags, lags, entering.reshape(rows, groups * state_w), p_out)

    return _from_chunk_major(yt_lat), _from_chunk_major(yt_ctx)


def kernel(x, c, ctx, c_ctx, mod_w, mod_b, ln_g, ln_b, router_w, router_b, moe_w1, moe_w3, moe_w2, s5_a_re, s5_a_im, s5_log_dt, s5_b_re, s5_b_im, s5_c_re, s5_c_im, s5_d, s5_w_gate, s5_w_val, swa_w_qkv, swa_w_o, swa_sink, gqa_w_qkv, gqa_w_o, gqa_q_norm, gqa_k_norm):
    bsz, seq, d = x.shape
    n_ctx = ctx.shape[1]
    depth = mod_w.shape[0]
    alpha = (2.0 * depth) ** 0.25
    heads_b, heads_c = d // HEAD_DIM_B, d // HEAD_DIM_C

    cond = jnp.concatenate([c_ctx[None, :], c], axis=0)
    mods = _modulation(cond, mod_w, mod_b).reshape(depth, bsz + 1, 6, d)
    xl = x.reshape(bsz * seq, d)
    xc = ctx.reshape(bsz * n_ctx, d)
    router_w_t = router_w.T
    rope_b = _rope_tables(seq, HEAD_DIM_B)
    rope_c = _rope_tables(seq, HEAD_DIM_C)

    for i in range(depth):
        kind, j = i % N_MIXERS, i // N_MIXERS
        ctx_out = i < depth - 1
        lat_mod = [mods[i, 1:, k][:, None, :] for k in range(6)]
        ctx_mod = [jnp.broadcast_to(mods[i, 0, k][None, None, :], (bsz, 1, d)) for k in range(6)]
        ln1 = (ln_g[i, 0], ln_b[i, 0])
        ln2 = (ln_g[i, 1], ln_b[i, 1])
        buffers = [(xl, lat_mod, True)] + ([(xc, ctx_mod, False)] if ctx_out else [])

        if kind == 0:
            tables = _s5_tables(s5_a_re[j], s5_a_im[j], s5_log_dt[j], s5_b_re[j], s5_b_im[j],
                                s5_c_re[j], s5_c_im[j])
            ctx_last = jnp.roll(mods[i], -1, axis=0)
            y_lat, y_ctx = _s5_scan(xl, xc, 1.0 + ctx_last[:, 1], ctx_last[:, 0], tables, bsz)
            glu_w = (s5_d[j], s5_w_val[j].astype(BF16), s5_w_gate[j].astype(BF16))
            mixed = {True: y_lat, False: y_ctx}
        else:
            if kind == 1:
                w_qkv, w_o = swa_w_qkv[j].astype(BF16), swa_w_o[j].astype(BF16)
                heads, kv, dh, norms, rope = heads_b, KV_HEADS_B, HEAD_DIM_B, None, rope_b
            else:
                w_qkv, w_o = gqa_w_qkv[j].astype(BF16), gqa_w_o[j].astype(BF16)
                heads, kv, dh, norms, rope = heads_c, KV_HEADS_C, HEAD_DIM_C, (gqa_q_norm[j], gqa_k_norm[j]), rope_c
            ql, kl, vl = _qkv_project(xl, lat_mod[1], lat_mod[0], w_qkv, heads, kv, dh, norms=norms, rope=rope)
            qc, kc, vc = _qkv_project(xc, ctx_mod[1], ctx_mod[0], w_qkv, heads, kv, dh, norms=norms, rope=None)
            if kind == 1:
                sink = swa_sink[j].astype(F32)
                o_lat = _window_attention(ql, kl, vl, kc, vc, sink, bsz, dh)
                o_ctx = _flash_attention(qc, kc, vc, bsz, dh, sink=sink) if ctx_out else None
            else:
                keys = jnp.concatenate([kl.reshape(kv, dh, bsz, seq), kc.reshape(kv, dh, bsz, n_ctx)], axis=3)
                vals = jnp.concatenate([vl.reshape(kv, bsz, seq, dh), vc.reshape(kv, bsz, n_ctx, dh)], axis=2)
                o_lat = _flash_attention(ql, keys.reshape(kv, dh, -1), vals.reshape(kv, -1, dh), bsz, dh)
                o_ctx = _flash_attention(qc, kc, vc, bsz, dh) if ctx_out else None
            mixed = {True: o_lat, False: o_ctx}

        w1, w3, w2 = moe_w1[i].astype(BF16), moe_w3[i].astype(BF16), moe_w2[i].astype(BF16)
        new = []
        for rows, mod, is_lat in buffers:
            sh1, sc1, g1, sh2, sc2, g2 = mod
            route = (sc2, sh2, router_w_t, router_b.astype(F32))
            if kind == 0:
                x1, h2, comb_t, grp = _post_mixer(mixed[is_lat], rows, g1, *ln1, alpha,
                                                  glu=(sc1, sh1) + glu_w, route=route)
            else:
                x1, h2, comb_t, grp = _post_mixer(mixed[is_lat], rows, g1, *ln1, alpha, w_o=w_o, route=route)
            new.append(_moe(h2, comb_t, grp, w1, w3, w2, x1, g2, *ln2, alpha))
        xl = new[0]
        if ctx_out:
            xc = new[1]
    return xl.reshape(bsz, seq, d)
```

```python
import functools
import math

import jax
import jax.numpy as jnp
from jax import lax
from jax.experimental import pallas as pl
from jax.experimental.pallas import tpu as pltpu

GRID_W = 64
N_MIXERS = 3
LN_EPS = 1e-6
RMS_EPS = 1e-6
NEG_INF = -1e30
ROPE_THETA = 10000.0
S5_GROUP_CH = 16
S5_STATE = 64
SCAN_CHUNK = 128
WINDOW = 128
HEAD_DIM_B = 64
KV_HEADS_B = 2
HEAD_DIM_C = 128
KV_HEADS_C = 2
N_EXPERTS = 16
N_EXPERT_GROUPS = 4
EXPERTS_PER_GROUP = N_EXPERTS // N_EXPERT_GROUPS
TOP_K = 2

LANES = 128
LOG2E = math.log2(math.e)
VMEM_LIMIT_BYTES = 56 * 1024 * 1024

F32 = jnp.float32
BF16 = jnp.bfloat16
HIGHEST = lax.Precision.HIGHEST


def _params(*semantics):
    return pltpu.CompilerParams(dimension_semantics=semantics,
                                vmem_limit_bytes=VMEM_LIMIT_BYTES)


def _row_tile(n_rows, target):
    t = min(target, n_rows)
    while n_rows % t:
        t //= 2
    return t


def _layer_norm_rows(v, g, b):
    vc = v - jnp.mean(v, axis=-1, keepdims=True)
    var = jnp.mean(vc * vc, axis=-1, keepdims=True)
    return vc * lax.rsqrt(var + LN_EPS) * g + b


def _sigmoid(v):
    return 1.0 / (1.0 + jnp.exp(-v))


def _gelu_tanh(v):
    return 0.5 * v * (1.0 + jnp.tanh(math.sqrt(2.0 / math.pi) * (v + 0.044715 * (v * v * v))))


def _modulation_kernel(cond_t_ref, w_ref, b_ref, out_ref):
    k = pl.program_id(1)
    cond = cond_t_ref[...]
    act = cond * _sigmoid(cond)
    w = w_ref[0]
    part = jnp.concatenate([jnp.sum(w * act[:, r:r + 1], axis=0, keepdims=True) for r in range(act.shape[1])],
                           axis=0)

    @pl.when(k == 0)
    def _():
        out_ref[0] = part + b_ref[0]

    @pl.when(k > 0)
    def _():
        out_ref[0] += part


def _modulation(cond, mod_w, mod_b):
    depth, d, n_out = mod_w.shape
    rows = cond.shape[0]
    tk = 256
    return pl.pallas_call(
        _modulation_kernel,
        grid=(depth, d // tk),
        in_specs=[pl.BlockSpec((tk, rows), lambda i, k: (k, 0)),
                  pl.BlockSpec((1, tk, n_out), lambda i, k: (i, k, 0)),
                  pl.BlockSpec((1, 1, n_out), lambda i, k: (i, 0, 0))],
        out_specs=pl.BlockSpec((1, rows, n_out), lambda i, k: (i, 0, 0)),
        out_shape=jax.ShapeDtypeStruct((depth, rows, n_out), F32),
        compiler_params=_params("parallel", "arbitrary"),
        name="modulation",
    )(cond.T, mod_w, mod_b.reshape(depth, 1, n_out))


def _swap_rotary_pairs(v, half):
    width = v.shape[-1]
    lane = lax.broadcasted_iota(jnp.int32, v.shape, v.ndim - 1)
    first = (lane // half) % 2 == 0
    return jnp.where(first, pltpu.roll(v, width - half, v.ndim - 1), pltpu.roll(v, half, v.ndim - 1))


def _qkv_kernel(*refs, n_heads, n_kv, head_dim, qk_norm, use_rope):
    it = iter(refs)
    x_ref, sc_ref, sh_ref, w_ref = next(it), next(it), next(it), next(it)
    if qk_norm:
        qn_ref, kn_ref = next(it), next(it)
    if use_rope:
        cos_ref, sin_ref = next(it), next(it)
    q_ref, kt_ref, v_ref = next(it), next(it), next(it)

    h = (x_ref[...] * (1.0 + sc_ref[0]) + sh_ref[0]).astype(BF16)
    proj = jnp.dot(h, w_ref[...], preferred_element_type=F32)
    q_width, kv_width = n_heads * head_dim, n_kv * head_dim
    q = proj[:, :q_width]
    k = proj[:, q_width:q_width + kv_width]
    v = proj[:, q_width + kv_width:]

    def rms_heads(t, gain, count):
        parts = []
        for hd in range(count):
            th = t[:, hd * head_dim:(hd + 1) * head_dim]
            ms = jnp.mean(th * th, axis=-1, keepdims=True)
            parts.append(th * lax.rsqrt(ms + RMS_EPS) * gain)
        return jnp.concatenate(parts, axis=-1)

    if qk_norm:
        q = rms_heads(q, qn_ref[...], n_heads)
        k = rms_heads(k, kn_ref[...], n_kv)

    if use_rope:
        cos, sin = cos_ref[...], sin_ref[...]

        def rope(t):
            reps = t.shape[-1] // LANES
            c = jnp.concatenate([cos] * reps, axis=-1) if reps > 1 else cos
            s = jnp.concatenate([sin] * reps, axis=-1) if reps > 1 else sin
            return t * c + _swap_rotary_pairs(t, head_dim // 4) * s

        q, k = rope(q), rope(k)

    q = q * (head_dim ** -0.5 * LOG2E)
    k_t = k.T
    group = q_width // n_kv
    for g in range(n_kv):
        q_ref[g] = q[:, g * group:(g + 1) * group].astype(BF16)
        kt_ref[g] = k_t[g * head_dim:(g + 1) * head_dim, :].astype(BF16)
        v_ref[g] = v[:, g * head_dim:(g + 1) * head_dim].astype(BF16)


def _qkv_project(x, sc, sh, w, n_heads, n_kv, head_dim, norms=None, rope=None, tm_target=512):
    n, d = x.shape
    bsz = sc.shape[0]
    per_batch = n // bsz
    tm = _row_tile(per_batch, tm_target)
    tiles_per_batch = per_batch // tm
    n_out = w.shape[1]
    group = n_heads * head_dim // n_kv
    row = lambda i: (i, 0)
    batch_vec = lambda i: (i // tiles_per_batch, 0, 0)
    const2 = lambda i: (0, 0)
    in_specs = [pl.BlockSpec((tm, d), row),
                pl.BlockSpec((1, 1, d), batch_vec),
                pl.BlockSpec((1, 1, d), batch_vec),
                pl.BlockSpec((d, n_out), const2)]
    args = [x, sc, sh, w]
    if norms is not None:
        in_specs += [pl.BlockSpec((1, head_dim), const2)] * 2
        args += [norms[0].reshape(1, head_dim), norms[1].reshape(1, head_dim)]
    if rope is not None:
        in_specs += [pl.BlockSpec((tm, LANES), lambda i: (i % tiles_per_batch, 0))] * 2
        args += [rope[0], rope[1]]
    kernel = functools.partial(_qkv_kernel, n_heads=n_heads, n_kv=n_kv, head_dim=head_dim,
                               qk_norm=norms is not None, use_rope=rope is not None)
    return pl.pallas_call(
        kernel,
        grid=(n // tm,),
        in_specs=in_specs,
        out_specs=[pl.BlockSpec((n_kv, tm, group), lambda i: (0, i, 0)),
                   pl.BlockSpec((n_kv, head_dim, tm), lambda i: (0, 0, i)),
                   pl.BlockSpec((n_kv, tm, head_dim), lambda i: (0, i, 0))],
        out_shape=[jax.ShapeDtypeStruct((n_kv, n, group), BF16),
                   jax.ShapeDtypeStruct((n_kv, head_dim, n), BF16),
                   jax.ShapeDtypeStruct((n_kv, n, head_dim), BF16)],
        compiler_params=_params("parallel"),
        name="qkv_project",
    )(*args)


def _rope_tables(seq, head_dim):
    quarter = head_dim // 4
    inv_freq = ROPE_THETA ** (-jnp.arange(quarter, dtype=F32) / quarter)
    t = jnp.arange(seq)
    rows = (t // GRID_W).astype(F32)
    cols = (t % GRID_W).astype(F32)
    ang_r = rows[:, None] * inv_freq
    ang_c = cols[:, None] * inv_freq
    cos = jnp.concatenate([jnp.cos(ang_r)] * 2 + [jnp.cos(ang_c)] * 2, axis=-1)
    sin = jnp.concatenate([-jnp.sin(ang_r), jnp.sin(ang_r), -jnp.sin(ang_c), jnp.sin(ang_c)], axis=-1)
    reps = LANES // head_dim
    return jnp.tile(cos, (1, reps)), jnp.tile(sin, (1, reps))


def _stack_heads(q, n_rep, head_dim):
    if n_rep == 1:
        return q
    return jnp.concatenate([q[:, r * head_dim:(r + 1) * head_dim] for r in range(n_rep)], axis=0)


def _unstack_heads(o, n_rep, rows):
    if n_rep == 1:
        return o
    return jnp.concatenate([o[r * rows:(r + 1) * rows] for r in range(n_rep)], axis=-1)


def _sink_column(sink_ref, g, n_rep, rows):
    return jnp.concatenate([jnp.full((rows, 1), sink_ref[g * n_rep + r] * LOG2E, F32) for r in range(n_rep)],
                           axis=0)


def _ones_width(head_dim):
    return LANES - head_dim % LANES


def _with_ones(v, head_dim):
    return jnp.concatenate([v, jnp.ones((v.shape[0], _ones_width(head_dim)), v.dtype)], axis=1)


def _lane_repeat(col_block, width):
    reps = width // col_block.shape[1]
    return col_block if reps == 1 else jnp.concatenate([col_block] * reps, axis=1)


def _flash_kernel(*refs, n_rep, head_dim, has_sink, n_split):
    if has_sink:
        sink_ref, q_ref, kt_ref, v_ref, o_ref, m_sc, acc_sc = refs
    else:
        q_ref, kt_ref, v_ref, o_ref, m_sc, acc_sc = refs
    kv_head, kj = pl.program_id(1), pl.program_id(3)
    tq = q_ref.shape[1]
    width = acc_sc.shape[1]

    @pl.when(kj == 0)
    def _():
        m_sc[...] = jnp.full(m_sc.shape, NEG_INF, F32)
        acc_sc[...] = jnp.zeros(acc_sc.shape, F32)

    m_all = m_sc[...]
    acc_all = acc_sc[...]
    q = _stack_heads(q_ref[0], n_rep, head_dim)
    kt = kt_ref[0]
    v1 = _with_ones(v_ref[0], head_dim)
    part = n_rep * tq // n_split
    m_out, acc_out = [], []
    for h in range(n_split):
        rs = slice(h * part, (h + 1) * part)
        s = jnp.dot(q[rs], kt, preferred_element_type=F32)
        m_prev = m_all[rs]
        m_new = jnp.maximum(m_prev, jnp.max(s, axis=-1, keepdims=True))
        p = jnp.exp2(s - _lane_repeat(m_new, s.shape[1]))
        alpha = jnp.exp2(m_prev - m_new)
        acc_out.append(_lane_repeat(alpha, width) * acc_all[rs]
                       + jnp.dot(p.astype(BF16), v1, preferred_element_type=F32))
        m_out.append(m_new)
    m_sc[...] = jnp.concatenate(m_out, axis=0)
    acc_sc[...] = jnp.concatenate(acc_out, axis=0)

    @pl.when(kj == pl.num_programs(3) - 1)
    def _():
        acc = acc_sc[...]
        num, l = acc[:, :head_dim], acc[:, head_dim:head_dim + 1]
        if has_sink:
            m = m_sc[:, :1]
            sink = _sink_column(sink_ref, kv_head, n_rep, tq)
            m_fin = jnp.maximum(m, sink)
            scale = jnp.exp2(m - m_fin)
            l = l * scale + jnp.exp2(sink - m_fin)
            num = num * scale
        o_ref[...] = _unstack_heads(num * (1.0 / l), n_rep, tq).astype(o_ref.dtype)


def _flash_attention(q, kt, v, bsz, head_dim, sink=None, tq_target=1024, tk_target=3328, n_split=32):
    n_kv, nq, group = q.shape
    n_rep = group // head_dim
    lq, lk = nq // bsz, v.shape[1] // bsz
    tq = _row_tile(lq, tq_target)
    tk = tk_target if lk % tk_target == 0 else _row_tile(lk, 256)
    qt, kt_tiles = lq // tq, lk // tk
    kernel = functools.partial(_flash_kernel, n_rep=n_rep, head_dim=head_dim, has_sink=sink is not None,
                               n_split=n_split)
    in_specs = [pl.BlockSpec((1, tq, group), lambda b, g, i, j: (g, b * qt + i, 0)),
                pl.BlockSpec((1, head_dim, tk), lambda b, g, i, j: (g, 0, b * kt_tiles + j)),
                pl.BlockSpec((1, tk, head_dim), lambda b, g, i, j: (g, b * kt_tiles + j, 0))]
    args = [q, kt, v]
    if sink is not None:
        in_specs = [pl.BlockSpec(memory_space=pltpu.SMEM)] + in_specs
        args = [sink] + args
    return pl.pallas_call(
        kernel,
        grid=(bsz, n_kv, qt, kt_tiles),
        in_specs=in_specs,
        out_specs=pl.BlockSpec((tq, group), lambda b, g, i, j: (b * qt + i, g)),
        out_shape=jax.ShapeDtypeStruct((nq, n_kv * group), BF16),
        scratch_shapes=[pltpu.VMEM((n_rep * tq, LANES), F32),
                        pltpu.VMEM((n_rep * tq, head_dim + _ones_width(head_dim)), F32)],
        compiler_params=_params("parallel", "parallel", "parallel", "arbitrary"),
        name="flash_attention",
    )(*args)


def _window_kernel(sink_ref, q_ref, kp_ref, kc_ref, kn_ref, kx_ref, vp_ref, vc_ref, vn_ref, vx_ref, o_ref,
                   *, n_rep, head_dim, seq):
    kv_head, i = pl.program_id(1), pl.program_id(2)
    tq = q_ref.shape[1]
    band = tq + 2 * WINDOW
    kt = jnp.concatenate([kp_ref[0], kc_ref[0], kn_ref[0], kx_ref[0]], axis=1)
    v1 = _with_ones(jnp.concatenate([vp_ref[0], vc_ref[0], vn_ref[0], vx_ref[0]], axis=0), head_dim)
    n_keys = kt.shape[1]

    qi = lax.broadcasted_iota(jnp.int32, (tq, n_keys), 0)
    kj = lax.broadcasted_iota(jnp.int32, (tq, n_keys), 1)
    key_pos = i * tq - WINDOW + kj
    ok = (kj >= band) | ((jnp.abs(kj - WINDOW - qi) <= WINDOW) & (key_pos >= 0) & (key_pos < seq))

    q = q_ref[0]
    outs = []
    for r in range(n_rep):
        s = jnp.dot(q[:, r * head_dim:(r + 1) * head_dim], kt, preferred_element_type=F32)
        s = jnp.where(ok, s, NEG_INF)
        sink = sink_ref[kv_head * n_rep + r] * LOG2E
        m = jnp.maximum(jnp.max(s, axis=-1, keepdims=True), sink)
        p = jnp.exp2(s - m)
        pv = jnp.dot(p.astype(BF16), v1, preferred_element_type=F32)
        denom = pv[:, head_dim:head_dim + 1] + jnp.exp2(sink - m)
        outs.append(pv[:, :head_dim] * (1.0 / denom))
    o_ref[...] = jnp.concatenate(outs, axis=1).astype(o_ref.dtype)


def _window_attention(q, kt, v, kt_ctx, v_ctx, sink, bsz, head_dim, tq_target=512):
    n_kv, nq, group = q.shape
    n_rep = group // head_dim
    seq = nq // bsz
    n_ctx = v_ctx.shape[1] // bsz
    tq = _row_tile(seq, tq_target)
    qt = seq // tq
    ratio = tq // WINDOW
    halo_blocks = seq // WINDOW
    prev_blk = lambda b, i: b * halo_blocks + jnp.maximum(i * ratio - 1, 0)
    next_blk = lambda b, i: b * halo_blocks + jnp.minimum((i + 1) * ratio, halo_blocks - 1)
    q_map = lambda b, g, i: (g, b * qt + i, 0)
    k_specs = [pl.BlockSpec((1, head_dim, WINDOW), lambda b, g, i: (g, 0, prev_blk(b, i))),
               pl.BlockSpec((1, head_dim, tq), lambda b, g, i: (g, 0, b * qt + i)),
               pl.BlockSpec((1, head_dim, WINDOW), lambda b, g, i: (g, 0, next_blk(b, i))),
               pl.BlockSpec((1, head_dim, n_ctx), lambda b, g, i: (g, 0, b))]
    v_specs = [pl.BlockSpec((1, WINDOW, head_dim), lambda b, g, i: (g, prev_blk(b, i), 0)),
               pl.BlockSpec((1, tq, head_dim), q_map),
               pl.BlockSpec((1, WINDOW, head_dim), lambda b, g, i: (g, next_blk(b, i), 0)),
               pl.BlockSpec((1, n_ctx, head_dim), lambda b, g, i: (g, b, 0))]
    kernel = functools.partial(_window_kernel, n_rep=n_rep, head_dim=head_dim, seq=seq)
    return pl.pallas_call(
        kernel,
        grid=(bsz, n_kv, qt),
        in_specs=[pl.BlockSpec(memory_space=pltpu.SMEM), pl.BlockSpec((1, tq, group), q_map)] + k_specs + v_specs,
        out_specs=pl.BlockSpec((tq, group), lambda b, g, i: (b * qt + i, g)),
        out_shape=jax.ShapeDtypeStruct((nq, n_kv * group), BF16),
        compiler_params=_params("parallel", "parallel", "parallel"),
        name="window_attention",
    )(sink, q, kt, kt, kt, kt_ctx, v, v, v, v_ctx)


def _router_combine(logits_t, bias_ref):
    aff = [_sigmoid(logits_t[e:e + 1, :]) for e in range(N_EXPERTS)]
    sel = [aff[e] + bias_ref[e] for e in range(N_EXPERTS)]
    best_score, best_group = None, None
    for g in range(N_EXPERT_GROUPS):
        a, b, c, d = sel[g * EXPERTS_PER_GROUP:(g + 1) * EXPERTS_PER_GROUP]
        hi1, lo1 = jnp.maximum(a, b), jnp.minimum(a, b)
        hi2, lo2 = jnp.maximum(c, d), jnp.minimum(c, d)
        score = jnp.maximum(hi1, hi2) + jnp.maximum(jnp.minimum(hi1, hi2), jnp.maximum(lo1, lo2))
        if g == 0:
            best_score, best_group = score, jnp.zeros(score.shape, jnp.int32)
        else:
            better = score > best_score
            best_score = jnp.where(better, score, best_score)
            best_group = jnp.where(better, g, best_group)
    gates = []
    for e in range(N_EXPERTS):
        g = e // EXPERTS_PER_GROUP
        beaten = jnp.zeros(best_group.shape, jnp.int32)
        for o in range(g * EXPERTS_PER_GROUP, (g + 1) * EXPERTS_PER_GROUP):
            if o == e:
                continue
            wins = (sel[o] > sel[e]) | ((sel[o] == sel[e]) & (o < e))
            beaten = beaten + wins.astype(jnp.int32)
        chosen = (best_group == g) & (beaten < TOP_K)
        gates.append(jnp.where(chosen, aff[e], 0.0))
    total = functools.reduce(jnp.add, gates)
    inv = 1.0 / total
    return jnp.concatenate([gt * inv for gt in gates], axis=0), best_group


POST_MIXER_ROWS = 16


def _post_mixer_kernel(*refs, alpha, glu, route):
    it = iter(refs)
    rb_ref = next(it) if route else None
    a_ref, x_ref = next(it), next(it)
    if glu:
        sc1_ref, sh1_ref, dsk_ref, wv_ref, wg_ref = next(it), next(it), next(it), next(it), next(it)
    else:
        wo_ref = next(it)
    g1_ref, lng_ref, lnb_ref = next(it), next(it), next(it)
    if route:
        sc2_ref, sh2_ref, rw_ref = next(it), next(it), next(it)
    x1_ref = next(it)
    if route:
        h2_ref, comb_ref, grp_ref, lo_sc = next(it), next(it), next(it), next(it)

    tm = x_ref.shape[0]
    blocks = [slice(r, r + POST_MIXER_ROWS) for r in range(0, tm, POST_MIXER_ROWS)]
    if glu:
        scale1, shift1, d_skip = 1.0 + sc1_ref[0], sh1_ref[0], dsk_ref[...]
        act = jnp.concatenate([_gelu_tanh(d_skip * (x_ref[b, :] * scale1 + shift1) + a_ref[b, :]).astype(BF16)
                               for b in blocks], axis=0)
        val = jnp.dot(act, wv_ref[...], preferred_element_type=F32)
        gate = jnp.dot(act, wg_ref[...], preferred_element_type=F32)
    else:
        y = jnp.dot(a_ref[...], wo_ref[...], preferred_element_type=F32)
    g1, ln_g, ln_b = g1_ref[0], lng_ref[...], lnb_ref[...]
    if route:
        scale2, shift2 = 1.0 + sc2_ref[0], sh2_ref[0]
    for b in blocks:
        yb = val[b] * _sigmoid(gate[b]) if glu else y[b]
        x1 = _layer_norm_rows(alpha * x_ref[b, :] + g1 * yb, ln_g, ln_b)
        x1_ref[b, :] = x1
        if route:
            h2 = x1 * scale2 + shift2
            hi = h2.astype(BF16)
            h2_ref[b, :] = hi
            lo_sc[b, :] = (h2 - hi.astype(F32)).astype(BF16)
    if route:
        rw = rw_ref[...]
        rw_hi = rw.astype(BF16)
        rw_lo = (rw - rw_hi.astype(F32)).astype(BF16)
        nt = (((1,), (1,)), ((), ()))
        hi_all, lo_all = h2_ref[...], lo_sc[...]
        logits_t = (lax.dot_general(rw_hi, hi_all, nt, preferred_element_type=F32)
                    + lax.dot_general(rw_lo, hi_all, nt, preferred_element_type=F32)
                    + lax.dot_general(rw_hi, lo_all, nt, preferred_element_type=F32))
        comb_ref[...], grp_ref[...] = _router_combine(logits_t, rb_ref)


def _post_mixer(a, x, g1, ln_g, ln_b, alpha, *, w_o=None, glu=None, route=None, tm_target=512):
    n, d = x.shape
    bsz = g1.shape[0]
    per_batch = n // bsz
    tm = _row_tile(per_batch, tm_target)
    tiles_per_batch = per_batch // tm
    row = lambda i: (i, 0)
    batch_vec = lambda i: (i // tiles_per_batch, 0, 0)
    const2 = lambda i: (0, 0)
    vec = pl.BlockSpec((1, 1, d), batch_vec)
    in_specs, args = [], []
    if route is not None:
        in_specs.append(pl.BlockSpec(memory_space=pltpu.SMEM))
        args.append(route[3])
    in_specs += [pl.BlockSpec((tm, d), row), pl.BlockSpec((tm, d), row)]
    args += [a, x]
    if glu is not None:
        sc1, sh1, d_skip, w_val, w_gate = glu
        in_specs += [vec, vec, pl.BlockSpec((1, d), const2),
                     pl.BlockSpec((d, d), const2), pl.BlockSpec((d, d), const2)]
        args += [sc1, sh1, d_skip.reshape(1, d), w_val, w_gate]
    else:
        in_specs.append(pl.BlockSpec((d, d), const2))
        args.append(w_o)
    in_specs += [vec, pl.BlockSpec((1, d), const2), pl.BlockSpec((1, d), const2)]
    args += [g1, ln_g.reshape(1, d), ln_b.reshape(1, d)]
    out_specs = [pl.BlockSpec((tm, d), row)]
    out_shape = [jax.ShapeDtypeStruct((n, d), F32)]
    if route is not None:
        in_specs += [vec, vec, pl.BlockSpec((N_EXPERTS, d), const2)]
        args += [route[0], route[1], route[2]]
        out_specs += [pl.BlockSpec((tm, d), row), pl.BlockSpec((N_EXPERTS, tm), lambda i: (0, i)),
                      pl.BlockSpec((1, tm), lambda i: (0, i))]
        out_shape += [jax.ShapeDtypeStruct((n, d), BF16), jax.ShapeDtypeStruct((N_EXPERTS, n), F32),
                      jax.ShapeDtypeStruct((1, n), jnp.int32)]
    kernel = functools.partial(_post_mixer_kernel, alpha=alpha, glu=glu is not None, route=route is not None)
    return pl.pallas_call(
        kernel,
        grid=(n // tm,),
        in_specs=in_specs,
        out_specs=out_specs,
        out_shape=out_shape,
        scratch_shapes=[pltpu.VMEM((tm, d), BF16)] if route is not None else [],
        compiler_params=_params("parallel"),
        name="post_mixer",
    )(*args)


MOE_BLOCK_ROWS = 256


def _moe_rank_kernel(grp_ref, key_ref, cnt_ref):
    grp = grp_ref[...]
    tm = grp.shape[1]
    member = [grp == g for g in range(N_EXPERT_GROUPS)]
    onehot = jnp.concatenate([m.astype(F32) for m in member]
                             + [jnp.zeros((8 - N_EXPERT_GROUPS, tm), F32)], axis=0)
    earlier = (lax.broadcasted_iota(jnp.int32, (tm, tm), 0)
               < lax.broadcasted_iota(jnp.int32, (tm, tm), 1)).astype(BF16)
    prefix = jnp.dot(onehot.astype(BF16), earlier, preferred_element_type=F32)
    rank = functools.reduce(jnp.add, [jnp.where(member[g], prefix[g:g + 1], 0.0)
                                      for g in range(N_EXPERT_GROUPS)])
    key_ref[...] = (grp * (2 * tm)).astype(F32) + rank
    cnt_ref[0] = jnp.broadcast_to(jnp.sum(onehot, axis=1, keepdims=True), (8, LANES))


def _moe_kernel(cnt_ref, h_ref, keyr_ref, keyc_ref, cg_ref, w1_ref, w3_ref, w2_ref, x_ref, g2_ref, lng_ref, lnb_ref,
                out_ref, acc_sc, *, alpha):
    i, g = pl.program_id(0), pl.program_id(1)
    tm = h_ref.shape[0]
    full, half = MOE_BLOCK_ROWS, MOE_BLOCK_ROWS // 2

    @pl.when(g == 0)
    def _():
        acc_sc[...] = jnp.zeros(acc_sc.shape, F32)

    count = cnt_ref[i * N_EXPERT_GROUPS + g]
    n_full = count // full
    tail = count - n_full * full
    key_row = keyr_ref[...]
    key_col = keyc_ref[...]
    h = h_ref[...]
    first_key = (g * (2 * tm)).astype(F32)

    def run_block(blk, rb):
        base = first_key + (blk * full).astype(F32)
        block_row = lax.broadcasted_iota(jnp.int32, (rb, tm), 0).astype(F32)
        block_col = lax.broadcasted_iota(jnp.int32, (tm, rb), 1).astype(F32)
        onehot = (key_row - base == block_row).astype(F32)
        scatter = (key_col - base == block_col).astype(BF16)
        xs = jnp.dot(onehot.astype(BF16), h, preferred_element_type=F32).astype(BF16)
        z = jnp.zeros((rb, h.shape[1]), F32)
        for e in range(EXPERTS_PER_GROUP):
            weight = jnp.sum(onehot * cg_ref[0, e:e + 1, :], axis=1, keepdims=True)
            a1 = jnp.dot(xs, w1_ref[e], preferred_element_type=F32)
            a3 = jnp.dot(xs, w3_ref[e], preferred_element_type=F32)
            act = (a1 * _sigmoid(a1) * a3 * weight).astype(BF16)
            z = z + jnp.dot(act, w2_ref[e], preferred_element_type=F32)
        acc_sc[...] += jnp.dot(scatter, z.astype(BF16), preferred_element_type=F32)

    def full_block(blk, carry):
        run_block(blk, full)
        return carry

    lax.fori_loop(0, n_full, full_block, 0)

    @pl.when(tail > half)
    def _():
        run_block(n_full, full)

    @pl.when((tail > 0) & (tail <= half))
    def _():
        run_block(n_full, half)

    @pl.when(g == pl.num_programs(1) - 1)
    def _():
        out_ref[...] = _layer_norm_rows(alpha * x_ref[...] + g2_ref[0] * acc_sc[...], lng_ref[...], lnb_ref[...])


def _moe(h2, comb_t, grp, w1, w3, w2, x1, g2, ln_g, ln_b, alpha, tm_target=1024):
    n, d = x1.shape
    bsz = g2.shape[0]
    per_batch = n // bsz
    tm = _row_tile(per_batch, tm_target)
    tiles_per_batch = per_batch // tm
    n_tiles = n // tm
    n_groups = N_EXPERT_GROUPS
    d_expert = w1.shape[2]
    expert_blk = lambda i, g: (g, 0, 0)

    key, cnt = pl.pallas_call(
        _moe_rank_kernel,
        grid=(n_tiles,),
        in_specs=[pl.BlockSpec((1, tm), lambda i: (0, i))],
        out_specs=[pl.BlockSpec((1, tm), lambda i: (0, i)),
                   pl.BlockSpec((1, 8, LANES), lambda i: (i, 0, 0))],
        out_shape=[jax.ShapeDtypeStruct((1, n), F32), jax.ShapeDtypeStruct((n_tiles, 8, LANES), F32)],
        compiler_params=_params("parallel"),
        name="moe_rank",
    )(grp)
    counts = cnt[:, :n_groups, 0].astype(jnp.int32).reshape(-1)
    comb_g = jnp.pad(comb_t.reshape(n_groups, EXPERTS_PER_GROUP, n), ((0, 0), (0, 8 - EXPERTS_PER_GROUP), (0, 0)))

    row = lambda i, g: (i, 0)
    const2 = lambda i, g: (0, 0)
    once = pl.Buffered(1)
    kernel = functools.partial(_moe_kernel, alpha=alpha)
    return pl.pallas_call(
        kernel,
        grid=(n_tiles, n_groups),
        in_specs=[pl.BlockSpec(memory_space=pltpu.SMEM),
                  pl.BlockSpec((tm, d), row),
                  pl.BlockSpec((1, tm), lambda i, g: (0, i)),
                  pl.BlockSpec((tm, 1), row),
                  pl.BlockSpec((1, 8, tm), lambda i, g: (g, 0, i)),
                  pl.BlockSpec((EXPERTS_PER_GROUP, d, d_expert), expert_blk),
                  pl.BlockSpec((EXPERTS_PER_GROUP, d, d_expert), expert_blk),
                  pl.BlockSpec((EXPERTS_PER_GROUP, d_expert, d), expert_blk),
                  pl.BlockSpec((tm, d), row, pipeline_mode=once),
                  pl.BlockSpec((1, 1, d), lambda i, g: (i // tiles_per_batch, 0, 0)),
                  pl.BlockSpec((1, d), const2),
                  pl.BlockSpec((1, d), const2)],
        out_specs=pl.BlockSpec((tm, d), row, pipeline_mode=once),
        out_shape=jax.ShapeDtypeStruct((n, d), F32),
        scratch_shapes=[pltpu.VMEM((tm, d), F32)],
        compiler_params=_params("parallel", "arbitrary"),
        name="moe",
    )(counts, h2, key, key.reshape(n, 1), comb_g, w1, w3, w2, x1, g2, ln_g.reshape(1, d), ln_b.reshape(1, d))


def _modulated_rows(lat_ref, ctx_ref, scale_ref, shift_ref, g, row, class_starts, d_model):
    slabs = []
    for c in range(S5_GROUP_CH):
        channel = g * S5_GROUP_CH + c
        u = jnp.concatenate([lat_ref[c], ctx_ref[c]], axis=0)
        scale = scale_ref[channel]
        shift = shift_ref[channel]
        for k, start in enumerate(class_starts[1:], start=1):
            later = row >= start
            scale = jnp.where(later, scale_ref[k * d_model + channel], scale)
            shift = jnp.where(later, shift_ref[k * d_model + channel], shift)
        slabs.append((u * scale + shift).astype(BF16))
    return jnp.concatenate(slabs, axis=1)


def _s5_inject_kernel(scale_ref, shift_ref, lat_ref, ctx_ref, pin_ref, out_ref, *, class_starts, d_model):
    rows = out_ref.shape[0]
    row = lax.broadcasted_iota(jnp.int32, (rows, SCAN_CHUNK), 0)
    u = _modulated_rows(lat_ref, ctx_ref, scale_ref, shift_ref, pl.program_id(0), row, class_starts, d_model)
    out_ref[...] = jnp.dot(u, pin_ref[0], preferred_element_type=F32)


def _s5_carry_kernel(sin_ref, a_ref, b_ref, h_ref, *, bsz, ctx_chunks, lat_chunks):
    half = 2 * S5_STATE
    ctx0 = bsz * lat_chunks

    def advance(h, row, lo):
        h_ref[row, :, lo:lo + half] = h
        a = a_ref[:, lo:lo + half]
        b = b_ref[:, lo:lo + half]
        return a * h + b * pltpu.roll(h, S5_STATE, 1) + sin_ref[row, :, lo:lo + half]

    zero = jnp.zeros((sin_ref.shape[1], half), F32)
    states = []
    for bi in range(bsz):
        hf, hb = zero, zero
        for n in range(ctx_chunks):
            hf = advance(hf, ctx0 + bi * ctx_chunks + n, 0)
            hb = advance(hb, ctx0 + bi * ctx_chunks + ctx_chunks - 1 - n, half)
        states += [hf, hb]

    def body(n, carry):
        out = []
        for bi in range(bsz):
            base = bi * lat_chunks
            out.append(advance(carry[2 * bi], base + n, 0))
            out.append(advance(carry[2 * bi + 1], base + lat_chunks - 1 - n, half))
        return tuple(out)

    lax.fori_loop(0, lat_chunks, body, tuple(states))


def _s5_output_kernel(scale_ref, shift_ref, lat_ref, ctx_ref, kc_ref, kc_next_ref, h_ref, pout_ref,
                      out_lat_ref, out_ctx_ref, m_even, m_odd, *, class_starts, d_model):
    g = pl.program_id(0)
    rows = h_ref.shape[0]
    lat_rows = lat_ref.shape[1]
    t = SCAN_CHUNK

    def toeplitz_block(lag_ref, cp, c):
        lags = lag_ref[0, cp, c:c + 1, :]
        shifted = pltpu.roll(jnp.broadcast_to(lags, (t, 2 * t)), 0, 1, stride=1, stride_axis=0)
        return shifted[:, t:].astype(BF16)

    @pl.when(g == 0)
    def _():
        def first(cp, carry):
            for c in range(S5_GROUP_CH):
                m_even[pl.ds(pl.multiple_of(cp * t, t), t), c * t:(c + 1) * t] = toeplitz_block(kc_ref, cp, c)
            return carry
        lax.fori_loop(0, S5_GROUP_CH, first, 0)

    row = lax.broadcasted_iota(jnp.int32, (rows, t), 0)
    u = _modulated_rows(lat_ref, ctx_ref, scale_ref, shift_ref, g, row, class_starts, d_model)
    h = h_ref[...].astype(BF16)

    def step(m_now, m_next):
        for cp in range(S5_GROUP_CH):
            for c in range(S5_GROUP_CH):
                m_next[cp * t:(cp + 1) * t, c * t:(c + 1) * t] = toeplitz_block(kc_next_ref, cp, c)
        acc = jnp.dot(u, m_now[...], preferred_element_type=F32)
        for c in range(S5_GROUP_CH):
            y = acc[:, c * t:(c + 1) * t] + jnp.dot(h, pout_ref[0, c], preferred_element_type=F32)
            out_lat_ref[c] = y[:lat_rows]
            out_ctx_ref[c] = y[lat_rows:]

    @pl.when(g % 2 == 0)
    def _():
        step(m_even, m_odd)

    @pl.when(g % 2 == 1)
    def _():
        step(m_odd, m_even)


def _s5_tables(a_re, a_im, log_dt, b_re, b_im, c_re, c_im):
    t = SCAN_CHUNK
    a_re, a_im = a_re.astype(F32), a_im.astype(F32)
    dt = jnp.exp(log_dt.astype(F32))[..., None]
    steps = jnp.arange(t + 1, dtype=F32)[None, None, :, None]
    mag = jnp.exp(steps * (a_re * dt)[:, :, None])
    ang = steps * (a_im * dt)[:, :, None]
    pw_re, pw_im = mag * jnp.cos(ang), mag * jnp.sin(ang)
    lam_re, lam_im = pw_re[:, :, 1], pw_im[:, :, 1]
    inv_den = 1.0 / (a_re * a_re + a_im * a_im)
    n_re = lam_re - 1.0
    f_re = (n_re * a_re + lam_im * a_im) * inv_den
    f_im = (lam_im * a_re - n_re * a_im) * inv_den
    bb_re = f_re[..., None] * b_re - f_im[..., None] * b_im
    bb_im = f_re[..., None] * b_im + f_im[..., None] * b_re

    cp_re, cp_im = c_re.transpose(0, 1, 3, 2)[..., None], c_im.transpose(0, 1, 3, 2)[..., None]
    cb_re = cp_re * bb_re[:, :, :, None] - cp_im * bb_im[:, :, :, None]
    cb_im = cp_re * bb_im[:, :, :, None] + cp_im * bb_re[:, :, :, None]
    resp = (jnp.einsum('dgtp,dgpck->dgkct', pw_re[:, :, :t], cb_re, precision=HIGHEST)
            - jnp.einsum('dgtp,dgpck->dgkct', pw_im[:, :, :t], cb_im, precision=HIGHEST))
    fwd, bwd = resp[0], resp[1]
    lags = jnp.concatenate([jnp.zeros_like(fwd[..., :1]), bwd[..., :0:-1],
                            fwd[..., :1] + bwd[..., :1], fwd[..., 1:]], axis=-1)

    def lanes(f_a, f_b, b_a, b_b):
        return jnp.concatenate([f_a, f_b, b_a, b_b], axis=-1)

    in_re = lanes(pw_re[0, :, t - 1::-1], pw_re[0, :, t - 1::-1], pw_re[1, :, :t], pw_re[1, :, :t])
    in_im = lanes(pw_im[0, :, t - 1::-1], pw_im[0, :, t - 1::-1], pw_im[1, :, :t], pw_im[1, :, :t])
    bt_re, bt_im = bb_re.transpose(0, 1, 3, 2), bb_im.transpose(0, 1, 3, 2)
    with_re = lanes(bt_re[0], bt_im[0], bt_re[1], bt_im[1])
    with_im = lanes(-bt_im[0], bt_re[0], -bt_im[1], bt_re[1])
    p_in = (in_re[:, None] * with_re[:, :, None] + in_im[:, None] * with_im[:, :, None]).astype(BF16)
    p_in = p_in.reshape(p_in.shape[0], -1, 4 * S5_STATE)

    def rows(f_a, f_b, b_a, b_b):
        return jnp.concatenate([f_a, f_b, b_a, b_b], axis=-2)

    fr, fi = pw_re[0, :, 1:].transpose(0, 2, 1), pw_im[0, :, 1:].transpose(0, 2, 1)
    br, bi = pw_re[1, :, :0:-1].transpose(0, 2, 1), pw_im[1, :, :0:-1].transpose(0, 2, 1)
    out_x, out_y = rows(fr, fi, br, bi), rows(fi, fr, bi, br)
    cr, ci = c_re[:, :, :, :, None], c_im[:, :, :, :, None]
    with_x = rows(cr[0], -cr[0], cr[1], -cr[1])
    with_y = rows(-ci[0], -ci[0], -ci[1], -ci[1])
    p_out = (with_x * out_x[:, None] + with_y * out_y[:, None]).astype(BF16)

    carry_a = lanes(pw_re[0, :, t], pw_re[0, :, t], pw_re[1, :, t], pw_re[1, :, t])
    carry_b = lanes(-pw_im[0, :, t], pw_im[0, :, t], -pw_im[1, :, t], pw_im[1, :, t])
    return lags, p_in, p_out, carry_a, carry_b


def _to_chunk_major(rows2d):
    n, d = rows2d.shape
    return rows2d.reshape(n // SCAN_CHUNK, SCAN_CHUNK, d).transpose(2, 0, 1)


def _from_chunk_major(xt):
    d, chunks, t = xt.shape
    return xt.transpose(1, 2, 0).reshape(chunks * t, d)


def _s5_scan(x_lat, x_ctx, scale, shift, tables, bsz):
    lags, p_in, p_out, carry_a, carry_b = tables
    d = x_lat.shape[1]
    groups = d // S5_GROUP_CH
    ctx_chunks = x_ctx.shape[0] // bsz // SCAN_CHUNK
    lat_chunks = x_lat.shape[0] // bsz // SCAN_CHUNK
    lat_rows, ctx_rows = bsz * lat_chunks, bsz * ctx_chunks
    rows = lat_rows + ctx_rows
    xt_lat, xt_ctx = _to_chunk_major(x_lat), _to_chunk_major(x_ctx)
    class_starts = tuple(b * lat_chunks for b in range(bsz)) + (lat_rows,)
    scale_flat, shift_flat = scale.reshape(-1), shift.reshape(-1)
    state_w = 4 * S5_STATE
    smem = pl.BlockSpec(memory_space=pltpu.SMEM)
    slab_lat = pl.BlockSpec((S5_GROUP_CH, lat_rows, SCAN_CHUNK), lambda g: (g, 0, 0))
    slab_ctx = pl.BlockSpec((S5_GROUP_CH, ctx_rows, SCAN_CHUNK), lambda g: (g, 0, 0))

    inject = pl.pallas_call(
        functools.partial(_s5_inject_kernel, class_starts=class_starts, d_model=d),
        grid=(groups,),
        in_specs=[smem, smem, slab_lat, slab_ctx,
                  pl.BlockSpec((1, S5_GROUP_CH * SCAN_CHUNK, state_w), lambda g: (g, 0, 0))],
        out_specs=pl.BlockSpec((rows, state_w), lambda g: (0, g)),
        out_shape=jax.ShapeDtypeStruct((rows, groups * state_w), F32),
        compiler_params=_params("parallel"),
        name="s5_inject",
    )(scale_flat, shift_flat, xt_lat, xt_ctx, p_in)

    gb = 16
    entering = pl.pallas_call(
        functools.partial(_s5_carry_kernel, bsz=bsz, ctx_chunks=ctx_chunks, lat_chunks=lat_chunks),
        grid=(groups // gb,),
        in_specs=[pl.BlockSpec((rows, gb, state_w), lambda g: (0, g, 0)),
                  pl.BlockSpec((gb, state_w), lambda g: (g, 0)),
                  pl.BlockSpec((gb, state_w), lambda g: (g, 0))],
        out_specs=pl.BlockSpec((rows, gb, state_w), lambda g: (0, g, 0)),
        out_shape=jax.ShapeDtypeStruct((rows, groups, state_w), F32),
        compiler_params=_params("parallel"),
        name="s5_carry",
    )(inject.reshape(rows, groups, state_w), carry_a, carry_b)

    yt_lat, yt_ctx = pl.pallas_call(
        functools.partial(_s5_output_kernel, class_starts=class_starts, d_model=d),
        grid=(groups,),
        in_specs=[smem, smem, slab_lat, slab_ctx,
                  pl.BlockSpec((1, S5_GROUP_CH, S5_GROUP_CH, 2 * SCAN_CHUNK), lambda g: (g, 0, 0, 0)),
                  pl.BlockSpec((1, S5_GROUP_CH, S5_GROUP_CH, 2 * SCAN_CHUNK),
                               lambda g: (jnp.minimum(g + 1, groups - 1), 0, 0, 0)),
                  pl.BlockSpec((rows, state_w), lambda g: (0, g)),
                  pl.BlockSpec((1, S5_GROUP_CH, state_w, SCAN_CHUNK), lambda g: (g, 0, 0, 0))],
        out_specs=[slab_lat, slab_ctx],
        out_shape=[jax.ShapeDtypeStruct((d, lat_rows, SCAN_CHUNK), F32),
                   jax.ShapeDtypeStruct((d, ctx_rows, SCAN_CHUNK), F32)],
        scratch_shapes=[pltpu.VMEM((S5_GROUP_CH * SCAN_CHUNK, S5_GROUP_CH * SCAN_CHUNK), BF16)] * 2,
        compiler_params=_params("arbitrary"),
        name="s5_output",
    )(scale_flat, shift_flat, xt_lat, xt_ctx, lags, lags, entering.reshape(rows, groups * state_w), p_out)

    return _from_chunk_major(yt_lat), _from_chunk_major(yt_ctx)


def kernel(x, c, ctx, c_ctx, mod_w, mod_b, ln_g, ln_b, router_w, router_b, moe_w1, moe_w3, moe_w2, s5_a_re, s5_a_im, s5_log_dt, s5_b_re, s5_b_im, s5_c_re, s5_c_im, s5_d, s5_w_gate, s5_w_val, swa_w_qkv, swa_w_o, swa_sink, gqa_w_qkv, gqa_w_o, gqa_q_norm, gqa_k_norm):
    bsz, seq, d = x.shape
    n_ctx = ctx.shape[1]
    depth = mod_w.shape[0]
    alpha = (2.0 * depth) ** 0.25
    heads_b, heads_c = d // HEAD_DIM_B, d // HEAD_DIM_C

    cond = jnp.concatenate([c_ctx[None, :], c], axis=0)
    mods = _modulation(cond, mod_w, mod_b).reshape(depth, bsz + 1, 6, d)
    xl = x.reshape(bsz * seq, d)
    xc = ctx.reshape(bsz * n_ctx, d)
    router_w_t = router_w.T
    rope_b = _rope_tables(seq, HEAD_DIM_B)
    rope_c = _rope_tables(seq, HEAD_DIM_C)

    for i in range(depth):
        kind, j = i % N_MIXERS, i // N_MIXERS
        ctx_out = i < depth - 1
        lat_mod = [mods[i, 1:, k][:, None, :] for k in range(6)]
        ctx_mod = [jnp.broadcast_to(mods[i, 0, k][None, None, :], (bsz, 1, d)) for k in range(6)]
        ln1 = (ln_g[i, 0], ln_b[i, 0])
        ln2 = (ln_g[i, 1], ln_b[i, 1])
        buffers = [(xl, lat_mod, True)] + ([(xc, ctx_mod, False)] if ctx_out else [])

        if kind == 0:
            tables = _s5_tables(s5_a_re[j], s5_a_im[j], s5_log_dt[j], s5_b_re[j], s5_b_im[j],
                                s5_c_re[j], s5_c_im[j])
            ctx_last = jnp.roll(mods[i], -1, axis=0)
            y_lat, y_ctx = _s5_scan(xl, xc, 1.0 + ctx_last[:, 1], ctx_last[:, 0], tables, bsz)
            glu_w = (s5_d[j], s5_w_val[j].astype(BF16), s5_w_gate[j].astype(BF16))
            mixed = {True: y_lat, False: y_ctx}
        else:
            if kind == 1:
                w_qkv, w_o = swa_w_qkv[j].astype(BF16), swa_w_o[j].astype(BF16)
                heads, kv, dh, norms, rope = heads_b, KV_HEADS_B, HEAD_DIM_B, None, rope_b
            else:
                w_qkv, w_o = gqa_w_qkv[j].astype(BF16), gqa_w_o[j].astype(BF16)
                heads, kv, dh, norms, rope = heads_c, KV_HEADS_C, HEAD_DIM_C, (gqa_q_norm[j], gqa_k_norm[j]), rope_c
            ql, kl, vl = _qkv_project(xl, lat_mod[1], lat_mod[0], w_qkv, heads, kv, dh, norms=norms, rope=rope)
            qc, kc, vc = _qkv_project(xc, ctx_mod[1], ctx_mod[0], w_qkv, heads, kv, dh, norms=norms, rope=None)
            if kind == 1:
                sink = swa_sink[j].astype(F32)
                o_lat = _window_attention(ql, kl, vl, kc, vc, sink, bsz, dh)
                o_ctx = _flash_attention(qc, kc, vc, bsz, dh, sink=sink) if ctx_out else None
            else:
                keys = jnp.concatenate([kl.reshape(kv, dh, bsz, seq), kc.reshape(kv, dh, bsz, n_ctx)], axis=3)
                vals = jnp.concatenate([vl.reshape(kv, bsz, seq, dh), vc.reshape(kv, bsz, n_ctx, dh)], axis=2)
                o_lat = _flash_attention(ql, keys.reshape(kv, dh, -1), vals.reshape(kv, -1, dh), bsz, dh)
                o_ctx = _flash_attention(qc, kc, vc, bsz, dh) if ctx_out else None
            mixed = {True: o_lat, False: o_ctx}

        w1, w3, w2 = moe_w1[i].astype(BF16), moe_w3[i].astype(BF16), moe_w2[i].astype(BF16)
        new = []
        for rows, mod, is_lat in buffers:
            sh1, sc1, g1, sh2, sc2, g2 = mod
            route = (sc2, sh2, router_w_t, router_b.astype(F32))
            if kind == 0:
                x1, h2, comb_t, grp = _post_mixer(mixed[is_lat], rows, g1, *ln1, alpha,
                                                  glu=(sc1, sh1) + glu_w, route=route)
            else:
                x1, h2, comb_t, grp = _post_mixer(mixed[is_lat], rows, g1, *ln1, alpha, w_o=w_o, route=route)
            new.append(_moe(h2, comb_t, grp, w1, w3, w2, x1, g2, *ln2, alpha))
        xl = new[0]
        if ctx_out:
            xc = new[1]
    return xl.reshape(bsz, seq, d)
```
